```python
import math
import jax, jax.numpy as jnp
from jax import lax
import numpy as np

D_MODEL = 1024
BATCH = 16
SEQ = 2048
DEPTH = 4

HEAD_DIM = 64
ROT_DIM = HEAD_DIM // 4
ROPE_THETA = 500000.0
BLOCK = 128
NORM_EPS = 1e-6

RWKV_HEADS = 6
RWKV_WIDTH = RWKV_HEADS * HEAD_DIM
DECAY_RANK = 64
ICL_RANK = 64
GN_EPS = 64e-5
DECAY_SCALE = math.exp(-0.5)

DSA_HEADS = 4
DSA_WIDTH = DSA_HEADS * HEAD_DIM
IDX_HEADS = 8
IDX_DIM = 32
IDX_ROT_DIM = IDX_DIM // 4
DSA_TOPK = 256

DIL_WINDOWS = (128, 512, 2048)
DIL_RATES = (1, 4, 16)
DIL_GROUPS = 3
DIL_HEADS_PER_GROUP = 2
DIL_HEADS = DIL_GROUPS * DIL_HEADS_PER_GROUP
DIL_WIDTH = DIL_HEADS * HEAD_DIM

MIX_WIDTH = RWKV_WIDTH + DSA_WIDTH + DIL_WIDTH
PLE_DIM = 256

RWKV_COLS = (RWKV_WIDTH, RWKV_WIDTH, RWKV_WIDTH, RWKV_WIDTH, DECAY_RANK, ICL_RANK)
RWKV_IN_WIDTH = 4 * RWKV_WIDTH + DECAY_RANK + ICL_RANK
DSA_COLS = (DSA_WIDTH, HEAD_DIM, HEAD_DIM, IDX_HEADS * IDX_DIM, IDX_DIM, IDX_HEADS, DSA_WIDTH)
DSA_IN_WIDTH = 2 * DSA_WIDTH + 2 * HEAD_DIM + IDX_HEADS * IDX_DIM + IDX_DIM + IDX_HEADS
DIL_COLS = (DIL_WIDTH, DIL_WIDTH, DIL_WIDTH, DIL_WIDTH)
DIL_IN_WIDTH = 4 * DIL_WIDTH
IN_COLS = RWKV_IN_WIDTH + DSA_IN_WIDTH + DIL_IN_WIDTH

kernel_name = 'hymba_rwkv7_dsa_dilated_trunk'


def rms_norm(x, g):
    xf = x.astype(jnp.float32)
    y = xf * lax.rsqrt(jnp.mean(xf * xf, axis=-1, keepdims=True) + NORM_EPS)
    return (y * g.astype(jnp.float32)).astype(x.dtype)


def split_cols(z, widths):
    out, off = [], 0
    for w in widths:
        out.append(z[..., off:off + w])
        off += w
    return out


def rope_tables(seq, rot_dim):
    half = rot_dim // 2
    inv = ROPE_THETA ** (-jnp.arange(half, dtype=jnp.float32) / half)
    ang = jnp.arange(seq, dtype=jnp.float32)[:, None] * inv[None, :]
    return jnp.cos(ang), jnp.sin(ang)


def partial_rope(x, cos, sin):
    half = cos.shape[-1]
    c = cos[None, :, None, :].astype(x.dtype)
    s = sin[None, :, None, :].astype(x.dtype)
    x1, x2, rest = x[..., :half], x[..., half:2 * half], x[..., 2 * half:]
    return jnp.concatenate([x1 * c - x2 * s, x2 * c + x1 * s, rest], axis=-1)


def token_shift_mix(z, mu):
    prev = jnp.pad(z, ((0, 0), (1, 0), (0, 0)))[:, :-1]
    return z + (prev - z) * mu


def rwkv7_branch(z, mu, w0, w_up, a0, a_up, k_k, k_a, r_k, ln_g, ln_b):
    B, S, _ = z.shape
    f32 = jnp.float32
    z = token_shift_mix(z, mu)
    r, k, v, g, wd, ad = split_cols(z, RWKV_COLS)
    heads = lambda t: t.astype(f32).reshape(B, S, RWKV_HEADS, HEAD_DIM)
    per_head = lambda t: t.astype(f32).reshape(RWKV_HEADS, HEAD_DIM)
    w = jnp.exp(-DECAY_SCALE * jax.nn.sigmoid((w0 + jnp.tanh(wd) @ w_up).astype(f32)))
    a = jax.nn.sigmoid((a0 + ad @ a_up).astype(f32))
    r, k, v, w, a = heads(r), heads(k), heads(v), heads(w), heads(a)
    kk = k * per_head(k_k)
    kk = kk / jnp.maximum(jnp.linalg.norm(kk, axis=-1, keepdims=True), 1e-12)
    k = k * (1.0 + (a - 1.0) * per_head(k_a))

    def step(state, inp):
        r_t, w_t, k_t, v_t, kk_t, a_t = inp
        sa = jnp.einsum('bhvk,bhk->bhv', state, -kk_t)
        state = (state * w_t[:, :, None, :]
                 + sa[..., None] * (kk_t * a_t)[:, :, None, :]
                 + v_t[..., None] * k_t[:, :, None, :])
        return state, jnp.einsum('bhvk,bhk->bhv', state, r_t)

    seq_first = lambda t: jnp.moveaxis(t, 1, 0)
    s0 = jnp.zeros((B, RWKV_HEADS, HEAD_DIM, HEAD_DIM), f32)
    _, y = lax.scan(step, s0, (seq_first(r), seq_first(w), seq_first(k), seq_first(v), seq_first(kk), seq_first(a)))
    y = jnp.moveaxis(y, 0, 1)
    mean = jnp.mean(y, axis=-1, keepdims=True)
    var = jnp.mean(jnp.square(y - mean), axis=-1, keepdims=True)
    y = ((y - mean) * lax.rsqrt(var + GN_EPS)).reshape(B, S, RWKV_WIDTH) * ln_g.astype(f32) + ln_b.astype(f32)
    bonus = jnp.sum(r * k * r_k.astype(f32), axis=-1, keepdims=True) * v
    y = y + bonus.reshape(B, S, RWKV_WIDTH)
    return (y * jax.nn.silu(g.astype(f32))).astype(z.dtype)


def dsa_branch(z, cos, sin, icos, isin):
    B, S, _ = z.shape
    f32 = jnp.float32
    q, k, v, iq, ik, iw, g = split_cols(z, DSA_COLS)
    q = partial_rope(q.reshape(B, S, DSA_HEADS, HEAD_DIM), cos, sin)
    k = partial_rope(k[:, :, None, :], cos, sin)[:, :, 0]
    iq = partial_rope(iq.reshape(B, S, IDX_HEADS, IDX_DIM), icos, isin)
    ik = partial_rope(ik[:, :, None, :], icos, isin)[:, :, 0]
    iw = iw * (IDX_HEADS ** -0.5)
    topk = min(DSA_TOPK, S // 4)
    nb = S // BLOCK
    key_pos = jnp.arange(S)
    gather = jax.vmap(lambda table, idx: table[idx])

    def block_fn(args):
        qb, iqb, iwb, start = args
        qpos = start + jnp.arange(BLOCK)
        sc = jnp.einsum('bqhd,bsd->bqhs', iqb, ik).astype(f32) * (IDX_DIM ** -0.5)
        sc = jnp.einsum('bqhs,bqh->bqs', jax.nn.relu(sc), iwb.astype(f32))
        sc = jnp.where((key_pos[None, :] <= qpos[:, None])[None], sc, -jnp.inf)
        _, idx = lax.top_k(sc, topk)
        valid = idx <= qpos[None, :, None]
        ks = gather(k, idx)
        vs = gather(v, idx)
        att = jnp.einsum('bqhd,bqkd->bqhk', qb, ks).astype(f32) * (HEAD_DIM ** -0.5)
        att = jnp.where(valid[:, :, None, :], att, -jnp.inf)
        probs = jax.nn.softmax(att, axis=-1).astype(vs.dtype)
        return jnp.einsum('bqhk,bqkd->bqhd', probs, vs)

    blocks = lambda t: jnp.moveaxis(t.reshape(B, nb, BLOCK, *t.shape[2:]), 1, 0)
    out = lax.map(block_fn, (blocks(q), blocks(iq), blocks(iw), jnp.arange(nb) * BLOCK))
    out = jnp.moveaxis(out, 0, 1).reshape(B, S, DSA_WIDTH)
    return out * jax.nn.silu(g)


def banded_attention(q, k, v, max_steps):
    N, n, H, dh = q.shape
    nb = -(-n // BLOCK)
    pad = nb * BLOCK - n
    to_blocks = lambda t: jnp.pad(t, ((0, 0), (0, pad), (0, 0), (0, 0))).reshape(N, nb, BLOCK, H, dh)
    qb, kb, vb = to_blocks(q), to_blocks(k), to_blocks(v)
    prev = lambda t: jnp.concatenate([jnp.zeros_like(t[:, :1]), t[:, :-1]], axis=1)
    kc = jnp.concatenate([prev(kb), kb], axis=2)
    vc = jnp.concatenate([prev(vb), vb], axis=2)
    s = jnp.einsum('nbqhd,nbkhd->nbhqk', qb, kc).astype(jnp.float32) * (HEAD_DIM ** -0.5)
    qi = jnp.arange(BLOCK)[:, None] + BLOCK
    kj = jnp.arange(2 * BLOCK)[None, :]
    dist = qi - kj
    band = (dist >= 0) & (dist <= max_steps)
    has_prev = (jnp.arange(nb)[:, None, None] > 0) | (kj[None] >= BLOCK)
    mask = band[None] & has_prev
    s = jnp.where(mask[None, :, None], s, -jnp.inf)
    m = jnp.max(s, axis=-1, keepdims=True)
    e = jnp.exp(s - m)
    l = jnp.sum(e, axis=-1, keepdims=True)
    o = jnp.einsum('nbhqk,nbkhd->nbqhd', (e / l).astype(v.dtype), vc)
    lse = (m + jnp.log(l))[..., 0]
    o = o.reshape(N, nb * BLOCK, H, dh)[:, :n]
    lse = jnp.moveaxis(lse, 2, 3).reshape(N, nb * BLOCK, H)[:, :n]
    return o, lse


def dilated_group(q, k, v, rate, steps):
    B, S, H, dh = q.shape
    n = S // rate
    to_res = lambda t: jnp.swapaxes(t.reshape(B, n, rate, H, dh), 1, 2).reshape(B * rate, n, H, dh)
    o, lse = banded_attention(to_res(q), to_res(k), to_res(v), steps)
    o = jnp.swapaxes(o.reshape(B, rate, n, H, dh), 1, 2).reshape(B, S, H, dh)
    lse = jnp.swapaxes(lse.reshape(B, rate, n, H), 1, 2).reshape(B, S, H)
    return o, lse


def dilated_branch(z, cos, sin):
    B, S, _ = z.shape
    q, k, v, g = split_cols(z, DIL_COLS)
    heads = lambda t: t.reshape(B, S, DIL_HEADS, HEAD_DIM)
    q = partial_rope(heads(q), cos, sin)
    k = partial_rope(heads(k), cos, sin)
    v = heads(v)
    outs, lses = [], []
    for gi in range(DIL_GROUPS):
        hs = slice(gi * DIL_HEADS_PER_GROUP, (gi + 1) * DIL_HEADS_PER_GROUP)
        o, lse = dilated_group(q[:, :, hs], k[:, :, hs], v[:, :, hs], DIL_RATES[gi], DIL_WINDOWS[gi] // DIL_RATES[gi])
        outs.append(o)
        lses.append(lse)
    o = jnp.stack(outs, axis=2)
    alpha = jax.nn.softmax(jnp.stack(lses, axis=2), axis=2)
    o = (o * alpha[..., None].astype(o.dtype)).reshape(B, S, DIL_WIDTH)
    return o * jax.nn.silu(g)


def setup_inputs(seed: int = 0) -> dict:
    key = jax.random.key(seed)
    ks = jax.random.split(key, 20)
    f32 = jnp.float32
    nrm = lambda kk, shape, scale: scale * jax.random.normal(kk, shape, f32)
    L = DEPTH
    return {
        'x': nrm(ks[0], (BATCH, SEQ, D_MODEL), 1.0),
        'p': nrm(ks[1], (DEPTH, BATCH, SEQ, PLE_DIM), 1.0),
        'norm_g': 1.0 + nrm(ks[2], (L, D_MODEL), 0.05),
        'w_in': nrm(ks[3], (L, D_MODEL, IN_COLS), D_MODEL ** -0.5),
        'tshift_mu': jax.random.uniform(ks[4], (L, RWKV_IN_WIDTH), f32),
        'rwkv_w0': jax.random.uniform(ks[5], (L, RWKV_WIDTH), f32, -6.0, 1.0),
        'rwkv_w_up': nrm(ks[6], (L, DECAY_RANK, RWKV_WIDTH), DECAY_RANK ** -0.5),
        'rwkv_a0': nrm(ks[7], (L, RWKV_WIDTH), 0.5),
        'rwkv_a_up': nrm(ks[8], (L, ICL_RANK, RWKV_WIDTH), ICL_RANK ** -0.5),
        'rwkv_k_k': 0.85 + nrm(ks[9], (L, RWKV_WIDTH), 0.05),
        'rwkv_k_a': 1.0 + nrm(ks[10], (L, RWKV_WIDTH), 0.05),
        'rwkv_r_k': nrm(ks[11], (L, RWKV_HEADS, HEAD_DIM), 0.1),
        'rwkv_ln_g': 1.0 + nrm(ks[12], (L, RWKV_WIDTH), 0.05),
        'rwkv_ln_b': nrm(ks[13], (L, RWKV_WIDTH), 0.01),
        'w_out': nrm(ks[14], (L, MIX_WIDTH, D_MODEL), 0.5 * MIX_WIDTH ** -0.5),
        'ple_norm_g': 1.0 + nrm(ks[15], (L, D_MODEL), 0.05),
        'ple_w_gate': nrm(ks[16], (L, D_MODEL, D_MODEL), D_MODEL ** -0.5),
        'ple_w_proj': nrm(ks[17], (L, PLE_DIM, D_MODEL), 0.5 * PLE_DIM ** -0.5),
        'final_norm_g': 1.0 + nrm(ks[18], (D_MODEL,), 0.05),
    }


def reference(x, p, norm_g, w_in, tshift_mu, rwkv_w0, rwkv_w_up, rwkv_a0, rwkv_a_up, rwkv_k_k, rwkv_k_a,
              rwkv_r_k, rwkv_ln_g, rwkv_ln_b, w_out, ple_norm_g, ple_w_gate, ple_w_proj, final_norm_g):
    S = x.shape[1]
    cos, sin = rope_tables(S, ROT_DIM)
    icos, isin = rope_tables(S, IDX_ROT_DIM)
    for i in range(DEPTH):
        h = rms_norm(x, norm_g[i])
        z = h @ w_in[i]
        z_a, z_b, z_c = split_cols(z, (RWKV_IN_WIDTH, DSA_IN_WIDTH, DIL_IN_WIDTH))
        y_a = rwkv7_branch(z_a, tshift_mu[i], rwkv_w0[i], rwkv_w_up[i], rwkv_a0[i], rwkv_a_up[i],
                           rwkv_k_k[i], rwkv_k_a[i], rwkv_r_k[i], rwkv_ln_g[i], rwkv_ln_b[i])
        y_b = dsa_branch(z_b, cos, sin, icos, isin)
        y_c = dilated_branch(z_c, cos, sin)
        y = jnp.concatenate([y_a, y_b, y_c], axis=-1)
        x = x + y @ w_out[i]
        gate = jax.nn.sigmoid(rms_norm(x, ple_norm_g[i]) @ ple_w_gate[i])
        x = x + gate * (p[i] @ ple_w_proj[i])
    return rms_norm(x, final_norm_g)
```

```python
import functools
import math

import jax
import jax.numpy as jnp
from jax import lax
from jax.experimental import pallas as pl
from jax.experimental.pallas import tpu as pltpu

F32 = jnp.float32
BF16 = jnp.bfloat16
I32 = jnp.int32

HEAD_DIM = 64
ROPE_THETA = 500000.0
NORM_EPS = 1e-6
RWKV_HEADS = 6
RWKV_WIDTH = RWKV_HEADS * HEAD_DIM
LORA_RANK = 64
GN_EPS = 64e-5
DECAY_SCALE = math.exp(-0.5)
DSA_HEADS = 4
DSA_WIDTH = DSA_HEADS * HEAD_DIM
IDX_HEADS = 8
IDX_DIM = 32
DSA_TOPK = 256
DIL_RATES = (1, 4, 16)
DIL_STEPS = 128
DIL_WIDTH = 384
ATT_BLOCK = 128

LANES = 128
VMEM_LIMIT = 48 * 1024 * 1024

U_RW = 0
U_DL = 12
U_DS_Q = 24
U_DS_G = 26
U_DS_IQ = 28
U_DS_KV = 30
U_DS_MISC = 31
U_RW_WA = 32
Z_UNITS = 33
Z_COLS = Z_UNITS * LANES

RWKV_CHUNK = 128
INV_BASE = 16
NEG_BIG = -1e30
INT_MIN = -2 ** 31


def _dot(a, b):
    return jnp.dot(a, b, preferred_element_type=F32)


def _dot_nt(a, b):
    return lax.dot_general(a, b, (((1,), (1,)), ((), ())), preferred_element_type=F32)


def _split(x):
    hi = x.astype(BF16)
    lo = (x - hi.astype(F32)).astype(BF16)
    return hi, lo


def _dot_x2(x, w_bf16):
    hi, lo = _split(x)
    return _dot(hi, w_bf16) + _dot(lo, w_bf16)


def _dot_x3(x, w):
    xh, xl = _split(x)
    wh, wl = _split(w)
    return _dot(xh, wh) + (_dot(xh, wl) + _dot(xl, wh))


def _sigmoid(x):
    return 1.0 / (1.0 + jnp.exp(-x))


def _rope(x, cos_t, sin_t, half, head):
    lane = lax.broadcasted_iota(I32, x.shape, 1) % head
    partner = jnp.where(lane < half, pltpu.roll(x, LANES - half, 1), pltpu.roll(x, half, 1))
    return x * cos_t + partner * sin_t


def _in_proj_kernel(x_ref, g_ref, w_ref, z_ref, h_ref):
    @pl.when(pl.program_id(1) == 0)
    def _():
        x = x_ref[...]
        ms = jnp.mean(x * x, axis=-1, keepdims=True)
        h_ref[...] = (x * lax.rsqrt(ms + NORM_EPS) * g_ref[...]).astype(BF16)

    z_ref[...] = _dot(h_ref[...], w_ref[...])


def _in_proj(x2d, g, w, tm=1024, tn=384):
    n, d = x2d.shape
    return pl.pallas_call(
        _in_proj_kernel,
        grid=(n // tm, Z_COLS // tn),
        in_specs=[
            pl.BlockSpec((tm, d), lambda i, j: (i, 0)),
            pl.BlockSpec((1, d), lambda i, j: (0, 0)),
            pl.BlockSpec((d, tn), lambda i, j: (0, j)),
        ],
        out_specs=pl.BlockSpec((tm, tn), lambda i, j: (i, j)),
        out_shape=jax.ShapeDtypeStruct((n, Z_COLS), F32),
        scratch_shapes=[pltpu.VMEM((tm, d), BF16)],
        compiler_params=pltpu.CompilerParams(
            dimension_semantics=("parallel", "arbitrary"), vmem_limit_bytes=VMEM_LIMIT),
        name="in_proj",
    )(x2d, g, w)


def _out_proj_kernel(x_ref, ya_ref, yb_ref, yc_ref, p_ref, wa_ref, wb_ref, wc_ref, ng_ref, wg_ref,
                     wp_ref, fg_ref, o_ref, *, final):
    acc = x_ref[...] + _dot(ya_ref[...], wa_ref[...])
    acc = acc + _dot(yb_ref[...], wb_ref[...])
    acc = acc + _dot(yc_ref[...], wc_ref[...])
    ms = jnp.mean(acc * acc, axis=-1, keepdims=True)
    hn = (acc * lax.rsqrt(ms + NORM_EPS) * ng_ref[...]).astype(BF16)
    gate = _sigmoid(_dot(hn, wg_ref[...]))
    out = acc + gate * _dot(p_ref[...].astype(BF16), wp_ref[...])
    if final:
        ms = jnp.mean(out * out, axis=-1, keepdims=True)
        out = out * lax.rsqrt(ms + NORM_EPS) * fg_ref[...]
    o_ref[...] = out


def _out_proj(x2d, ya, yb, yc, p2d, wa, wb, wc, ng, wg, wp, fg, final, tm=512):
    n, d = x2d.shape
    row = lambda w: pl.BlockSpec((tm, w), lambda i: (i, 0))
    full = lambda a: pl.BlockSpec(a.shape, lambda i: (0, 0))
    return pl.pallas_call(
        functools.partial(_out_proj_kernel, final=final),
        grid=(n // tm,),
        in_specs=[row(d), row(ya.shape[1]), row(yb.shape[1]), row(yc.shape[1]), row(p2d.shape[1]),
                  full(wa), full(wb), full(wc), full(ng), full(wg), full(wp), full(fg)],
        out_specs=row(d),
        out_shape=jax.ShapeDtypeStruct((n, d), F32),
        compiler_params=pltpu.CompilerParams(
            dimension_semantics=("parallel",), vmem_limit_bytes=VMEM_LIMIT),
        name="out_proj",
    )(x2d, ya, yb, yc, p2d, wa, wb, wc, ng, wg, wp, fg)


def _unit_lower_inverse(n_strict, row, col):
    c = n_strict.shape[0]
    eye = jnp.where(row == col, 1.0, 0.0).astype(F32)
    same = (row // INV_BASE) == (col // INV_BASE)
    nb = jnp.where(same, n_strict, 0.0)
    t = eye - nb
    pw = nb
    span = 2
    while span < INV_BASE:
        pwb = pw.astype(BF16)
        pw = _dot(pwb, pwb)
        t = t + _dot(t.astype(BF16), pw.astype(BF16))
        span *= 2
    size = INV_BASE
    while size < c:
        lower_left = ((row // (2 * size)) == (col // (2 * size))) & ((row // size) != (col // size))
        e = jnp.where(lower_left, n_strict, 0.0).astype(BF16)
        tb = t.astype(BF16)
        t = t - _dot(tb, _dot(e, tb).astype(BF16))
        size *= 2
    return t


def _rwkv_kernel(zr_ref, zk_ref, zv_ref, zg_ref, zwa_ref, mu_ref, muwa_ref, w0_ref, wup_ref, a0_ref,
                 aup_ref, kk_ref, ka_ref, rk_ref, lng_ref, lnb_ref, o_ref,
                 state_ref, prev_ref, prevwa_ref, y_ref):
    c = zr_ref.shape[0]
    width = RWKV_WIDTH

    @pl.when(pl.program_id(1) == 0)
    def _():
        state_ref[...] = jnp.zeros_like(state_ref)
        prev_ref[...] = jnp.zeros_like(prev_ref)
        prevwa_ref[...] = jnp.zeros_like(prevwa_ref)

    first_row = lax.broadcasted_iota(I32, (c, width), 0) == 0

    def shifted(z_ref, idx):
        z = z_ref[...]
        prev = jnp.where(first_row, prev_ref[idx, 0:1, :], pltpu.roll(z, 1, 0))
        prev_ref[idx, 0:1, :] = z[c - 1:c, :]
        return z + (prev - z) * mu_ref[idx:idx + 1, :]

    r = shifted(zr_ref, 0)
    k = shifted(zk_ref, 1)
    v = shifted(zv_ref, 2)
    g = shifted(zg_ref, 3)
    zwa = zwa_ref[...]
    first_row_wa = lax.broadcasted_iota(I32, (c, LANES), 0) == 0
    prev_wa = jnp.where(first_row_wa, prevwa_ref[0:1, :], pltpu.roll(zwa, 1, 0))
    prevwa_ref[0:1, :] = zwa[c - 1:c, :]
    wa = zwa + (prev_wa - zwa) * muwa_ref[...]
    wd = wa[:, :LORA_RANK]
    ad = wa[:, LORA_RANK:]

    lw = -DECAY_SCALE * _sigmoid(w0_ref[...] + _dot_x3(jnp.tanh(wd), wup_ref[...]))
    a = _sigmoid(a0_ref[...] + _dot_x3(ad, aup_ref[...]))
    hrow = lax.broadcasted_iota(I32, (width, width), 0) // HEAD_DIM
    hcol = lax.broadcasted_iota(I32, (width, width), 1) // HEAD_DIM
    head_ones = jnp.where(hrow == hcol, 1.0, 0.0).astype(BF16)
    kk = k * kk_ref[...]
    norm = jnp.sqrt(_dot_x2(kk * kk, head_ones))
    kk = kk / jnp.maximum(norm, 1e-12)
    kmod = k * (1.0 + (a - 1.0) * ka_ref[...])
    b = a * kk

    row = lax.broadcasted_iota(I32, (c, c), 0)
    col = lax.broadcasted_iota(I32, (c, c), 1)
    tri = jnp.where(col <= row, 1.0, 0.0).astype(BF16)
    lw_hi, lw_lo = _split(lw)
    cl = _dot(tri, lw_hi) + _dot(tri, lw_lo)
    c_mid = cl[c // 2 - 1:c // 2, :]
    c_end = cl[c - 1:c, :]
    w = jnp.exp(lw)
    e_in1 = jnp.exp(cl - lw)
    e_in2 = e_in1 * w
    e_k = jnp.exp(c_mid - cl)
    e_mid_inv = jnp.exp(-c_mid)
    q1 = (kk * e_in1 * e_mid_inv).astype(BF16)
    q2 = (r * e_in2 * e_mid_inv).astype(BF16)
    k1 = (b * e_k).astype(BF16)
    k2 = (kmod * e_k).astype(BF16)
    kkd = (kk * e_in1).astype(BF16)
    rd = (r * e_in2).astype(BF16)
    e_end = e_k * jnp.exp(c_end - c_mid)
    bd = (b * e_end).astype(BF16)
    kd = (kmod * e_end).astype(BF16)
    d_end = jnp.exp(c_end)
    vb = v.astype(BF16)

    strict = col < row
    incl = col <= row
    for h in range(RWKV_HEADS):
        sl = slice(h * HEAD_DIM, (h + 1) * HEAD_DIM)
        a_ab = jnp.where(strict, _dot_nt(q1[:, sl], k1[:, sl]), 0.0)
        a_ak = jnp.where(strict, _dot_nt(q1[:, sl], k2[:, sl]), 0.0)
        a_rb = jnp.where(incl, _dot_nt(q2[:, sl], k1[:, sl]), 0.0)
        a_rk = jnp.where(incl, _dot_nt(q2[:, sl], k2[:, sl]), 0.0)
        t_inv = _unit_lower_inverse(a_ab, row, col)
        s_in = state_ref[h].astype(BF16)
        vh = vb[:, sl]
        rhs = _dot_nt(kkd[:, sl], s_in) + _dot(a_ak.astype(BF16), vh)
        u = -_dot(t_inv.astype(BF16), rhs.astype(BF16))
        ub = u.astype(BF16)
        y = _dot_nt(rd[:, sl], s_in) + _dot(a_rb.astype(BF16), ub) + _dot(a_rk.astype(BF16), vh)
        y_ref[:, sl] = y
        uv_t = jnp.concatenate([u, v[:, sl]], axis=1).T.astype(BF16)
        state_ref[h] = (state_ref[h] * d_end[:, sl]
                        + _dot(uv_t[:HEAD_DIM, :], bd[:, sl]) + _dot(uv_t[HEAD_DIM:, :], kd[:, sl]))

    y = y_ref[...]
    inv_n = 1.0 / HEAD_DIM
    mean = _dot_x2(y, head_ones) * inv_n
    dlt = y - mean
    var = _dot_x2(dlt * dlt, head_ones) * inv_n
    yn = dlt * lax.rsqrt(var + GN_EPS) * lng_ref[...] + lnb_ref[...]
    bonus = _dot_x2(r * kmod * rk_ref[...], head_ones) * v
    o_ref[...] = ((yn + bonus) * (g * _sigmoid(g))).astype(o_ref.dtype)


def _rwkv(z2d, batch, seq, mu4, mu_wa, w0, w_up, a0, a_up, k_k, k_a, r_k, ln_g, ln_b):
    c = RWKV_CHUNK
    nc = seq // c
    zspec = lambda u: pl.BlockSpec((c, RWKV_WIDTH), lambda b, i, u=u: (b * nc + i, u))
    full = lambda a: pl.BlockSpec(a.shape, lambda b, i: (0,) * a.ndim)
    params = (mu4, mu_wa, w0, w_up, a0, a_up, k_k, k_a, r_k, ln_g, ln_b)
    return pl.pallas_call(
        _rwkv_kernel,
        grid=(batch, nc),
        in_specs=[zspec(0), zspec(1), zspec(2), zspec(3),
                  pl.BlockSpec((c, LANES), lambda b, i: (b * nc + i, U_RW_WA))]
                 + [full(a) for a in params],
        out_specs=pl.BlockSpec((c, RWKV_WIDTH), lambda b, i: (b * nc + i, 0)),
        out_shape=jax.ShapeDtypeStruct((batch * seq, RWKV_WIDTH), BF16),
        scratch_shapes=[
            pltpu.VMEM((RWKV_HEADS, HEAD_DIM, HEAD_DIM), F32),
            pltpu.VMEM((4, 8, RWKV_WIDTH), F32),
            pltpu.VMEM((8, LANES), F32),
            pltpu.VMEM((c, RWKV_WIDTH), F32),
        ],
        compiler_params=pltpu.CompilerParams(
            dimension_semantics=("parallel", "arbitrary"), vmem_limit_bytes=VMEM_LIMIT),
        name="rwkv7",
    )(z2d, z2d, z2d, z2d, z2d, *params)


def _dsa_kernel(zq_ref, zg_ref, ziq_ref, zkv_ref, zms_ref, c64_ref, s64_ref, c32_ref, s32_ref, o_ref,
                k_ref, vt_ref, ik_ref, key_ref, bias_ref):
    seq = zkv_ref.shape[0]
    blk = ATT_BLOCK
    nkb = seq // blk
    qb = pl.program_id(1)
    lane = lax.broadcasted_iota(I32, (blk, LANES), 1)

    @pl.when(qb == 0)
    def _():
        def body(t, carry):
            rows = pl.ds(pl.multiple_of(t * blk, blk), blk)
            kv = zkv_ref[rows, :]
            kvr = jnp.where(lane < HEAD_DIM, _rope(kv, c64_ref[rows, :], s64_ref[rows, :], 8, HEAD_DIM), kv)
            k_ref[rows, :] = kvr[:, :HEAD_DIM].astype(BF16)
            vt_ref[t] = kvr.T[HEAD_DIM:, :].astype(BF16)
            ms = zms_ref[rows, :]
            ikr = _rope(ms, c32_ref[rows, :], s32_ref[rows, :], 4, IDX_DIM)
            ik_ref[rows, :] = ikr[:, :IDX_DIM].astype(BF16)
            return carry

        lax.fori_loop(0, nkb, body, 0)

    r0 = pl.multiple_of(qb * blk, blk)
    qrows = pl.ds(r0, blk)
    c64 = c64_ref[qrows, :]
    s64 = s64_ref[qrows, :]
    c32 = c32_ref[qrows, :]
    s32 = s32_ref[qrows, :]
    q = jnp.concatenate(
        [_rope(zq_ref[:, i * LANES:(i + 1) * LANES], c64, s64, 8, HEAD_DIM) for i in range(2)],
        axis=1) * (HEAD_DIM ** -0.5)
    q = q.astype(BF16)
    iq = jnp.concatenate(
        [_rope(ziq_ref[:, i * LANES:(i + 1) * LANES], c32, s32, 4, IDX_DIM) for i in range(2)],
        axis=1).astype(BF16)
    iw_t = zms_ref[qrows, :].T[IDX_DIM:IDX_DIM + IDX_HEADS, :] * ((IDX_HEADS * IDX_DIM) ** -0.5)

    kpos_l = lax.broadcasted_iota(I32, (blk, blk), 0)
    qpos = r0 + lax.broadcasted_iota(I32, (blk, blk), 1)

    def score_body(kb, carry):
        k0 = pl.multiple_of(kb * blk, blk)
        ikb = ik_ref[pl.ds(k0, blk), :]
        sc = jnp.zeros((blk, blk), F32)
        for h in range(IDX_HEADS):
            s = _dot_nt(ikb, iq[:, h * IDX_DIM:(h + 1) * IDX_DIM])
            sc = sc + jnp.maximum(s, 0.0) * iw_t[h:h + 1, :]
        sc = jnp.where(k0 + kpos_l <= qpos, sc, -jnp.inf)
        bits = pltpu.bitcast(sc, I32)
        key_ref[pl.ds(k0, blk), :] = bits ^ ((bits >> 31) & 0x7FFFFFFF)
        return carry

    lax.fori_loop(0, qb + 1, score_body, 0)

    def count(pred_fn):
        def body(kb, acc):
            keys = key_ref[pl.ds(pl.multiple_of(kb * blk, blk), blk), :]
            hit = jnp.where(pred_fn(keys), 1, 0).astype(I32)
            return acc + hit.reshape(blk // 8, 8, blk).sum(axis=0)

        acc = lax.fori_loop(0, qb + 1, body, jnp.zeros((8, blk), I32))
        return acc.sum(axis=0, keepdims=True)

    def bit_body(i, lo):
        cand = lo + (jnp.int32(1) << (31 - i))
        return jnp.where(count(lambda keys: keys >= cand) >= DSA_TOPK, cand, lo)

    lo0 = jnp.full((1, blk), INT_MIN, I32)
    thr = lax.cond(qb * blk >= DSA_TOPK, lambda: lax.fori_loop(0, 32, bit_body, lo0), lambda: lo0)
    need = (DSA_TOPK - count(lambda keys: keys > thr)).astype(F32)

    tri = jnp.where(lax.broadcasted_iota(I32, (blk, blk), 1) <= kpos_l, 1.0, 0.0).astype(BF16)

    def select_body(kb, run):
        k0 = pl.multiple_of(kb * blk, blk)
        keys = key_ref[pl.ds(k0, blk), :]
        eq = keys == thr
        rank = _dot(tri, jnp.where(eq, 1.0, 0.0).astype(BF16)) + run
        take = (keys > thr) | (eq & (rank <= need))
        take = take & (k0 + kpos_l <= qpos)
        bias_ref[pl.ds(k0, blk), :] = jnp.where(take, 0.0, NEG_BIG)
        return rank[blk - 1:blk, :]

    lax.fori_loop(0, qb + 1, select_body, jnp.zeros((1, blk), F32))

    def att_body(kb, carry):
        k0 = pl.multiple_of(kb * blk, blk)
        kblk = k_ref[pl.ds(k0, blk), :]
        vt = vt_ref[kb]
        bias = bias_ref[pl.ds(k0, blk), :]
        keep = bias == 0.0
        out = []
        for h in range(DSA_HEADS):
            m, l, acc = carry[h]
            s = _dot_nt(kblk, q[:, h * HEAD_DIM:(h + 1) * HEAD_DIM]) + bias
            m_new = jnp.maximum(m, jnp.max(s, axis=0, keepdims=True))
            p = jnp.where(keep, jnp.exp(s - m_new), 0.0)
            alpha = jnp.exp(m - m_new)
            l = alpha * l + jnp.sum(p, axis=0, keepdims=True)
            acc = alpha * acc + _dot(vt, p.astype(BF16))
            out.append((m_new, l, acc))
        return tuple(out)

    init = tuple((jnp.full((1, blk), NEG_BIG, F32), jnp.zeros((1, blk), F32),
                  jnp.zeros((HEAD_DIM, blk), F32)) for _ in range(DSA_HEADS))
    res = lax.fori_loop(0, qb + 1, att_body, init)
    heads = [acc / l for (_, l, acc) in res]
    o = jnp.concatenate([jnp.concatenate(heads[0:2], axis=0).T, jnp.concatenate(heads[2:4], axis=0).T], axis=1)
    g = zg_ref[...]
    o_ref[...] = (o * (g * _sigmoid(g))).astype(o_ref.dtype)


def _dsa(z2d, batch, seq, c64, s64, c32, s32):
    blk = ATT_BLOCK
    nq = seq // blk
    qspec = lambda u2: pl.BlockSpec((blk, 2 * LANES), lambda b, i, u2=u2: (b * nq + i, u2))
    seqspec = lambda u: pl.BlockSpec((seq, LANES), lambda b, i, u=u: (b, u))
    tab = pl.BlockSpec((seq, LANES), lambda b, i: (0, 0))
    return pl.pallas_call(
        _dsa_kernel,
        grid=(batch, nq),
        in_specs=[qspec(U_DS_Q // 2), qspec(U_DS_G // 2), qspec(U_DS_IQ // 2),
                  seqspec(U_DS_KV), seqspec(U_DS_MISC), tab, tab, tab, tab],
        out_specs=pl.BlockSpec((blk, DSA_WIDTH), lambda b, i: (b * nq + i, 0)),
        out_shape=jax.ShapeDtypeStruct((batch * seq, DSA_WIDTH), BF16),
        scratch_shapes=[
            pltpu.VMEM((seq, HEAD_DIM), BF16),
            pltpu.VMEM((seq // blk, HEAD_DIM, blk), BF16),
            pltpu.VMEM((seq, IDX_DIM), BF16),
            pltpu.VMEM((seq, blk), I32),
            pltpu.VMEM((seq, blk), F32),
        ],
        compiler_params=pltpu.CompilerParams(
            dimension_semantics=("parallel", "arbitrary"), vmem_limit_bytes=VMEM_LIMIT),
        name="dsa",
    )(z2d, z2d, z2d, z2d, z2d, c64, s64, c32, s32)


def _dil_group(gi, rate, q_ref, k_ref, v_ref, g_ref, c64_ref, s64_ref, qr_ref, kr_ref, o_s, lse_s):
    seq = q_ref.shape[0]
    blk = ATT_BLOCK
    nrow = seq // blk

    def rope_body(t, carry):
        rows = pl.ds(pl.multiple_of(t * blk, blk), blk)
        ct = c64_ref[rows, :]
        st = s64_ref[rows, :]
        qr_ref[rows, :] = _rope(q_ref[rows, :], ct, st, 8, HEAD_DIM) * (HEAD_DIM ** -0.5)
        kr_ref[rows, :] = _rope(k_ref[rows, :], ct, st, 8, HEAD_DIM)
        return carry

    lax.fori_loop(0, nrow, rope_body, 0)

    nblk = seq // (blk * rate)
    qi = lax.broadcasted_iota(I32, (blk, 2 * blk), 0)
    kj = lax.broadcasted_iota(I32, (blk, 2 * blk), 1)
    band = (kj >= qi) & (kj <= qi + DIL_STEPS)

    def strided(ref, start):
        if rate == 1:
            return ref[pl.ds(start, blk), :]
        return ref[pl.ds(start, blk, stride=rate), :]

    def body(cidx, carry):
        rho = cidx // nblk
        j = cidx % nblk
        start = j * (blk * rate) + rho
        pstart = jnp.maximum(j - 1, 0) * (blk * rate) + rho
        qblk = strided(qr_ref, start).astype(BF16)
        kc = jnp.concatenate([strided(kr_ref, pstart), strided(kr_ref, start)], axis=0).astype(BF16)
        vc = jnp.concatenate([strided(v_ref, pstart), strided(v_ref, start)], axis=0).astype(BF16)
        mask = band & ((kj >= blk) | (j > 0))
        outs, lses = [], []
        for h in range(2):
            sl = slice(h * HEAD_DIM, (h + 1) * HEAD_DIM)
            s = jnp.where(mask, _dot_nt(qblk[:, sl], kc[:, sl]), -jnp.inf)
            m = jnp.max(s, axis=-1, keepdims=True)
            e = jnp.exp(s - m)
            l = jnp.sum(e, axis=-1, keepdims=True)
            outs.append(_dot((e / l).astype(BF16), vc[:, sl]))
            lses.append(jnp.broadcast_to(m + jnp.log(l), (blk, HEAD_DIM)))
        gate = strided(g_ref, start)
        o = jnp.concatenate(outs, axis=1) * (gate * _sigmoid(gate))
        lse = jnp.concatenate(lses, axis=1)
        if rate == 1:
            o_s[gi, pl.ds(start, blk), :] = o
            lse_s[gi, pl.ds(start, blk), :] = lse
        else:
            o_s[gi, pl.ds(start, blk, stride=rate), :] = o
            lse_s[gi, pl.ds(start, blk, stride=rate), :] = lse
        return carry

    lax.fori_loop(0, rate * nblk, body, 0)


def _dil_kernel(q_ref, k_ref, v_ref, g_ref, c64_ref, s64_ref, o_ref, qr_ref, kr_ref, o_s, lse_s):
    grp = pl.program_id(1)
    for gi, rate in enumerate(DIL_RATES):
        pl.when(grp == gi)(functools.partial(
            _dil_group, gi, rate, q_ref, k_ref, v_ref, g_ref, c64_ref, s64_ref, qr_ref, kr_ref, o_s, lse_s))

    @pl.when(grp == len(DIL_RATES) - 1)
    def _():
        blk = ATT_BLOCK

        def body(t, carry):
            rows = pl.ds(pl.multiple_of(t * blk, blk), blk)
            ls = [lse_s[i, rows, :] for i in range(3)]
            top = jnp.maximum(jnp.maximum(ls[0], ls[1]), ls[2])
            es = [jnp.exp(x - top) for x in ls]
            inv = 1.0 / (es[0] + es[1] + es[2])
            o_ref[rows, :] = jnp.concatenate(
                [o_s[i, rows, :] * (es[i] * inv) for i in range(3)], axis=1).astype(o_ref.dtype)
            return carry

        lax.fori_loop(0, q_ref.shape[0] // blk, body, 0)


def _dilated(z2d, batch, seq, c64, s64):
    ng = len(DIL_RATES)
    zspec = lambda base: pl.BlockSpec((seq, LANES), lambda b, g, base=base: (b, base + g))
    tab = pl.BlockSpec((seq, LANES), lambda b, g: (0, 0))
    return pl.pallas_call(
        _dil_kernel,
        grid=(batch, ng),
        in_specs=[zspec(U_DL), zspec(U_DL + 3), zspec(U_DL + 6), zspec(U_DL + 9), tab, tab],
        out_specs=pl.BlockSpec((seq, DIL_WIDTH), lambda b, g: (b, 0)),
        out_shape=jax.ShapeDtypeStruct((batch * seq, DIL_WIDTH), BF16),
        scratch_shapes=[
            pltpu.VMEM((seq, LANES), F32),
            pltpu.VMEM((seq, LANES), F32),
            pltpu.VMEM((ng, seq, LANES), F32),
            pltpu.VMEM((ng, seq, LANES), F32),
        ],
        compiler_params=pltpu.CompilerParams(
            dimension_semantics=("parallel", "arbitrary"), vmem_limit_bytes=VMEM_LIMIT),
        name="dilated",
    )(z2d, z2d, z2d, z2d, c64, s64)


def _rope_tables(seq, head, half):
    inv = ROPE_THETA ** (-jnp.arange(half, dtype=F32) / half)
    ang = jnp.arange(seq, dtype=F32)[:, None] * inv[None, :]
    cos, sin = jnp.cos(ang), jnp.sin(ang)
    lane = jnp.arange(LANES) % head
    idx = lane % half
    cos_t = jnp.where(lane[None, :] < 2 * half, cos[:, idx], 1.0)
    sin_t = jnp.where(lane[None, :] < half, -sin[:, idx], jnp.where(lane[None, :] < 2 * half, sin[:, idx], 0.0))
    return cos_t, sin_t


def _permute_in_cols(w):
    rw = RWKV_WIDTH
    o = 0
    r, k, v, g = (w[..., o + i * rw:o + (i + 1) * rw] for i in range(4))
    o = 4 * rw
    wd, ad = w[..., o:o + LORA_RANK], w[..., o + LORA_RANK:o + 2 * LORA_RANK]
    o += 2 * LORA_RANK
    widths = (DSA_WIDTH, HEAD_DIM, HEAD_DIM, IDX_HEADS * IDX_DIM, IDX_DIM, IDX_HEADS, DSA_WIDTH)
    parts = []
    for wdt in widths:
        parts.append(w[..., o:o + wdt])
        o += wdt
    dq, dk, dv, diq, dik, diw, dg = parts
    cq, ck, cv, cg = (w[..., o + i * DIL_WIDTH:o + (i + 1) * DIL_WIDTH] for i in range(4))
    pad = jnp.zeros(w.shape[:-1] + (LANES - IDX_DIM - IDX_HEADS,), w.dtype)
    return jnp.concatenate([r, k, v, g, cq, ck, cv, cg, dq, dg, diq, dk, dv, dik, diw, pad, wd, ad], axis=-1)


def kernel(x, p, norm_g, w_in, tshift_mu, rwkv_w0, rwkv_w_up, rwkv_a0, rwkv_a_up, rwkv_k_k, rwkv_k_a, rwkv_r_k,
           rwkv_ln_g, rwkv_ln_b, w_out, ple_norm_g, ple_w_gate, ple_w_proj, final_norm_g):
    batch, seq, d_model = x.shape
    depth = w_in.shape[0]
    n = batch * seq
    assert seq % (ATT_BLOCK * max(DIL_RATES)) == 0 and seq >= 4 * DSA_TOPK

    c64, s64 = _rope_tables(seq, HEAD_DIM, HEAD_DIM // 8)
    c32, s32 = _rope_tables(seq, IDX_DIM, IDX_DIM // 8)
    w_in_k = _permute_in_cols(w_in).astype(BF16)
    rw = RWKV_WIDTH
    row = lambda a: a.reshape(1, -1)

    x2d = x.reshape(n, d_model)
    for i in range(depth):
        z = _in_proj(x2d, row(norm_g[i]), w_in_k[i])
        mu = tshift_mu[i]
        y_a = _rwkv(z, batch, seq, mu[:4 * rw].reshape(4, rw), row(mu[4 * rw:]), row(rwkv_w0[i]), rwkv_w_up[i],
                    row(rwkv_a0[i]), rwkv_a_up[i], row(rwkv_k_k[i]), row(rwkv_k_a[i]), row(rwkv_r_k[i]),
                    row(rwkv_ln_g[i]), row(rwkv_ln_b[i]))
        y_b = _dsa(z, batch, seq, c64, s64, c32, s32)
        y_c = _dilated(z, batch, seq, c64, s64)
        wo = w_out[i].astype(BF16)
        x2d = _out_proj(x2d, y_a, y_b, y_c, p[i].reshape(n, -1), wo[:rw], wo[rw:rw + DSA_WIDTH],
                        wo[rw + DSA_WIDTH:], row(ple_norm_g[i]), ple_w_gate[i].astype(BF16),
                        ple_w_proj[i].astype(BF16), row(final_norm_g), final=(i == depth - 1))
    return x2d.reshape(batch, seq, d_model)
```

```python
import functools
import math

import jax
import jax.numpy as jnp
from jax import lax
from jax.experimental import pallas as pl
from jax.experimental.pallas import tpu as pltpu

F32 = jnp.float32
BF16 = jnp.bfloat16
I32 = jnp.int32

HEAD_DIM = 64
ROPE_THETA = 500000.0
NORM_EPS = 1e-6
RWKV_HEADS = 6
RWKV_WIDTH = RWKV_HEADS * HEAD_DIM
LORA_RANK = 64
GN_EPS = 64e-5
DECAY_SCALE = math.exp(-0.5)
DSA_HEADS = 4
DSA_WIDTH = DSA_HEADS * HEAD_DIM
IDX_HEADS = 8
IDX_DIM = 32
DSA_TOPK = 256
DIL_RATES = (1, 4, 16)
DIL_STEPS = 128
DIL_WIDTH = 384
ATT_BLOCK = 128
DSA_KEY_BLOCK = 256

LANES = 128
VMEM_LIMIT = 48 * 1024 * 1024

U_RW = 0
U_DL = 12
U_DS_Q = 24
U_DS_G = 26
U_DS_IQ = 28
U_DS_KV = 30
U_DS_MISC = 31
U_RW_WA = 32
Z_UNITS = 33
Z_COLS = Z_UNITS * LANES

RWKV_CHUNK = 128
INV_BASE = 16
NEG_BIG = -1e30
INT_MIN = -2 ** 31


def _dot(a, b):
    return jnp.dot(a, b, preferred_element_type=F32)


def _dot_nt(a, b):
    return lax.dot_general(a, b, (((1,), (1,)), ((), ())), preferred_element_type=F32)


def _split(x):
    hi = x.astype(BF16)
    lo = (x - hi.astype(F32)).astype(BF16)
    return hi, lo


def _dot_x2(x, w_bf16):
    hi, lo = _split(x)
    return _dot(hi, w_bf16) + _dot(lo, w_bf16)


def _dot_x3(x, w):
    xh, xl = _split(x)
    wh, wl = _split(w)
    return _dot(xh, wh) + (_dot(xh, wl) + _dot(xl, wh))


def _sigmoid(x):
    return 1.0 / (1.0 + jnp.exp(-x))


def _rope(x, cos_t, sin_t, half, head):
    lane = lax.broadcasted_iota(I32, x.shape, 1) % head
    partner = jnp.where(lane < half, pltpu.roll(x, LANES - half, 1), pltpu.roll(x, half, 1))
    return x * cos_t + partner * sin_t


def _in_proj_kernel(x_ref, g_ref, w_ref, z_ref, h_ref):
    @pl.when(pl.program_id(1) == 0)
    def _():
        x = x_ref[...]
        ms = jnp.mean(x * x, axis=-1, keepdims=True)
        h_ref[...] = (x * lax.rsqrt(ms + NORM_EPS) * g_ref[...]).astype(BF16)

    z_ref[...] = _dot(h_ref[...], w_ref[...])


def _in_proj(x2d, g, w, tm=1024, tn=384):
    n, d = x2d.shape
    return pl.pallas_call(
        _in_proj_kernel,
        grid=(n // tm, Z_COLS // tn),
        in_specs=[
            pl.BlockSpec((tm, d), lambda i, j: (i, 0)),
            pl.BlockSpec((1, d), lambda i, j: (0, 0)),
            pl.BlockSpec((d, tn), lambda i, j: (0, j)),
        ],
        out_specs=pl.BlockSpec((tm, tn), lambda i, j: (i, j)),
        out_shape=jax.ShapeDtypeStruct((n, Z_COLS), F32),
        scratch_shapes=[pltpu.VMEM((tm, d), BF16)],
        compiler_params=pltpu.CompilerParams(
            dimension_semantics=("parallel", "arbitrary"), vmem_limit_bytes=VMEM_LIMIT),
        name="in_proj",
    )(x2d, g, w)


def _out_proj_kernel(x_ref, ya_ref, yb_ref, yc_ref, p_ref, wa_ref, wb_ref, wc_ref, ng_ref, wg_ref,
                     wp_ref, fg_ref, o_ref, *, final):
    acc = x_ref[...] + _dot(ya_ref[...], wa_ref[...])
    acc = acc + _dot(yb_ref[...], wb_ref[...])
    acc = acc + _dot(yc_ref[...], wc_ref[...])
    ms = jnp.mean(acc * acc, axis=-1, keepdims=True)
    hn = (acc * lax.rsqrt(ms + NORM_EPS) * ng_ref[...]).astype(BF16)
    gate = _sigmoid(_dot(hn, wg_ref[...]))
    out = acc + gate * _dot(p_ref[...].astype(BF16), wp_ref[...])
    if final:
        ms = jnp.mean(out * out, axis=-1, keepdims=True)
        out = out * lax.rsqrt(ms + NORM_EPS) * fg_ref[...]
    o_ref[...] = out


def _out_proj(x2d, ya, yb, yc, p2d, wa, wb, wc, ng, wg, wp, fg, final, tm=512):
    n, d = x2d.shape
    row = lambda w: pl.BlockSpec((tm, w), lambda i: (i, 0))
    full = lambda a: pl.BlockSpec(a.shape, lambda i: (0, 0))
    return pl.pallas_call(
        functools.partial(_out_proj_kernel, final=final),
        grid=(n // tm,),
        in_specs=[row(d), row(ya.shape[1]), row(yb.shape[1]), row(yc.shape[1]), row(p2d.shape[1]),
                  full(wa), full(wb), full(wc), full(ng), full(wg), full(wp), full(fg)],
        out_specs=row(d),
        out_shape=jax.ShapeDtypeStruct((n, d), F32),
        compiler_params=pltpu.CompilerParams(
            dimension_semantics=("parallel",), vmem_limit_bytes=VMEM_LIMIT),
        name="out_proj",
    )(x2d, ya, yb, yc, p2d, wa, wb, wc, ng, wg, wp, fg)


def _unit_lower_inverse(n_list, row, col):
    c = n_list[0].shape[0]
    eye = jnp.where(row == col, 1.0, 0.0).astype(F32)
    same = (row // INV_BASE) == (col // INV_BASE)
    pw = [jnp.where(same, n, 0.0) for n in n_list]
    t = [eye - x for x in pw]
    span = 2
    while span < INV_BASE:
        pwb = [x.astype(BF16) for x in pw]
        pw = [_dot(x, x) for x in pwb]
        t = [x + _dot(x.astype(BF16), y.astype(BF16)) for x, y in zip(t, pw)]
        span *= 2
    size = INV_BASE
    while size < c:
        lower_left = ((row // (2 * size)) == (col // (2 * size))) & ((row // size) != (col // size))
        tb = [x.astype(BF16) for x in t]
        et = [_dot(jnp.where(lower_left, n, 0.0).astype(BF16), x).astype(BF16) for n, x in zip(n_list, tb)]
        t = [x - _dot(xb, y) for x, xb, y in zip(t, tb, et)]
        size *= 2
    return t


def _rwkv_kernel(zr_ref, zk_ref, zv_ref, zg_ref, zwa_ref, mu_ref, muwa_ref, w0_ref, wup_ref, a0_ref,
                 aup_ref, kk_ref, ka_ref, rk_ref, lng_ref, lnb_ref, o_ref,
                 state_ref, prev_ref, prevwa_ref, y_ref):
    c = zr_ref.shape[0]
    width = RWKV_WIDTH

    @pl.when(pl.program_id(1) == 0)
    def _():
        state_ref[...] = jnp.zeros_like(state_ref)
        prev_ref[...] = jnp.zeros_like(prev_ref)
        prevwa_ref[...] = jnp.zeros_like(prevwa_ref)

    first_row = lax.broadcasted_iota(I32, (c, width), 0) == 0

    def shifted(z_ref, idx):
        z = z_ref[...]
        prev = jnp.where(first_row, prev_ref[idx, 0:1, :], pltpu.roll(z, 1, 0))
        prev_ref[idx, 0:1, :] = z[c - 1:c, :]
        return z + (prev - z) * mu_ref[idx:idx + 1, :]

    r = shifted(zr_ref, 0)
    k = shifted(zk_ref, 1)
    v = shifted(zv_ref, 2)
    g = shifted(zg_ref, 3)
    zwa = zwa_ref[...]
    first_row_wa = lax.broadcasted_iota(I32, (c, LANES), 0) == 0
    prev_wa = jnp.where(first_row_wa, prevwa_ref[0:1, :], pltpu.roll(zwa, 1, 0))
    prevwa_ref[0:1, :] = zwa[c - 1:c, :]
    wa = zwa + (prev_wa - zwa) * muwa_ref[...]
    wd = wa[:, :LORA_RANK]
    ad = wa[:, LORA_RANK:]

    lw = -DECAY_SCALE * _sigmoid(w0_ref[...] + _dot_x3(jnp.tanh(wd), wup_ref[...]))
    a = _sigmoid(a0_ref[...] + _dot_x3(ad, aup_ref[...]))
    hrow = lax.broadcasted_iota(I32, (width, width), 0) // HEAD_DIM
    hcol = lax.broadcasted_iota(I32, (width, width), 1) // HEAD_DIM
    head_ones = jnp.where(hrow == hcol, 1.0, 0.0).astype(BF16)
    kk = k * kk_ref[...]
    norm = jnp.sqrt(_dot_x2(kk * kk, head_ones))
    kk = kk / jnp.maximum(norm, 1e-12)
    kmod = k * (1.0 + (a - 1.0) * ka_ref[...])
    b = a * kk

    row = lax.broadcasted_iota(I32, (c, c), 0)
    col = lax.broadcasted_iota(I32, (c, c), 1)
    tri = jnp.where(col <= row, 1.0, 0.0).astype(BF16)
    lw_hi, lw_lo = _split(lw)
    cl = _dot(tri, lw_hi) + _dot(tri, lw_lo)
    c_mid = cl[c // 2 - 1:c // 2, :]
    c_end = cl[c - 1:c, :]
    w = jnp.exp(lw)
    e_in1 = jnp.exp(cl - lw)
    e_in2 = e_in1 * w
    e_k = jnp.exp(c_mid - cl)
    e_mid_inv = jnp.exp(-c_mid)
    q1 = (kk * e_in1 * e_mid_inv).astype(BF16)
    q2 = (r * e_in2 * e_mid_inv).astype(BF16)
    k1 = (b * e_k).astype(BF16)
    k2 = (kmod * e_k).astype(BF16)
    kkd = (kk * e_in1).astype(BF16)
    rd = (r * e_in2).astype(BF16)
    e_end = e_k * jnp.exp(c_end - c_mid)
    bd = (b * e_end).astype(BF16)
    kd = (kmod * e_end).astype(BF16)
    d_end = jnp.exp(c_end)
    vb = v.astype(BF16)

    strict = col < row
    incl = col <= row
    heads = range(RWKV_HEADS)
    hs = [slice(h * HEAD_DIM, (h + 1) * HEAD_DIM) for h in heads]
    per_head = lambda x: [x[:, s] for s in hs]
    q1, q2, k1, k2, kkd, rd, bd, kd, vh = map(per_head, (q1, q2, k1, k2, kkd, rd, bd, kd, vb))
    a_ab = [jnp.where(strict, _dot_nt(q1[h], k1[h]), 0.0) for h in heads]
    t_inv = _unit_lower_inverse(a_ab, row, col)
    a_ak = [jnp.where(strict, _dot_nt(q1[h], k2[h]), 0.0).astype(BF16) for h in heads]
    a_rb = [jnp.where(incl, _dot_nt(q2[h], k1[h]), 0.0).astype(BF16) for h in heads]
    a_rk = [jnp.where(incl, _dot_nt(q2[h], k2[h]), 0.0).astype(BF16) for h in heads]
    s_old = [state_ref[h] for h in heads]
    s_in = [s.astype(BF16) for s in s_old]
    rhs = [_dot_nt(kkd[h], s_in[h]) + _dot(a_ak[h], vh[h]) for h in heads]
    u = [-_dot(t_inv[h].astype(BF16), rhs[h].astype(BF16)) for h in heads]
    uv_t = [jnp.concatenate([u[h], v[:, hs[h]]], axis=1).T.astype(BF16) for h in heads]
    for h in heads:
        state_ref[h] = (s_old[h] * d_end[:, hs[h]]
                        + _dot(uv_t[h][:HEAD_DIM, :], bd[h]) + _dot(uv_t[h][HEAD_DIM:, :], kd[h]))
    for h in heads:
        y_ref[:, hs[h]] = (_dot_nt(rd[h], s_in[h]) + _dot(a_rb[h], u[h].astype(BF16))
                           + _dot(a_rk[h], vh[h]))

    y = y_ref[...]
    inv_n = 1.0 / HEAD_DIM
    mean = _dot_x2(y, head_ones) * inv_n
    dlt = y - mean
    var = _dot_x2(dlt * dlt, head_ones) * inv_n
    yn = dlt * lax.rsqrt(var + GN_EPS) * lng_ref[...] + lnb_ref[...]
    bonus = _dot_x2(r * kmod * rk_ref[...], head_ones) * v
    o_ref[...] = ((yn + bonus) * (g * _sigmoid(g))).astype(o_ref.dtype)


def _rwkv(z2d, batch, seq, mu4, mu_wa, w0, w_up, a0, a_up, k_k, k_a, r_k, ln_g, ln_b):
    c = RWKV_CHUNK
    nc = seq // c
    zspec = lambda u: pl.BlockSpec((c, RWKV_WIDTH), lambda b, i, u=u: (b * nc + i, u))
    full = lambda a: pl.BlockSpec(a.shape, lambda b, i: (0,) * a.ndim)
    params = (mu4, mu_wa, w0, w_up, a0, a_up, k_k, k_a, r_k, ln_g, ln_b)
    return pl.pallas_call(
        _rwkv_kernel,
        grid=(batch, nc),
        in_specs=[zspec(0), zspec(1), zspec(2), zspec(3),
                  pl.BlockSpec((c, LANES), lambda b, i: (b * nc + i, U_RW_WA))]
                 + [full(a) for a in params],
        out_specs=pl.BlockSpec((c, RWKV_WIDTH), lambda b, i: (b * nc + i, 0)),
        out_shape=jax.ShapeDtypeStruct((batch * seq, RWKV_WIDTH), BF16),
        scratch_shapes=[
            pltpu.VMEM((RWKV_HEADS, HEAD_DIM, HEAD_DIM), F32),
            pltpu.VMEM((4, 8, RWKV_WIDTH), F32),
            pltpu.VMEM((8, LANES), F32),
            pltpu.VMEM((c, RWKV_WIDTH), F32),
        ],
        compiler_params=pltpu.CompilerParams(
            dimension_semantics=("parallel", "arbitrary"), vmem_limit_bytes=VMEM_LIMIT),
        name="rwkv7",
    )(z2d, z2d, z2d, z2d, z2d, *params)


def _dsa_kernel(zq_ref, zg_ref, ziq_ref, zkv_ref, zms_ref, c64_ref, s64_ref, c32_ref, s32_ref, o_ref,
                k_ref, vt_ref, ik_ref, key_ref, bias_ref):
    seq = zkv_ref.shape[0]
    blk = ATT_BLOCK
    kb_rows = DSA_KEY_BLOCK
    qb = pl.program_id(1)
    n_kb = (qb * blk) // kb_rows + 1

    @pl.when(qb == 0)
    def _():
        lane = lax.broadcasted_iota(I32, (kb_rows, LANES), 1)

        def body(t, carry):
            rows = pl.ds(pl.multiple_of(t * kb_rows, kb_rows), kb_rows)
            kv = zkv_ref[rows, :]
            kvr = jnp.where(lane < HEAD_DIM, _rope(kv, c64_ref[rows, :], s64_ref[rows, :], 8, HEAD_DIM), kv)
            k_ref[rows, :] = kvr[:, :HEAD_DIM].astype(BF16)
            vt_ref[t] = kvr.T[HEAD_DIM:, :].astype(BF16)
            ms = zms_ref[rows, :]
            ikr = _rope(ms, c32_ref[rows, :], s32_ref[rows, :], 4, IDX_DIM)
            ik_ref[rows, :] = ikr[:, :IDX_DIM].astype(BF16)
            return carry

        lax.fori_loop(0, seq // kb_rows, body, 0)

    r0 = pl.multiple_of(qb * blk, blk)
    qrows = pl.ds(r0, blk)
    c64 = c64_ref[qrows, :]
    s64 = s64_ref[qrows, :]
    c32 = c32_ref[qrows, :]
    s32 = s32_ref[qrows, :]
    q2 = [_rope(zq_ref[:, i * LANES:(i + 1) * LANES], c64, s64, 8, HEAD_DIM) * (HEAD_DIM ** -0.5)
          for i in range(2)]
    q_stack = jnp.concatenate(
        [q2[h // 2][:, (h % 2) * HEAD_DIM:(h % 2 + 1) * HEAD_DIM] for h in range(DSA_HEADS)],
        axis=0).astype(BF16)
    iq2 = [_rope(ziq_ref[:, i * LANES:(i + 1) * LANES], c32, s32, 4, IDX_DIM) for i in range(2)]
    iq_stack = jnp.concatenate(
        [iq2[h // 4][:, (h % 4) * IDX_DIM:(h % 4 + 1) * IDX_DIM] for h in range(IDX_HEADS)],
        axis=0).astype(BF16)
    iw_t = zms_ref[qrows, :].T[IDX_DIM:IDX_DIM + IDX_HEADS, :] * ((IDX_HEADS * IDX_DIM) ** -0.5)
    iw_row = jnp.concatenate([iw_t[h:h + 1, :] for h in range(IDX_HEADS)], axis=1)

    kpos_l = lax.broadcasted_iota(I32, (kb_rows, blk), 0)
    qpos = r0 + lax.broadcasted_iota(I32, (kb_rows, blk), 1)

    def key_rows(kb):
        return pl.ds(pl.multiple_of(kb * kb_rows, kb_rows), kb_rows)

    def score_body(kb, carry):
        s = jnp.maximum(_dot_nt(ik_ref[key_rows(kb), :], iq_stack), 0.0) * iw_row
        sc = s[:, 0:blk]
        for h in range(1, IDX_HEADS):
            sc = sc + s[:, h * blk:(h + 1) * blk]
        sc = jnp.where(kb * kb_rows + kpos_l <= qpos, sc, -jnp.inf)
        bits = pltpu.bitcast(sc, I32)
        key_ref[key_rows(kb), :] = bits ^ ((bits >> 31) & 0x7FFFFFFF)
        return carry

    lax.fori_loop(0, n_kb, score_body, 0)

    def count(pred_fn):
        def body(kb, acc):
            hit = jnp.where(pred_fn(key_ref[key_rows(kb), :]), 1, 0).astype(I32)
            return acc + hit.reshape(kb_rows // 8, 8, blk).sum(axis=0)

        acc = lax.fori_loop(0, n_kb, body, jnp.zeros((8, blk), I32))
        return acc.sum(axis=0, keepdims=True)

    def bit_body(i, lo):
        cand = lo + (jnp.int32(1) << (31 - i))
        return jnp.where(count(lambda keys: keys >= cand) >= DSA_TOPK, cand, lo)

    lo0 = jnp.full((1, blk), INT_MIN, I32)
    thr = lax.cond(qb * blk >= DSA_TOPK, lambda: lax.fori_loop(0, 32, bit_body, lo0), lambda: lo0)
    need = (DSA_TOPK - count(lambda keys: keys > thr)).astype(F32)

    tri = jnp.where(lax.broadcasted_iota(I32, (kb_rows, kb_rows), 1)
                    <= lax.broadcasted_iota(I32, (kb_rows, kb_rows), 0), 1.0, 0.0).astype(BF16)

    def select_body(kb, run):
        keys = key_ref[key_rows(kb), :]
        eq = keys == thr
        rank = _dot(tri, jnp.where(eq, 1.0, 0.0).astype(BF16)) + run
        take = (keys > thr) | (eq & (rank <= need))
        take = take & (kb * kb_rows + kpos_l <= qpos)
        bias_ref[key_rows(kb), :] = jnp.where(take, 0.0, NEG_BIG)
        return rank[kb_rows - 1:kb_rows, :]

    lax.fori_loop(0, n_kb, select_body, jnp.zeros((1, blk), F32))

    def att_body(kb, carry):
        m, l, acc = carry
        bias = bias_ref[key_rows(kb), :]
        s = _dot_nt(k_ref[key_rows(kb), :], q_stack)
        s = s + jnp.concatenate([bias] * DSA_HEADS, axis=1)
        m_new = jnp.maximum(m, jnp.max(s, axis=0, keepdims=True))
        p = jnp.exp(s - m_new)
        alpha = jnp.exp(m - m_new)
        l = alpha * l + jnp.sum(p, axis=0, keepdims=True)
        acc = alpha * acc + _dot(vt_ref[kb], p.astype(BF16))
        return m_new, l, acc

    wide = DSA_HEADS * blk
    init = (jnp.full((1, wide), NEG_BIG, F32), jnp.zeros((1, wide), F32), jnp.zeros((HEAD_DIM, wide), F32))
    _, l, acc = lax.fori_loop(0, n_kb, att_body, init)
    o_t = acc / l
    o = jnp.concatenate(
        [jnp.concatenate([o_t[:, (2 * i) * blk:(2 * i + 1) * blk], o_t[:, (2 * i + 1) * blk:(2 * i + 2) * blk]],
                         axis=0).T for i in range(DSA_HEADS // 2)], axis=1)
    g = zg_ref[...]
    o_ref[...] = (o * (g * _sigmoid(g))).astype(o_ref.dtype)


def _dsa(z2d, batch, seq, c64, s64, c32, s32):
    blk = ATT_BLOCK
    nq = seq // blk
    qspec = lambda u2: pl.BlockSpec((blk, 2 * LANES), lambda b, i, u2=u2: (b * nq + i, u2))
    seqspec = lambda u: pl.BlockSpec((seq, LANES), lambda b, i, u=u: (b, u))
    tab = pl.BlockSpec((seq, LANES), lambda b, i: (0, 0))
    return pl.pallas_call(
        _dsa_kernel,
        grid=(batch, nq),
        in_specs=[qspec(U_DS_Q // 2), qspec(U_DS_G // 2), qspec(U_DS_IQ // 2),
                  seqspec(U_DS_KV), seqspec(U_DS_MISC), tab, tab, tab, tab],
        out_specs=pl.BlockSpec((blk, DSA_WIDTH), lambda b, i: (b * nq + i, 0)),
        out_shape=jax.ShapeDtypeStruct((batch * seq, DSA_WIDTH), BF16),
        scratch_shapes=[
            pltpu.VMEM((seq, HEAD_DIM), BF16),
            pltpu.VMEM((seq // DSA_KEY_BLOCK, HEAD_DIM, DSA_KEY_BLOCK), BF16),
            pltpu.VMEM((seq, IDX_DIM), BF16),
            pltpu.VMEM((seq, blk), I32),
            pltpu.VMEM((seq, blk), F32),
        ],
        compiler_params=pltpu.CompilerParams(
            dimension_semantics=("parallel", "arbitrary"), vmem_limit_bytes=VMEM_LIMIT),
        name="dsa",
    )(z2d, z2d, z2d, z2d, z2d, c64, s64, c32, s32)


def _dil_group(gi, rate, q_ref, k_ref, v_ref, g_ref, c64_ref, s64_ref, qr_ref, kr_ref, o_s, lse_s):
    seq = q_ref.shape[0]
    blk = ATT_BLOCK
    nrow = seq // blk

    def rope_body(t, carry):
        rows = pl.ds(pl.multiple_of(t * blk, blk), blk)
        ct = c64_ref[rows, :]
        st = s64_ref[rows, :]
        qr_ref[rows, :] = _rope(q_ref[rows, :], ct, st, 8, HEAD_DIM) * (HEAD_DIM ** -0.5)
        kr_ref[rows, :] = _rope(k_ref[rows, :], ct, st, 8, HEAD_DIM)
        return carry

    lax.fori_loop(0, nrow, rope_body, 0)

    nblk = seq // (blk * rate)
    qi = lax.broadcasted_iota(I32, (blk, 2 * blk), 0)
    kj = lax.broadcasted_iota(I32, (blk, 2 * blk), 1)
    band = (kj >= qi) & (kj <= qi + DIL_STEPS)

    def strided(ref, start):
        if rate == 1:
            return ref[pl.ds(start, blk), :]
        return ref[pl.ds(start, blk, stride=rate), :]

    def body(cidx, carry):
        rho = cidx // nblk
        j = cidx % nblk
        start = j * (blk * rate) + rho
        pstart = jnp.maximum(j - 1, 0) * (blk * rate) + rho
        qblk = strided(qr_ref, start).astype(BF16)
        kc = jnp.concatenate([strided(kr_ref, pstart), strided(kr_ref, start)], axis=0).astype(BF16)
        vc = jnp.concatenate([strided(v_ref, pstart), strided(v_ref, start)], axis=0).astype(BF16)
        mask = band & ((kj >= blk) | (j > 0))
        outs, lses = [], []
        for h in range(2):
            sl = slice(h * HEAD_DIM, (h + 1) * HEAD_DIM)
            s = jnp.where(mask, _dot_nt(qblk[:, sl], kc[:, sl]), -jnp.inf)
            m = jnp.max(s, axis=-1, keepdims=True)
            e = jnp.exp(s - m)
            l = jnp.sum(e, axis=-1, keepdims=True)
            outs.append(_dot((e / l).astype(BF16), vc[:, sl]))
            lses.append(jnp.broadcast_to(m + jnp.log(l), (blk, HEAD_DIM)))
        gate = strided(g_ref, start)
        o = jnp.concatenate(outs, axis=1) * (gate * _sigmoid(gate))
        lse = jnp.concatenate(lses, axis=1)
        if rate == 1:
            o_s[gi, pl.ds(start, blk), :] = o
            lse_s[gi, pl.ds(start, blk), :] = lse
        else:
            o_s[gi, pl.ds(start, blk, stride=rate), :] = o
            lse_s[gi, pl.ds(start, blk, stride=rate), :] = lse
        return carry

    lax.fori_loop(0, rate * nblk, body, 0)


def _dil_kernel(q_ref, k_ref, v_ref, g_ref, c64_ref, s64_ref, o_ref, qr_ref, kr_ref, o_s, lse_s):
    grp = pl.program_id(1)
    for gi, rate in enumerate(DIL_RATES):
        pl.when(grp == gi)(functools.partial(
            _dil_group, gi, rate, q_ref, k_ref, v_ref, g_ref, c64_ref, s64_ref, qr_ref, kr_ref, o_s, lse_s))

    @pl.when(grp == len(DIL_RATES) - 1)
    def _():
        blk = ATT_BLOCK

        def body(t, carry):
            rows = pl.ds(pl.multiple_of(t * blk, blk), blk)
            ls = [lse_s[i, rows, :] for i in range(3)]
            top = jnp.maximum(jnp.maximum(ls[0], ls[1]), ls[2])
            es = [jnp.exp(x - top) for x in ls]
            inv = 1.0 / (es[0] + es[1] + es[2])
            o_ref[rows, :] = jnp.concatenate(
                [o_s[i, rows, :] * (es[i] * inv) for i in range(3)], axis=1).astype(o_ref.dtype)
            return carry

        lax.fori_loop(0, q_ref.shape[0] // blk, body, 0)


def _dilated(z2d, batch, seq, c64, s64):
    ng = len(DIL_RATES)
    zspec = lambda base: pl.BlockSpec((seq, LANES), lambda b, g, base=base: (b, base + g))
    tab = pl.BlockSpec((seq, LANES), lambda b, g: (0, 0))
    return pl.pallas_call(
        _dil_kernel,
        grid=(batch, ng),
        in_specs=[zspec(U_DL), zspec(U_DL + 3), zspec(U_DL + 6), zspec(U_DL + 9), tab, tab],
        out_specs=pl.BlockSpec((seq, DIL_WIDTH), lambda b, g: (b, 0)),
        out_shape=jax.ShapeDtypeStruct((batch * seq, DIL_WIDTH), BF16),
        scratch_shapes=[
            pltpu.VMEM((seq, LANES), F32),
            pltpu.VMEM((seq, LANES), F32),
            pltpu.VMEM((ng, seq, LANES), F32),
            pltpu.VMEM((ng, seq, LANES), F32),
        ],
        compiler_params=pltpu.CompilerParams(
            dimension_semantics=("parallel", "arbitrary"), vmem_limit_bytes=VMEM_LIMIT),
        name="dilated",
    )(z2d, z2d, z2d, z2d, c64, s64)


def _rope_tables(seq, head, half):
    inv = ROPE_THETA ** (-jnp.arange(half, dtype=F32) / half)
    ang = jnp.arange(seq, dtype=F32)[:, None] * inv[None, :]
    cos, sin = jnp.cos(ang), jnp.sin(ang)
    lane = jnp.arange(LANES) % head
    idx = lane % half
    cos_t = jnp.where(lane[None, :] < 2 * half, cos[:, idx], 1.0)
    sin_t = jnp.where(lane[None, :] < half, -sin[:, idx], jnp.where(lane[None, :] < 2 * half, sin[:, idx], 0.0))
    return cos_t, sin_t


def _permute_in_cols(w):
    rw = RWKV_WIDTH
    o = 0
    r, k, v, g = (w[..., o + i * rw:o + (i + 1) * rw] for i in range(4))
    o = 4 * rw
    wd, ad = w[..., o:o + LORA_RANK], w[..., o + LORA_RANK:o + 2 * LORA_RANK]
    o += 2 * LORA_RANK
    widths = (DSA_WIDTH, HEAD_DIM, HEAD_DIM, IDX_HEADS * IDX_DIM, IDX_DIM, IDX_HEADS, DSA_WIDTH)
    parts = []
    for wdt in widths:
        parts.append(w[..., o:o + wdt])
        o += wdt
    dq, dk, dv, diq, dik, diw, dg = parts
    cq, ck, cv, cg = (w[..., o + i * DIL_WIDTH:o + (i + 1) * DIL_WIDTH] for i in range(4))
    pad = jnp.zeros(w.shape[:-1] + (LANES - IDX_DIM - IDX_HEADS,), w.dtype)
    return jnp.concatenate([r, k, v, g, cq, ck, cv, cg, dq, dg, diq, dk, dv, dik, diw, pad, wd, ad], axis=-1)


def kernel(x, p, norm_g, w_in, tshift_mu, rwkv_w0, rwkv_w_up, rwkv_a0, rwkv_a_up, rwkv_k_k, rwkv_k_a, rwkv_r_k,
           rwkv_ln_g, rwkv_ln_b, w_out, ple_norm_g, ple_w_gate, ple_w_proj, final_norm_g):
    batch, seq, d_model = x.shape
    depth = w_in.shape[0]
    n = batch * seq
    assert seq % (ATT_BLOCK * max(DIL_RATES)) == 0 and seq >= 4 * DSA_TOPK

    c64, s64 = _rope_tables(seq, HEAD_DIM, HEAD_DIM // 8)
    c32, s32 = _rope_tables(seq, IDX_DIM, IDX_DIM // 8)
    w_in_k = _permute_in_cols(w_in).astype(BF16)
    rw = RWKV_WIDTH
    row = lambda a: a.reshape(1, -1)

    x2d = x.reshape(n, d_model)
    for i in range(depth):
        z = _in_proj(x2d, row(norm_g[i]), w_in_k[i])
        mu = tshift_mu[i]
        y_a = _rwkv(z, batch, seq, mu[:4 * rw].reshape(4, rw), row(mu[4 * rw:]), row(rwkv_w0[i]), rwkv_w_up[i],
                    row(rwkv_a0[i]), rwkv_a_up[i], row(rwkv_k_k[i]), row(rwkv_k_a[i]), row(rwkv_r_k[i]),
                    row(rwkv_ln_g[i]), row(rwkv_ln_b[i]))
        y_b = _dsa(z, batch, seq, c64, s64, c32, s32)
        y_c = _dilated(z, batch, seq, c64, s64)
        wo = w_out[i].astype(BF16)
        x2d = _out_proj(x2d, y_a, y_b, y_c, p[i].reshape(n, -1), wo[:rw], wo[rw:rw + DSA_WIDTH],
                        wo[rw + DSA_WIDTH:], row(ple_norm_g[i]), ple_w_gate[i].astype(BF16),
                        ple_w_proj[i].astype(BF16), row(final_norm_g), final=(i == depth - 1))
    return x2d.reshape(batch, seq, d_model)
```

```python
import functools
import math

import jax
import jax.numpy as jnp
from jax import lax
from jax.experimental import pallas as pl
from jax.experimental.pallas import tpu as pltpu

F32 = jnp.float32
BF16 = jnp.bfloat16
I32 = jnp.int32

HEAD_DIM = 64
ROPE_THETA = 500000.0
NORM_EPS = 1e-6
RWKV_HEADS = 6
RWKV_WIDTH = RWKV_HEADS * HEAD_DIM
LORA_RANK = 64
GN_EPS = 64e-5
DECAY_SCALE = math.exp(-0.5)
DSA_HEADS = 4
DSA_WIDTH = DSA_HEADS * HEAD_DIM
IDX_HEADS = 8
IDX_DIM = 32
DSA_TOPK = 256
DIL_RATES = (1, 4, 16)
DIL_STEPS = 128
DIL_UNROLL = 4
DIL_WIDTH = 384
ATT_BLOCK = 128
DSA_KEY_BLOCK = 512

LANES = 128
VMEM_LIMIT = 48 * 1024 * 1024

U_RW = 0
U_DL = 12
U_DS_Q = 24
U_DS_G = 26
U_DS_IQ = 28
U_DS_KV = 30
U_DS_MISC = 31
U_RW_WA = 32
Z_UNITS = 33
Z_COLS = Z_UNITS * LANES

RWKV_CHUNK = 128
INV_BASE = 16
QK_SCALE_LOG2 = HEAD_DIM ** -0.5 * math.log2(math.e)
NEG_BIG = -1e30
INT_MIN = -2 ** 31


def _dot(a, b):
    return jnp.dot(a, b, preferred_element_type=F32)


def _dot_nt(a, b):
    return lax.dot_general(a, b, (((1,), (1,)), ((), ())), preferred_element_type=F32)


def _split(x):
    hi = x.astype(BF16)
    lo = (x - hi.astype(F32)).astype(BF16)
    return hi, lo


def _head_sum(x, head_ones):
    return _dot(x.astype(BF16), head_ones)


def _dot_x3(x, w):
    xh, xl = _split(x)
    wh, wl = _split(w)
    return _dot(xh, wh) + (_dot(xh, wl) + _dot(xl, wh))


def _sigmoid(x):
    return 1.0 / (1.0 + jnp.exp(-x))


def _rope(x, cos_t, sin_t, half, head):
    lane = lax.broadcasted_iota(I32, x.shape, 1) % head
    partner = jnp.where(lane < half, pltpu.roll(x, LANES - half, 1), pltpu.roll(x, half, 1))
    return x * cos_t + partner * sin_t


def _in_proj_kernel(x_ref, g_ref, w_ref, z_ref, *, tn):
    x = x_ref[...]
    ms = jnp.mean(x * x, axis=-1, keepdims=True)
    h = (x * lax.rsqrt(ms + NORM_EPS) * g_ref[...]).astype(BF16)
    for j in range(Z_COLS // tn):
        z_ref[:, j * tn:(j + 1) * tn] = _dot(h, w_ref[:, j * tn:(j + 1) * tn])


def _in_proj(x2d, g, w, tm=512, tn=384):
    n, d = x2d.shape
    return pl.pallas_call(
        functools.partial(_in_proj_kernel, tn=tn),
        grid=(n // tm,),
        in_specs=[
            pl.BlockSpec((tm, d), lambda i: (i, 0)),
            pl.BlockSpec((1, d), lambda i: (0, 0)),
            pl.BlockSpec((d, Z_COLS), lambda i: (0, 0)),
        ],
        out_specs=pl.BlockSpec((tm, Z_COLS), lambda i: (i, 0)),
        out_shape=jax.ShapeDtypeStruct((n, Z_COLS), F32),
        compiler_params=pltpu.CompilerParams(
            dimension_semantics=("parallel",), vmem_limit_bytes=VMEM_LIMIT),
        name="in_proj",
    )(x2d, g, w)


def _out_proj_kernel(x_ref, ya_ref, yb_ref, yc_ref, p_ref, wa_ref, wb_ref, wc_ref, ng_ref, wg_ref,
                     wp_ref, fg_ref, o_ref, *, final):
    acc = x_ref[...] + _dot(ya_ref[...], wa_ref[...])
    acc = acc + _dot(yb_ref[...], wb_ref[...])
    acc = acc + _dot(yc_ref[...], wc_ref[...])
    ms = jnp.mean(acc * acc, axis=-1, keepdims=True)
    hn = (acc * lax.rsqrt(ms + NORM_EPS) * ng_ref[...]).astype(BF16)
    gate = _sigmoid(_dot(hn, wg_ref[...]))
    out = acc + gate * _dot(p_ref[...].astype(BF16), wp_ref[...])
    if final:
        ms = jnp.mean(out * out, axis=-1, keepdims=True)
        out = out * lax.rsqrt(ms + NORM_EPS) * fg_ref[...]
    o_ref[...] = out


def _out_proj(x2d, ya, yb, yc, p2d, wa, wb, wc, ng, wg, wp, fg, final, tm=512):
    n, d = x2d.shape
    row = lambda w: pl.BlockSpec((tm, w), lambda i: (i, 0))
    full = lambda a: pl.BlockSpec(a.shape, lambda i: (0, 0))
    return pl.pallas_call(
        functools.partial(_out_proj_kernel, final=final),
        grid=(n // tm,),
        in_specs=[row(d), row(ya.shape[1]), row(yb.shape[1]), row(yc.shape[1]), row(p2d.shape[1]),
                  full(wa), full(wb), full(wc), full(ng), full(wg), full(wp), full(fg)],
        out_specs=row(d),
        out_shape=jax.ShapeDtypeStruct((n, d), F32),
        compiler_params=pltpu.CompilerParams(
            dimension_semantics=("parallel",), vmem_limit_bytes=VMEM_LIMIT),
        name="out_proj",
    )(x2d, ya, yb, yc, p2d, wa, wb, wc, ng, wg, wp, fg)


def _unit_lower_inverse(n_list, row, col):
    c = n_list[0].shape[0]
    eye = jnp.where(row == col, 1.0, 0.0).astype(F32)
    same = (row // INV_BASE) == (col // INV_BASE)
    pw = [jnp.where(same, n, 0.0) for n in n_list]
    t = [eye - x for x in pw]
    span = 2
    while span < INV_BASE:
        pwb = [x.astype(BF16) for x in pw]
        pw = [_dot(x, x) for x in pwb]
        t = [x + _dot(x.astype(BF16), y.astype(BF16)) for x, y in zip(t, pw)]
        span *= 2
    size = INV_BASE
    while size < c:
        lower_left = ((row // (2 * size)) == (col // (2 * size))) & ((row // size) != (col // size))
        tb = [x.astype(BF16) for x in t]
        et = [_dot(jnp.where(lower_left, n, 0.0).astype(BF16), x).astype(BF16) for n, x in zip(n_list, tb)]
        t = [x - _dot(xb, y) for x, xb, y in zip(t, tb, et)]
        size *= 2
    return t


def _rwkv_kernel(zr_ref, zk_ref, zv_ref, zg_ref, zwa_ref, mu_ref, muwa_ref, w0_ref, wup_ref, a0_ref,
                 aup_ref, kk_ref, ka_ref, rk_ref, lng_ref, lnb_ref, o_ref,
                 state_ref, prev_ref, prevwa_ref, y_ref):
    c = zr_ref.shape[0]
    width = RWKV_WIDTH

    @pl.when(pl.program_id(1) == 0)
    def _():
        state_ref[...] = jnp.zeros_like(state_ref)
        prev_ref[...] = jnp.zeros_like(prev_ref)
        prevwa_ref[...] = jnp.zeros_like(prevwa_ref)

    first_row = lax.broadcasted_iota(I32, (c, width), 0) == 0

    def shifted(z_ref, idx):
        z = z_ref[...]
        prev = jnp.where(first_row, prev_ref[idx, 0:1, :], pltpu.roll(z, 1, 0))
        prev_ref[idx, 0:1, :] = z[c - 1:c, :]
        return z + (prev - z) * mu_ref[idx:idx + 1, :]

    r = shifted(zr_ref, 0)
    k = shifted(zk_ref, 1)
    v = shifted(zv_ref, 2)
    g = shifted(zg_ref, 3)
    zwa = zwa_ref[...]
    first_row_wa = lax.broadcasted_iota(I32, (c, LANES), 0) == 0
    prev_wa = jnp.where(first_row_wa, prevwa_ref[0:1, :], pltpu.roll(zwa, 1, 0))
    prevwa_ref[0:1, :] = zwa[c - 1:c, :]
    wa = zwa + (prev_wa - zwa) * muwa_ref[...]
    wd = wa[:, :LORA_RANK]
    ad = wa[:, LORA_RANK:]

    lw = -DECAY_SCALE * _sigmoid(w0_ref[...] + _dot_x3(jnp.tanh(wd), wup_ref[...]))
    a = _sigmoid(a0_ref[...] + _dot_x3(ad, aup_ref[...]))
    hrow = lax.broadcasted_iota(I32, (width, width), 0) // HEAD_DIM
    hcol = lax.broadcasted_iota(I32, (width, width), 1) // HEAD_DIM
    head_ones = jnp.where(hrow == hcol, 1.0, 0.0).astype(BF16)
    kk = k * kk_ref[...]
    norm = jnp.sqrt(_head_sum(kk * kk, head_ones))
    kk = kk / jnp.maximum(norm, 1e-12)
    kmod = k * (1.0 + (a - 1.0) * ka_ref[...])
    b = a * kk

    row = lax.broadcasted_iota(I32, (c, c), 0)
    col = lax.broadcasted_iota(I32, (c, c), 1)
    tri = jnp.where(col <= row, 1.0, 0.0).astype(BF16)
    lw_hi, lw_lo = _split(lw)
    cl = _dot(tri, lw_hi) + _dot(tri, lw_lo)
    c_mid = cl[c // 2 - 1:c // 2, :]
    c_end = cl[c - 1:c, :]
    w = jnp.exp(lw)
    e_in1 = jnp.exp(cl - lw)
    e_in2 = e_in1 * w
    e_k = jnp.exp(c_mid - cl)
    e_mid_inv = jnp.exp(-c_mid)
    q1 = (kk * e_in1 * e_mid_inv).astype(BF16)
    q2 = (r * e_in2 * e_mid_inv).astype(BF16)
    k1 = (b * e_k).astype(BF16)
    k2 = (kmod * e_k).astype(BF16)
    kkd = (kk * e_in1).astype(BF16)
    rd = (r * e_in2).astype(BF16)
    e_end = e_k * jnp.exp(c_end - c_mid)
    bd = (b * e_end).astype(BF16)
    kd = (kmod * e_end).astype(BF16)
    d_end = jnp.exp(c_end)
    vb = v.astype(BF16)

    strict = col < row
    incl = col <= row
    heads = range(RWKV_HEADS)
    hs = [slice(h * HEAD_DIM, (h + 1) * HEAD_DIM) for h in heads]
    per_head = lambda x: [x[:, s] for s in hs]
    q1, q2, k1, k2, kkd, rd, bd, kd, vh = map(per_head, (q1, q2, k1, k2, kkd, rd, bd, kd, vb))
    a_ab = [jnp.where(strict, _dot_nt(q1[h], k1[h]), 0.0) for h in heads]
    t_inv = _unit_lower_inverse(a_ab, row, col)
    a_ak = [jnp.where(strict, _dot_nt(q1[h], k2[h]), 0.0).astype(BF16) for h in heads]
    a_rb = [jnp.where(incl, _dot_nt(q2[h], k1[h]), 0.0).astype(BF16) for h in heads]
    a_rk = [jnp.where(incl, _dot_nt(q2[h], k2[h]), 0.0).astype(BF16) for h in heads]
    s_old = [state_ref[h] for h in heads]
    s_in = [s.astype(BF16) for s in s_old]
    rhs = [_dot_nt(kkd[h], s_in[h]) + _dot(a_ak[h], vh[h]) for h in heads]
    u = [-_dot(t_inv[h].astype(BF16), rhs[h].astype(BF16)) for h in heads]
    uv_t = [jnp.concatenate([u[h], v[:, hs[h]]], axis=1).T.astype(BF16) for h in heads]
    for h in heads:
        state_ref[h] = (s_old[h] * d_end[:, hs[h]]
                        + _dot(uv_t[h][:HEAD_DIM, :], bd[h]) + _dot(uv_t[h][HEAD_DIM:, :], kd[h]))
    for h in heads:
        y_ref[:, hs[h]] = (_dot_nt(rd[h], s_in[h]) + _dot(a_rb[h], u[h].astype(BF16))
                           + _dot(a_rk[h], vh[h]))

    y = y_ref[...]
    inv_n = 1.0 / HEAD_DIM
    mean = _head_sum(y, head_ones) * inv_n
    dlt = y - mean
    var = _head_sum(dlt * dlt, head_ones) * inv_n
    yn = dlt * lax.rsqrt(var + GN_EPS) * lng_ref[...] + lnb_ref[...]
    bonus = _head_sum(r * kmod * rk_ref[...], head_ones) * v
    o_ref[...] = ((yn + bonus) * (g * _sigmoid(g))).astype(o_ref.dtype)


def _rwkv(z2d, batch, seq, mu4, mu_wa, w0, w_up, a0, a_up, k_k, k_a, r_k, ln_g, ln_b):
    c = RWKV_CHUNK
    nc = seq // c
    zspec = lambda u: pl.BlockSpec((c, RWKV_WIDTH), lambda b, i, u=u: (b * nc + i, u))
    full = lambda a: pl.BlockSpec(a.shape, lambda b, i: (0,) * a.ndim)
    params = (mu4, mu_wa, w0, w_up, a0, a_up, k_k, k_a, r_k, ln_g, ln_b)
    return pl.pallas_call(
        _rwkv_kernel,
        grid=(batch, nc),
        in_specs=[zspec(0), zspec(1), zspec(2), zspec(3),
                  pl.BlockSpec((c, LANES), lambda b, i: (b * nc + i, U_RW_WA))]
                 + [full(a) for a in params],
        out_specs=pl.BlockSpec((c, RWKV_WIDTH), lambda b, i: (b * nc + i, 0)),
        out_shape=jax.ShapeDtypeStruct((batch * seq, RWKV_WIDTH), BF16),
        scratch_shapes=[
            pltpu.VMEM((RWKV_HEADS, HEAD_DIM, HEAD_DIM), F32),
            pltpu.VMEM((4, 8, RWKV_WIDTH), F32),
            pltpu.VMEM((8, LANES), F32),
            pltpu.VMEM((c, RWKV_WIDTH), F32),
        ],
        compiler_params=pltpu.CompilerParams(
            dimension_semantics=("parallel", "arbitrary"), vmem_limit_bytes=VMEM_LIMIT),
        name="rwkv7",
    )(z2d, z2d, z2d, z2d, z2d, *params)


def _dsa_kernel(zq_ref, zg_ref, ziq_ref, zkv_ref, zms_ref, c64_ref, s64_ref, c32_ref, s32_ref, o_ref,
                k_ref, vt_ref, ik_ref, key_ref, bias_ref):
    seq = zkv_ref.shape[0]
    blk = ATT_BLOCK
    kb_rows = DSA_KEY_BLOCK
    qb = pl.program_id(1)
    n_kb = (qb * blk) // kb_rows + 1

    @pl.when(qb == 0)
    def _():
        lane = lax.broadcasted_iota(I32, (kb_rows, LANES), 1)

        def body(t, carry):
            rows = pl.ds(pl.multiple_of(t * kb_rows, kb_rows), kb_rows)
            kv = zkv_ref[rows, :]
            kvr = jnp.where(lane < HEAD_DIM, _rope(kv, c64_ref[rows, :], s64_ref[rows, :], 8, HEAD_DIM), kv)
            k_ref[rows, :] = kvr[:, :HEAD_DIM].astype(BF16)
            vt_ref[t] = kvr.T[HEAD_DIM:, :].astype(BF16)
            ms = zms_ref[rows, :]
            ikr = _rope(ms, c32_ref[rows, :], s32_ref[rows, :], 4, IDX_DIM)
            ik_ref[rows, :] = ikr[:, :IDX_DIM].astype(BF16)
            return carry

        lax.fori_loop(0, seq // kb_rows, body, 0)

    r0 = pl.multiple_of(qb * blk, blk)
    qrows = pl.ds(r0, blk)
    c64 = c64_ref[qrows, :]
    s64 = s64_ref[qrows, :]
    c32 = c32_ref[qrows, :]
    s32 = s32_ref[qrows, :]
    q2 = [_rope(zq_ref[:, i * LANES:(i + 1) * LANES], c64, s64, 8, HEAD_DIM) * QK_SCALE_LOG2
          for i in range(2)]
    q_stack = jnp.concatenate(
        [q2[h // 2][:, (h % 2) * HEAD_DIM:(h % 2 + 1) * HEAD_DIM] for h in range(DSA_HEADS)],
        axis=0).astype(BF16)
    iq2 = [_rope(ziq_ref[:, i * LANES:(i + 1) * LANES], c32, s32, 4, IDX_DIM) for i in range(2)]
    iq_stack = jnp.concatenate(
        [iq2[h // 4][:, (h % 4) * IDX_DIM:(h % 4 + 1) * IDX_DIM] for h in range(IDX_HEADS)],
        axis=0).astype(BF16)
    iw_t = zms_ref[qrows, :].T[IDX_DIM:IDX_DIM + IDX_HEADS, :] * ((IDX_HEADS * IDX_DIM) ** -0.5)
    iw_row = jnp.concatenate([iw_t[h:h + 1, :] for h in range(IDX_HEADS)], axis=1)

    kpos_l = lax.broadcasted_iota(I32, (kb_rows, blk), 0)
    qpos = r0 + lax.broadcasted_iota(I32, (kb_rows, blk), 1)

    def key_rows(kb):
        return pl.ds(pl.multiple_of(kb * kb_rows, kb_rows), kb_rows)

    def score_body(kb, carry):
        s = jnp.maximum(_dot_nt(ik_ref[key_rows(kb), :], iq_stack), 0.0) * iw_row
        sc = s[:, 0:blk]
        for h in range(1, IDX_HEADS):
            sc = sc + s[:, h * blk:(h + 1) * blk]
        sc = jnp.where(kb * kb_rows + kpos_l <= qpos, sc, -jnp.inf)
        bits = pltpu.bitcast(sc, I32)
        key_ref[key_rows(kb), :] = bits ^ ((bits >> 31) & 0x7FFFFFFF)
        return carry

    lax.fori_loop(0, n_kb, score_body, 0)

    def count(pred_fn):
        def body(kb, acc):
            hit = jnp.where(pred_fn(key_ref[key_rows(kb), :]), 1, 0).astype(I32)
            return acc + hit.reshape(kb_rows // 8, 8, blk).sum(axis=0)

        acc = lax.fori_loop(0, n_kb, body, jnp.zeros((8, blk), I32))
        return acc.sum(axis=0, keepdims=True)

    def bit_body(i, lo):
        cand = lo + (jnp.int32(1) << (31 - i))
        return jnp.where(count(lambda keys: keys >= cand) >= DSA_TOPK, cand, lo)

    lo0 = jnp.full((1, blk), INT_MIN, I32)
    thr = lax.cond(qb * blk >= DSA_TOPK, lambda: lax.fori_loop(0, 32, bit_body, lo0), lambda: lo0)
    need = (DSA_TOPK - count(lambda keys: keys > thr)).astype(F32)

    tri = jnp.where(lax.broadcasted_iota(I32, (kb_rows, kb_rows), 1)
                    <= lax.broadcasted_iota(I32, (kb_rows, kb_rows), 0), 1.0, 0.0).astype(BF16)

    def select_body(kb, run):
        keys = key_ref[key_rows(kb), :]
        eq = keys == thr
        rank = _dot(tri, jnp.where(eq, 1.0, 0.0).astype(BF16)) + run
        take = (keys > thr) | (eq & (rank <= need))
        take = take & (kb * kb_rows + kpos_l <= qpos)
        bias_ref[key_rows(kb), :] = jnp.where(take, 0.0, NEG_BIG)
        return rank[kb_rows - 1:kb_rows, :]

    lax.fori_loop(0, n_kb, select_body, jnp.zeros((1, blk), F32))

    def att_body(kb, carry):
        m, l, acc = carry
        bias = bias_ref[key_rows(kb), :]
        s = _dot_nt(k_ref[key_rows(kb), :], q_stack)
        s = s + jnp.concatenate([bias] * DSA_HEADS, axis=1)
        m_new = jnp.maximum(m, jnp.max(s, axis=0, keepdims=True))
        p = jnp.exp2(s - m_new)
        alpha = jnp.exp2(m - m_new)
        l = alpha * l + jnp.sum(p, axis=0, keepdims=True)
        acc = alpha * acc + _dot(vt_ref[kb], p.astype(BF16))
        return m_new, l, acc

    wide = DSA_HEADS * blk
    init = (jnp.full((1, wide), NEG_BIG, F32), jnp.zeros((1, wide), F32), jnp.zeros((HEAD_DIM, wide), F32))
    _, l, acc = lax.fori_loop(0, n_kb, att_body, init)
    o_t = acc / l
    o = jnp.concatenate(
        [jnp.concatenate([o_t[:, (2 * i) * blk:(2 * i + 1) * blk], o_t[:, (2 * i + 1) * blk:(2 * i + 2) * blk]],
                         axis=0).T for i in range(DSA_HEADS // 2)], axis=1)
    g = zg_ref[...]
    o_ref[...] = (o * (g * _sigmoid(g))).astype(o_ref.dtype)


def _dsa(z2d, batch, seq, c64, s64, c32, s32):
    blk = ATT_BLOCK
    nq = seq // blk
    qspec = lambda u2: pl.BlockSpec((blk, 2 * LANES), lambda b, i, u2=u2: (b * nq + i, u2))
    seqspec = lambda u: pl.BlockSpec((seq, LANES), lambda b, i, u=u: (b, u))
    tab = pl.BlockSpec((seq, LANES), lambda b, i: (0, 0))
    return pl.pallas_call(
        _dsa_kernel,
        grid=(batch, nq),
        in_specs=[qspec(U_DS_Q // 2), qspec(U_DS_G // 2), qspec(U_DS_IQ // 2),
                  seqspec(U_DS_KV), seqspec(U_DS_MISC), tab, tab, tab, tab],
        out_specs=pl.BlockSpec((blk, DSA_WIDTH), lambda b, i: (b * nq + i, 0)),
        out_shape=jax.ShapeDtypeStruct((batch * seq, DSA_WIDTH), BF16),
        scratch_shapes=[
            pltpu.VMEM((seq, HEAD_DIM), BF16),
            pltpu.VMEM((seq // DSA_KEY_BLOCK, HEAD_DIM, DSA_KEY_BLOCK), BF16),
            pltpu.VMEM((seq, IDX_DIM), BF16),
            pltpu.VMEM((seq, blk), I32),
            pltpu.VMEM((seq, blk), F32),
        ],
        compiler_params=pltpu.CompilerParams(
            dimension_semantics=("parallel", "arbitrary"), vmem_limit_bytes=VMEM_LIMIT),
        name="dsa",
    )(z2d, z2d, z2d, z2d, z2d, c64, s64, c32, s32)


def _dil_group(gi, rate, q_ref, k_ref, v_ref, g_ref, c64_ref, s64_ref, qr_ref, kr_ref, o_s, lse_s):
    seq = q_ref.shape[0]
    blk = ATT_BLOCK
    nrow = seq // blk

    def rope_body(t, carry):
        rows = pl.ds(pl.multiple_of(t * blk, blk), blk)
        ct = c64_ref[rows, :]
        st = s64_ref[rows, :]
        qr_ref[rows, :] = _rope(q_ref[rows, :], ct, st, 8, HEAD_DIM) * QK_SCALE_LOG2
        kr_ref[rows, :] = _rope(k_ref[rows, :], ct, st, 8, HEAD_DIM)
        return carry

    lax.fori_loop(0, nrow, rope_body, 0)

    nblk = seq // (blk * rate)
    qi = lax.broadcasted_iota(I32, (blk, 2 * blk), 0)
    kj = lax.broadcasted_iota(I32, (blk, 2 * blk), 1)
    band = (kj >= qi) & (kj <= qi + DIL_STEPS)

    def strided(ref, start):
        if rate == 1:
            return ref[pl.ds(start, blk), :]
        return ref[pl.ds(start, blk, stride=rate), :]

    def store(ref, start, val):
        if rate == 1:
            ref[gi, pl.ds(start, blk), :] = val
        else:
            ref[gi, pl.ds(start, blk, stride=rate), :] = val

    first_q = lax.broadcasted_iota(I32, (blk, LANES), 1) < HEAD_DIM
    first_kv = lax.broadcasted_iota(I32, (2 * blk, LANES), 1) < HEAD_DIM
    unroll = DIL_UNROLL

    def body(it, carry):
        starts, qs, ks, vs, masks = [], [], [], [], []
        for u in range(unroll):
            cidx = it * unroll + u
            rho = cidx // nblk
            j = cidx % nblk
            start = j * (blk * rate) + rho
            pstart = jnp.maximum(j - 1, 0) * (blk * rate) + rho
            starts.append(start)
            qs.append(strided(qr_ref, start))
            ks.append(jnp.concatenate([strided(kr_ref, pstart), strided(kr_ref, start)], axis=0).astype(BF16))
            vs.append(jnp.concatenate([strided(v_ref, pstart), strided(v_ref, start)], axis=0))
            masks.append(band & ((kj >= blk) | (j > 0)))
        prob = [(u, h) for u in range(unroll) for h in range(2)]
        qh = [jnp.where(first_q == (h == 0), qs[u], 0.0).astype(BF16) for u, h in prob]
        vh = [jnp.where(first_kv == (h == 0), vs[u], 1.0).astype(BF16) for u, h in prob]
        s = [jnp.where(masks[u], _dot_nt(qh[i], ks[u]), -jnp.inf) for i, (u, h) in enumerate(prob)]
        m = [jnp.max(x, axis=-1, keepdims=True) for x in s]
        r = [_dot(jnp.exp2(s[i] - m[i]).astype(BF16), vh[i]) for i in range(len(prob))]
        for u in range(unroll):
            r0, r1 = r[2 * u], r[2 * u + 1]
            o = jnp.where(first_q, r0, r1)
            l = pltpu.roll(jnp.where(first_q, r1, r0), HEAD_DIM, 1)
            lse = jnp.where(first_q, m[2 * u], m[2 * u + 1]) * math.log(2.0) + jnp.log(l)
            gate = strided(g_ref, starts[u])
            store(o_s, starts[u], o / l * (gate * _sigmoid(gate)))
            store(lse_s, starts[u], lse)
        return carry

    lax.fori_loop(0, rate * nblk // unroll, body, 0)


def _dil_kernel(q_ref, k_ref, v_ref, g_ref, c64_ref, s64_ref, o_ref, qr_ref, kr_ref, o_s, lse_s):
    grp = pl.program_id(1)
    for gi, rate in enumerate(DIL_RATES):
        pl.when(grp == gi)(functools.partial(
            _dil_group, gi, rate, q_ref, k_ref, v_ref, g_ref, c64_ref, s64_ref, qr_ref, kr_ref, o_s, lse_s))

    @pl.when(grp == len(DIL_RATES) - 1)
    def _():
        blk = ATT_BLOCK

        def body(t, carry):
            rows = pl.ds(pl.multiple_of(t * blk, blk), blk)
            ls = [lse_s[i, rows, :] for i in range(3)]
            top = jnp.maximum(jnp.maximum(ls[0], ls[1]), ls[2])
            es = [jnp.exp(x - top) for x in ls]
            inv = 1.0 / (es[0] + es[1] + es[2])
            o_ref[rows, :] = jnp.concatenate(
                [o_s[i, rows, :] * (es[i] * inv) for i in range(3)], axis=1).astype(o_ref.dtype)
            return carry

        lax.fori_loop(0, q_ref.shape[0] // blk, body, 0)


def _dilated(z2d, batch, seq, c64, s64):
    ng = len(DIL_RATES)
    zspec = lambda base: pl.BlockSpec((seq, LANES), lambda b, g, base=base: (b, base + g))
    tab = pl.BlockSpec((seq, LANES), lambda b, g: (0, 0))
    return pl.pallas_call(
        _dil_kernel,
        grid=(batch, ng),
        in_specs=[zspec(U_DL), zspec(U_DL + 3), zspec(U_DL + 6), zspec(U_DL + 9), tab, tab],
        out_specs=pl.BlockSpec((seq, DIL_WIDTH), lambda b, g: (b, 0)),
        out_shape=jax.ShapeDtypeStruct((batch * seq, DIL_WIDTH), BF16),
        scratch_shapes=[
            pltpu.VMEM((seq, LANES), F32),
            pltpu.VMEM((seq, LANES), F32),
            pltpu.VMEM((ng, seq, LANES), F32),
            pltpu.VMEM((ng, seq, LANES), F32),
        ],
        compiler_params=pltpu.CompilerParams(
            dimension_semantics=("parallel", "arbitrary"), vmem_limit_bytes=VMEM_LIMIT),
        name="dilated",
    )(z2d, z2d, z2d, z2d, c64, s64)


def _rope_tables(seq, head, half):
    inv = ROPE_THETA ** (-jnp.arange(half, dtype=F32) / half)
    ang = jnp.arange(seq, dtype=F32)[:, None] * inv[None, :]
    cos, sin = jnp.cos(ang), jnp.sin(ang)
    lane = jnp.arange(LANES) % head
    idx = lane % half
    cos_t = jnp.where(lane[None, :] < 2 * half, cos[:, idx], 1.0)
    sin_t = jnp.where(lane[None, :] < half, -sin[:, idx], jnp.where(lane[None, :] < 2 * half, sin[:, idx], 0.0))
    return cos_t, sin_t


def _permute_in_cols(w):
    rw = RWKV_WIDTH
    o = 0
    r, k, v, g = (w[..., o + i * rw:o + (i + 1) * rw] for i in range(4))
    o = 4 * rw
    wd, ad = w[..., o:o + LORA_RANK], w[..., o + LORA_RANK:o + 2 * LORA_RANK]
    o += 2 * LORA_RANK
    widths = (DSA_WIDTH, HEAD_DIM, HEAD_DIM, IDX_HEADS * IDX_DIM, IDX_DIM, IDX_HEADS, DSA_WIDTH)
    parts = []
    for wdt in widths:
        parts.append(w[..., o:o + wdt])
        o += wdt
    dq, dk, dv, diq, dik, diw, dg = parts
    cq, ck, cv, cg = (w[..., o + i * DIL_WIDTH:o + (i + 1) * DIL_WIDTH] for i in range(4))
    pad = jnp.zeros(w.shape[:-1] + (LANES - IDX_DIM - IDX_HEADS,), w.dtype)
    return jnp.concatenate([r, k, v, g, cq, ck, cv, cg, dq, dg, diq, dk, dv, dik, diw, pad, wd, ad], axis=-1)


def kernel(x, p, norm_g, w_in, tshift_mu, rwkv_w0, rwkv_w_up, rwkv_a0, rwkv_a_up, rwkv_k_k, rwkv_k_a, rwkv_r_k,
           rwkv_ln_g, rwkv_ln_b, w_out, ple_norm_g, ple_w_gate, ple_w_proj, final_norm_g):
    batch, seq, d_model = x.shape
    depth = w_in.shape[0]
    n = batch * seq
    assert seq % (ATT_BLOCK * max(DIL_RATES)) == 0 and seq >= 4 * DSA_TOPK

    c64, s64 = _rope_tables(seq, HEAD_DIM, HEAD_DIM // 8)
    c32, s32 = _rope_tables(seq, IDX_DIM, IDX_DIM // 8)
    w_in_k = _permute_in_cols(w_in).astype(BF16)
    rw = RWKV_WIDTH
    row = lambda a: a.reshape(1, -1)

    x2d = x.reshape(n, d_model)
    for i in range(depth):
        z = _in_proj(x2d, row(norm_g[i]), w_in_k[i])
        mu = tshift_mu[i]
        y_a = _rwkv(z, batch, seq, mu[:4 * rw].reshape(4, rw), row(mu[4 * rw:]), row(rwkv_w0[i]), rwkv_w_up[i],
                    row(rwkv_a0[i]), rwkv_a_up[i], row(rwkv_k_k[i]), row(rwkv_k_a[i]), row(rwkv_r_k[i]),
                    row(rwkv_ln_g[i]), row(rwkv_ln_b[i]))
        y_b = _dsa(z, batch, seq, c64, s64, c32, s32)
        y_c = _dilated(z, batch, seq, c64, s64)
        wo = w_out[i].astype(BF16)
        x2d = _out_proj(x2d, y_a, y_b, y_c, p[i].reshape(n, -1), wo[:rw], wo[rw:rw + DSA_WIDTH],
                        wo[rw + DSA_WIDTH:], row(ple_norm_g[i]), ple_w_gate[i].astype(BF16),
                        ple_w_proj[i].astype(BF16), row(final_norm_g), final=(i == depth - 1))
    return x2d.reshape(batch, seq, d_model)
```

```python
import functools
import math

import jax
import jax.numpy as jnp
from jax import lax
from jax.experimental import pallas as pl
from jax.experimental.pallas import tpu as pltpu

F32 = jnp.float32
BF16 = jnp.bfloat16
I32 = jnp.int32

HEAD_DIM = 64
ROPE_THETA = 500000.0
NORM_EPS = 1e-6
RWKV_HEADS = 6
RWKV_WIDTH = RWKV_HEADS * HEAD_DIM
LORA_RANK = 64
GN_EPS = 64e-5
DECAY_SCALE = math.exp(-0.5)
DSA_HEADS = 4
DSA_WIDTH = DSA_HEADS * HEAD_DIM
IDX_HEADS = 8
IDX_DIM = 32
DSA_TOPK = 256
DIL_RATES = (1, 4, 16)
DIL_STEPS = 128
DIL_UNROLL = 4
DIL_WIDTH = 384
ATT_BLOCK = 128
DSA_KEY_BLOCK = 512

LANES = 128
MXU_WIDTH = 256
VMEM_LIMIT = 48 * 1024 * 1024

U_RW = 0
U_DL = 12
U_DS_Q = 24
U_DS_G = 26
U_DS_IQ = 28
U_DS_KV = 30
U_DS_MISC = 31
U_RW_WA = 32
Z_UNITS = 33
Z_COLS = Z_UNITS * LANES

RWKV_CHUNK = 128
RWKV_BATCH_ROWS = 4
INV_BASE = 16
QK_SCALE_LOG2 = HEAD_DIM ** -0.5 * math.log2(math.e)
NEG_BIG = -1e30
INT_MIN = -2 ** 31


def _dot(a, b):
    return jnp.dot(a, b, preferred_element_type=F32)


def _dot_nt(a, b):
    return lax.dot_general(a, b, (((1,), (1,)), ((), ())), preferred_element_type=F32)


def _split(x):
    hi = x.astype(BF16)
    lo = (x - hi.astype(F32)).astype(BF16)
    return hi, lo


def _head_sum(x, head_ones):
    return _dot(x.astype(BF16), head_ones)


def _dot_x3(x, w):
    xh, xl = _split(x)
    wh, wl = _split(w)
    return _dot(xh, wh) + (_dot(xh, wl) + _dot(xl, wh))


def _sigmoid(x):
    return 1.0 / (1.0 + jnp.exp(-x))


def _rope(x, cos_t, sin_t, half, head):
    lane = lax.broadcasted_iota(I32, x.shape, 1) % head
    partner = jnp.where(lane < half, pltpu.roll(x, LANES - half, 1), pltpu.roll(x, half, 1))
    return x * cos_t + partner * sin_t


def _in_proj_kernel(x_ref, g_ref, w_ref, z_ref, *, tn):
    x = x_ref[...]
    ms = jnp.mean(x * x, axis=-1, keepdims=True)
    h = (x * lax.rsqrt(ms + NORM_EPS) * g_ref[...]).astype(BF16)
    for c in range(0, Z_COLS, tn):
        wdt = min(tn, Z_COLS - c)
        z_ref[:, c:c + wdt] = _dot(h, w_ref[:, c:c + wdt])


def _in_proj(x2d, g, w, tm=512, tn=MXU_WIDTH):
    n, d = x2d.shape
    return pl.pallas_call(
        functools.partial(_in_proj_kernel, tn=tn),
        grid=(n // tm,),
        in_specs=[
            pl.BlockSpec((tm, d), lambda i: (i, 0)),
            pl.BlockSpec((1, d), lambda i: (0, 0)),
            pl.BlockSpec((d, Z_COLS), lambda i: (0, 0)),
        ],
        out_specs=pl.BlockSpec((tm, Z_COLS), lambda i: (i, 0)),
        out_shape=jax.ShapeDtypeStruct((n, Z_COLS), F32),
        compiler_params=pltpu.CompilerParams(
            dimension_semantics=("parallel",), vmem_limit_bytes=VMEM_LIMIT),
        name="in_proj",
    )(x2d, g, w)


def _out_proj_kernel(x_ref, ya_ref, yb_ref, yc_ref, p_ref, wa_ref, wb_ref, wc_ref, ng_ref, wg_ref,
                     wp_ref, fg_ref, o_ref, *, final):
    acc = x_ref[...] + _dot(ya_ref[...], wa_ref[...])
    acc = acc + _dot(yb_ref[...], wb_ref[...])
    acc = acc + _dot(yc_ref[...], wc_ref[...])
    ms = jnp.mean(acc * acc, axis=-1, keepdims=True)
    hn = (acc * lax.rsqrt(ms + NORM_EPS) * ng_ref[...]).astype(BF16)
    gate = _sigmoid(_dot(hn, wg_ref[...]))
    out = acc + gate * _dot(p_ref[...].astype(BF16), wp_ref[...])
    if final:
        ms = jnp.mean(out * out, axis=-1, keepdims=True)
        out = out * lax.rsqrt(ms + NORM_EPS) * fg_ref[...]
    o_ref[...] = out


def _out_proj(x2d, ya, yb, yc, p2d, wa, wb, wc, ng, wg, wp, fg, final, tm=512):
    n, d = x2d.shape
    row = lambda w: pl.BlockSpec((tm, w), lambda i: (i, 0))
    full = lambda a: pl.BlockSpec(a.shape, lambda i: (0, 0))
    return pl.pallas_call(
        functools.partial(_out_proj_kernel, final=final),
        grid=(n // tm,),
        in_specs=[row(d), row(ya.shape[1]), row(yb.shape[1]), row(yc.shape[1]), row(p2d.shape[1]),
                  full(wa), full(wb), full(wc), full(ng), full(wg), full(wp), full(fg)],
        out_specs=row(d),
        out_shape=jax.ShapeDtypeStruct((n, d), F32),
        compiler_params=pltpu.CompilerParams(
            dimension_semantics=("parallel",), vmem_limit_bytes=VMEM_LIMIT),
        name="out_proj",
    )(x2d, ya, yb, yc, p2d, wa, wb, wc, ng, wg, wp, fg)


def _unit_lower_inverse(n_list, row, col):
    c = n_list[0].shape[0]
    eye = jnp.where(row == col, 1.0, 0.0).astype(F32)
    same = (row // INV_BASE) == (col // INV_BASE)
    pw = [jnp.where(same, n, 0.0) for n in n_list]
    t = [eye - x for x in pw]
    span = 2
    while span < INV_BASE:
        pwb = [x.astype(BF16) for x in pw]
        pw = [_dot(x, x) for x in pwb]
        t = [x + _dot(x.astype(BF16), y.astype(BF16)) for x, y in zip(t, pw)]
        span *= 2
    size = INV_BASE
    while size < c:
        lower_left = ((row // (2 * size)) == (col // (2 * size))) & ((row // size) != (col // size))
        tb = [x.astype(BF16) for x in t]
        et = [_dot(jnp.where(lower_left, n, 0.0).astype(BF16), x).astype(BF16) for n, x in zip(n_list, tb)]
        t = [x - _dot(xb, y) for x, xb, y in zip(t, tb, et)]
        size *= 2
    return t


def _rwkv_kernel(zr_ref, zk_ref, zv_ref, zg_ref, zwa_ref, mu_ref, muwa_ref, w0_ref, wup_ref, a0_ref,
                 aup_ref, kk_ref, ka_ref, rk_ref, lng_ref, lnb_ref, o_ref,
                 state_ref, prev_ref, prevwa_ref, y_ref):
    nb, c = zr_ref.shape[0], zr_ref.shape[1]
    width = RWKV_WIDTH

    @pl.when(pl.program_id(1) == 0)
    def _():
        state_ref[...] = jnp.zeros_like(state_ref)
        prev_ref[...] = jnp.zeros_like(prev_ref)
        prevwa_ref[...] = jnp.zeros_like(prevwa_ref)

    first_row = lax.broadcasted_iota(I32, (c, width), 0) == 0
    first_row_wa = lax.broadcasted_iota(I32, (c, LANES), 0) == 0
    hrow = lax.broadcasted_iota(I32, (width, width), 0) // HEAD_DIM
    hcol = lax.broadcasted_iota(I32, (width, width), 1) // HEAD_DIM
    head_ones = jnp.where(hrow == hcol, 1.0, 0.0).astype(BF16)
    row = lax.broadcasted_iota(I32, (c, c), 0)
    col = lax.broadcasted_iota(I32, (c, c), 1)
    tri = jnp.where(col <= row, 1.0, 0.0).astype(BF16)
    strict = col < row
    incl = col <= row
    hs = [slice(h * HEAD_DIM, (h + 1) * HEAD_DIM) for h in range(RWKV_HEADS)]

    def prepare(bi):
        def shifted(z_ref, idx):
            z = z_ref[bi]
            prev = jnp.where(first_row, prev_ref[bi, idx, 0:1, :], pltpu.roll(z, 1, 0))
            prev_ref[bi, idx, 0:1, :] = z[c - 1:c, :]
            return z + (prev - z) * mu_ref[idx:idx + 1, :]

        r = shifted(zr_ref, 0)
        k = shifted(zk_ref, 1)
        v = shifted(zv_ref, 2)
        g = shifted(zg_ref, 3)
        zwa = zwa_ref[bi]
        prev_wa = jnp.where(first_row_wa, prevwa_ref[bi, 0:1, :], pltpu.roll(zwa, 1, 0))
        prevwa_ref[bi, 0:1, :] = zwa[c - 1:c, :]
        wa = zwa + (prev_wa - zwa) * muwa_ref[...]
        wd = wa[:, :LORA_RANK]
        ad = wa[:, LORA_RANK:]

        lw = -DECAY_SCALE * _sigmoid(w0_ref[...] + _dot_x3(jnp.tanh(wd), wup_ref[...]))
        a = _sigmoid(a0_ref[...] + _dot_x3(ad, aup_ref[...]))
        kk = k * kk_ref[...]
        norm = jnp.sqrt(_head_sum(kk * kk, head_ones))
        kk = kk / jnp.maximum(norm, 1e-12)
        kmod = k * (1.0 + (a - 1.0) * ka_ref[...])
        b = a * kk

        lw_hi, lw_lo = _split(lw)
        cl = _dot(tri, lw_hi) + _dot(tri, lw_lo)
        c_mid = cl[c // 2 - 1:c // 2, :]
        c_end = cl[c - 1:c, :]
        w = jnp.exp(lw)
        e_in1 = jnp.exp(cl - lw)
        e_in2 = e_in1 * w
        e_k = jnp.exp(c_mid - cl)
        e_mid_inv = jnp.exp(-c_mid)
        e_end = e_k * jnp.exp(c_end - c_mid)
        ops = dict(
            q1=kk * e_in1 * e_mid_inv, q2=r * e_in2 * e_mid_inv, k1=b * e_k, k2=kmod * e_k,
            kkd=kk * e_in1, rd=r * e_in2, bd=b * e_end, kd=kmod * e_end, vb=v)
        ops = {name: [x.astype(BF16)[:, s] for s in hs] for name, x in ops.items()}
        return dict(ops, r=r, kmod=kmod, v=v, g=g, d_end=jnp.exp(c_end))

    prep = [prepare(bi) for bi in range(nb)]

    probs = [(bi, h) for bi in range(nb) for h in range(RWKV_HEADS)]
    op = lambda name: [prep[bi][name][h] for bi, h in probs]
    q1, q2, k1, k2, kkd, rd, bd, kd, vh = map(op, ("q1", "q2", "k1", "k2", "kkd", "rd", "bd", "kd", "vb"))
    n = range(len(probs))
    a_ab = [jnp.where(strict, _dot_nt(q1[i], k1[i]), 0.0) for i in n]
    t_inv = _unit_lower_inverse(a_ab, row, col)
    a_ak = [jnp.where(strict, _dot_nt(q1[i], k2[i]), 0.0).astype(BF16) for i in n]
    a_rb = [jnp.where(incl, _dot_nt(q2[i], k1[i]), 0.0).astype(BF16) for i in n]
    a_rk = [jnp.where(incl, _dot_nt(q2[i], k2[i]), 0.0).astype(BF16) for i in n]
    s_old = [state_ref[bi, h] for bi, h in probs]
    s_in = [s.astype(BF16) for s in s_old]
    rhs = [_dot_nt(kkd[i], s_in[i]) + _dot(a_ak[i], vh[i]) for i in n]
    u = [-_dot(t_inv[i].astype(BF16), rhs[i].astype(BF16)) for i in n]
    uv_t = [jnp.concatenate([u[i], prep[bi]["v"][:, hs[h]]], axis=1).T.astype(BF16)
            for i, (bi, h) in enumerate(probs)]
    for i, (bi, h) in enumerate(probs):
        state_ref[bi, h] = (s_old[i] * prep[bi]["d_end"][:, hs[h]]
                            + _dot(uv_t[i][:HEAD_DIM, :], bd[i]) + _dot(uv_t[i][HEAD_DIM:, :], kd[i]))
    for i, (bi, h) in enumerate(probs):
        y_ref[bi, :, hs[h]] = (_dot_nt(rd[i], s_in[i]) + _dot(a_rb[i], u[i].astype(BF16))
                               + _dot(a_rk[i], vh[i]))

    inv_n = 1.0 / HEAD_DIM
    for bi in range(nb):
        pb = prep[bi]
        y = y_ref[bi]
        mean = _head_sum(y, head_ones) * inv_n
        dlt = y - mean
        var = _head_sum(dlt * dlt, head_ones) * inv_n
        yn = dlt * lax.rsqrt(var + GN_EPS) * lng_ref[...] + lnb_ref[...]
        bonus = _head_sum(pb["r"] * pb["kmod"] * rk_ref[...], head_ones) * pb["v"]
        g = pb["g"]
        o_ref[bi] = ((yn + bonus) * (g * _sigmoid(g))).astype(o_ref.dtype)


def _rwkv(z3d, mu4, mu_wa, w0, w_up, a0, a_up, k_k, k_a, r_k, ln_g, ln_b):
    batch, seq, _ = z3d.shape
    c = RWKV_CHUNK
    nb = RWKV_BATCH_ROWS if batch % RWKV_BATCH_ROWS == 0 else 1
    zspec = lambda u: pl.BlockSpec((nb, c, RWKV_WIDTH), lambda b, i, u=u: (b, i, u))
    full = lambda a: pl.BlockSpec(a.shape, lambda b, i: (0,) * a.ndim)
    params = (mu4, mu_wa, w0, w_up, a0, a_up, k_k, k_a, r_k, ln_g, ln_b)
    return pl.pallas_call(
        _rwkv_kernel,
        grid=(batch // nb, seq // c),
        in_specs=[zspec(0), zspec(1), zspec(2), zspec(3),
                  pl.BlockSpec((nb, c, LANES), lambda b, i: (b, i, U_RW_WA))]
                 + [full(a) for a in params],
        out_specs=pl.BlockSpec((nb, c, RWKV_WIDTH), lambda b, i: (b, i, 0)),
        out_shape=jax.ShapeDtypeStruct((batch, seq, RWKV_WIDTH), BF16),
        scratch_shapes=[
            pltpu.VMEM((nb, RWKV_HEADS, HEAD_DIM, HEAD_DIM), F32),
            pltpu.VMEM((nb, 4, 8, RWKV_WIDTH), F32),
            pltpu.VMEM((nb, 8, LANES), F32),
            pltpu.VMEM((nb, c, RWKV_WIDTH), F32),
        ],
        compiler_params=pltpu.CompilerParams(
            dimension_semantics=("parallel", "arbitrary"), vmem_limit_bytes=VMEM_LIMIT),
        name="rwkv7",
    )(z3d, z3d, z3d, z3d, z3d, *params)


def _dsa_kernel(zq_ref, zg_ref, ziq_ref, zkv_ref, zms_ref, c64_ref, s64_ref, c32_ref, s32_ref, o_ref,
                k_ref, vt_ref, ik_ref, key_ref, bias_ref):
    seq = zkv_ref.shape[0]
    blk = ATT_BLOCK
    kb_rows = DSA_KEY_BLOCK
    qb = pl.program_id(1)
    n_kb = (qb * blk) // kb_rows + 1

    @pl.when(qb == 0)
    def _():
        lane = lax.broadcasted_iota(I32, (kb_rows, LANES), 1)

        def body(t, carry):
            rows = pl.ds(pl.multiple_of(t * kb_rows, kb_rows), kb_rows)
            kv = zkv_ref[rows, :]
            kvr = jnp.where(lane < HEAD_DIM, _rope(kv, c64_ref[rows, :], s64_ref[rows, :], 8, HEAD_DIM), kv)
            k_ref[rows, :] = kvr[:, :HEAD_DIM].astype(BF16)
            vt_ref[t] = kvr.T[HEAD_DIM:, :].astype(BF16)
            ms = zms_ref[rows, :]
            ikr = _rope(ms, c32_ref[rows, :], s32_ref[rows, :], 4, IDX_DIM)
            ik_ref[rows, :] = ikr[:, :IDX_DIM].astype(BF16)
            return carry

        lax.fori_loop(0, seq // kb_rows, body, 0)

    r0 = pl.multiple_of(qb * blk, blk)
    qrows = pl.ds(r0, blk)
    c64 = c64_ref[qrows, :]
    s64 = s64_ref[qrows, :]
    c32 = c32_ref[qrows, :]
    s32 = s32_ref[qrows, :]
    q2 = [_rope(zq_ref[:, i * LANES:(i + 1) * LANES], c64, s64, 8, HEAD_DIM) * QK_SCALE_LOG2
          for i in range(2)]
    q_stack = jnp.concatenate(
        [q2[h // 2][:, (h % 2) * HEAD_DIM:(h % 2 + 1) * HEAD_DIM] for h in range(DSA_HEADS)],
        axis=0).astype(BF16)
    iq2 = [_rope(ziq_ref[:, i * LANES:(i + 1) * LANES], c32, s32, 4, IDX_DIM) for i in range(2)]
    iq_stack = jnp.concatenate(
        [iq2[h // 4][:, (h % 4) * IDX_DIM:(h % 4 + 1) * IDX_DIM] for h in range(IDX_HEADS)],
        axis=0).astype(BF16)
    iw_t = zms_ref[qrows, :].T[IDX_DIM:IDX_DIM + IDX_HEADS, :] * ((IDX_HEADS * IDX_DIM) ** -0.5)
    iw_row = jnp.concatenate([iw_t[h:h + 1, :] for h in range(IDX_HEADS)], axis=1)

    kpos_l = lax.broadcasted_iota(I32, (kb_rows, blk), 0)
    qpos = r0 + lax.broadcasted_iota(I32, (kb_rows, blk), 1)

    def key_rows(kb):
        return pl.ds(pl.multiple_of(kb * kb_rows, kb_rows), kb_rows)

    def score_body(kb, carry):
        s = jnp.maximum(_dot_nt(ik_ref[key_rows(kb), :], iq_stack), 0.0) * iw_row
        sc = s[:, 0:blk]
        for h in range(1, IDX_HEADS):
            sc = sc + s[:, h * blk:(h + 1) * blk]
        sc = jnp.where(kb * kb_rows + kpos_l <= qpos, sc, -jnp.inf)
        bits = pltpu.bitcast(sc, I32)
        key_ref[key_rows(kb), :] = bits ^ ((bits >> 31) & 0x7FFFFFFF)
        return carry

    lax.fori_loop(0, n_kb, score_body, 0)

    def count(pred_fn):
        def body(kb, acc):
            hit = jnp.where(pred_fn(key_ref[key_rows(kb), :]), 1, 0).astype(I32)
            return acc + hit.reshape(kb_rows // 8, 8, blk).sum(axis=0)

        acc = lax.fori_loop(0, n_kb, body, jnp.zeros((8, blk), I32))
        return acc.sum(axis=0, keepdims=True)

    def bit_body(i, lo):
        cand = lo + (jnp.int32(1) << (31 - i))
        return jnp.where(count(lambda keys: keys >= cand) >= DSA_TOPK, cand, lo)

    lo0 = jnp.full((1, blk), INT_MIN, I32)
    thr = lax.cond(qb * blk >= DSA_TOPK, lambda: lax.fori_loop(0, 32, bit_body, lo0), lambda: lo0)
    need = (DSA_TOPK - count(lambda keys: keys > thr)).astype(F32)

    tri = jnp.where(lax.broadcasted_iota(I32, (kb_rows, kb_rows), 1)
                    <= lax.broadcasted_iota(I32, (kb_rows, kb_rows), 0), 1.0, 0.0).astype(BF16)

    def select_body(kb, run):
        keys = key_ref[key_rows(kb), :]
        eq = keys == thr
        rank = _dot(tri, jnp.where(eq, 1.0, 0.0).astype(BF16)) + run
        take = (keys > thr) | (eq & (rank <= need))
        take = take & (kb * kb_rows + kpos_l <= qpos)
        bias_ref[key_rows(kb), :] = jnp.where(take, 0.0, NEG_BIG)
        return rank[kb_rows - 1:kb_rows, :]

    lax.fori_loop(0, n_kb, select_body, jnp.zeros((1, blk), F32))

    def att_body(kb, carry):
        m, l, acc = carry
        bias = bias_ref[key_rows(kb), :]
        s = _dot_nt(k_ref[key_rows(kb), :], q_stack)
        s = s + jnp.concatenate([bias] * DSA_HEADS, axis=1)
        m_new = jnp.maximum(m, jnp.max(s, axis=0, keepdims=True))
        p = jnp.exp2(s - m_new)
        alpha = jnp.exp2(m - m_new)
        l = alpha * l + jnp.sum(p, axis=0, keepdims=True)
        acc = alpha * acc + _dot(vt_ref[kb], p.astype(BF16))
        return m_new, l, acc

    wide = DSA_HEADS * blk
    init = (jnp.full((1, wide), NEG_BIG, F32), jnp.zeros((1, wide), F32), jnp.zeros((HEAD_DIM, wide), F32))
    _, l, acc = lax.fori_loop(0, n_kb, att_body, init)
    o_t = acc / l
    o = jnp.concatenate(
        [jnp.concatenate([o_t[:, (2 * i) * blk:(2 * i + 1) * blk], o_t[:, (2 * i + 1) * blk:(2 * i + 2) * blk]],
                         axis=0).T for i in range(DSA_HEADS // 2)], axis=1)
    g = zg_ref[...]
    o_ref[...] = (o * (g * _sigmoid(g))).astype(o_ref.dtype)


def _dsa(z2d, batch, seq, c64, s64, c32, s32):
    blk = ATT_BLOCK
    nq = seq // blk
    qspec = lambda u2: pl.BlockSpec((blk, 2 * LANES), lambda b, i, u2=u2: (b * nq + i, u2))
    seqspec = lambda u: pl.BlockSpec((seq, LANES), lambda b, i, u=u: (b, u))
    tab = pl.BlockSpec((seq, LANES), lambda b, i: (0, 0))
    return pl.pallas_call(
        _dsa_kernel,
        grid=(batch, nq),
        in_specs=[qspec(U_DS_Q // 2), qspec(U_DS_G // 2), qspec(U_DS_IQ // 2),
                  seqspec(U_DS_KV), seqspec(U_DS_MISC), tab, tab, tab, tab],
        out_specs=pl.BlockSpec((blk, DSA_WIDTH), lambda b, i: (b * nq + i, 0)),
        out_shape=jax.ShapeDtypeStruct((batch * seq, DSA_WIDTH), BF16),
        scratch_shapes=[
            pltpu.VMEM((seq, HEAD_DIM), BF16),
            pltpu.VMEM((seq // DSA_KEY_BLOCK, HEAD_DIM, DSA_KEY_BLOCK), BF16),
            pltpu.VMEM((seq, IDX_DIM), BF16),
            pltpu.VMEM((seq, blk), I32),
            pltpu.VMEM((seq, blk), F32),
        ],
        compiler_params=pltpu.CompilerParams(
            dimension_semantics=("parallel", "arbitrary"), vmem_limit_bytes=VMEM_LIMIT),
        name="dsa",
    )(z2d, z2d, z2d, z2d, z2d, c64, s64, c32, s32)


def _dil_group(gi, rate, q_ref, k_ref, v_ref, g_ref, c64_ref, s64_ref, qr_ref, kr_ref, o_s, lse_s):
    seq = q_ref.shape[0]
    blk = ATT_BLOCK
    nrow = seq // blk

    def rope_body(t, carry):
        rows = pl.ds(pl.multiple_of(t * blk, blk), blk)
        ct = c64_ref[rows, :]
        st = s64_ref[rows, :]
        qr_ref[rows, :] = _rope(q_ref[rows, :], ct, st, 8, HEAD_DIM) * QK_SCALE_LOG2
        kr_ref[rows, :] = _rope(k_ref[rows, :], ct, st, 8, HEAD_DIM)
        return carry

    lax.fori_loop(0, nrow, rope_body, 0)

    nblk = seq // (blk * rate)
    qi = lax.broadcasted_iota(I32, (blk, 2 * blk), 0)
    kj = lax.broadcasted_iota(I32, (blk, 2 * blk), 1)
    band = (kj >= qi) & (kj <= qi + DIL_STEPS)

    def strided(ref, start):
        if rate == 1:
            return ref[pl.ds(start, blk), :]
        return ref[pl.ds(start, blk, stride=rate), :]

    def store(ref, start, val):
        if rate == 1:
            ref[gi, pl.ds(start, blk), :] = val
        else:
            ref[gi, pl.ds(start, blk, stride=rate), :] = val

    first_q = lax.broadcasted_iota(I32, (blk, LANES), 1) < HEAD_DIM
    first_kv = lax.broadcasted_iota(I32, (2 * blk, LANES), 1) < HEAD_DIM
    unroll = DIL_UNROLL

    def body(it, carry):
        starts, qs, ks, vs, masks = [], [], [], [], []
        for u in range(unroll):
            cidx = it * unroll + u
            rho = cidx // nblk
            j = cidx % nblk
            start = j * (blk * rate) + rho
            pstart = jnp.maximum(j - 1, 0) * (blk * rate) + rho
            starts.append(start)
            qs.append(strided(qr_ref, start))
            ks.append(jnp.concatenate([strided(kr_ref, pstart), strided(kr_ref, start)], axis=0).astype(BF16))
            vs.append(jnp.concatenate([strided(v_ref, pstart), strided(v_ref, start)], axis=0))
            masks.append(band & ((kj >= blk) | (j > 0)))
        prob = [(u, h) for u in range(unroll) for h in range(2)]
        qh = [jnp.where(first_q == (h == 0), qs[u], 0.0).astype(BF16) for u, h in prob]
        vh = [jnp.where(first_kv == (h == 0), vs[u], 1.0).astype(BF16) for u, h in prob]
        s = [jnp.where(masks[u], _dot_nt(qh[i], ks[u]), -jnp.inf) for i, (u, h) in enumerate(prob)]
        m = [jnp.max(x, axis=-1, keepdims=True) for x in s]
        r = [_dot(jnp.exp2(s[i] - m[i]).astype(BF16), vh[i]) for i in range(len(prob))]
        for u in range(unroll):
            r0, r1 = r[2 * u], r[2 * u + 1]
            o = jnp.where(first_q, r0, r1)
            l = pltpu.roll(jnp.where(first_q, r1, r0), HEAD_DIM, 1)
            lse = jnp.where(first_q, m[2 * u], m[2 * u + 1]) * math.log(2.0) + jnp.log(l)
            gate = strided(g_ref, starts[u])
            store(o_s, starts[u], o / l * (gate * _sigmoid(gate)))
            store(lse_s, starts[u], lse)
        return carry

    lax.fori_loop(0, rate * nblk // unroll, body, 0)


def _dil_kernel(q_ref, k_ref, v_ref, g_ref, c64_ref, s64_ref, o_ref, qr_ref, kr_ref, o_s, lse_s):
    grp = pl.program_id(1)
    for gi, rate in enumerate(DIL_RATES):
        pl.when(grp == gi)(functools.partial(
            _dil_group, gi, rate, q_ref, k_ref, v_ref, g_ref, c64_ref, s64_ref, qr_ref, kr_ref, o_s, lse_s))

    @pl.when(grp == len(DIL_RATES) - 1)
    def _():
        blk = ATT_BLOCK

        def body(t, carry):
            rows = pl.ds(pl.multiple_of(t * blk, blk), blk)
            ls = [lse_s[i, rows, :] for i in range(3)]
            top = jnp.maximum(jnp.maximum(ls[0], ls[1]), ls[2])
            es = [jnp.exp(x - top) for x in ls]
            inv = 1.0 / (es[0] + es[1] + es[2])
            o_ref[rows, :] = jnp.concatenate(
                [o_s[i, rows, :] * (es[i] * inv) for i in range(3)], axis=1).astype(o_ref.dtype)
            return carry

        lax.fori_loop(0, q_ref.shape[0] // blk, body, 0)


def _dilated(z2d, batch, seq, c64, s64):
    ng = len(DIL_RATES)
    zspec = lambda base: pl.BlockSpec((seq, LANES), lambda b, g, base=base: (b, base + g))
    tab = pl.BlockSpec((seq, LANES), lambda b, g: (0, 0))
    return pl.pallas_call(
        _dil_kernel,
        grid=(batch, ng),
        in_specs=[zspec(U_DL), zspec(U_DL + 3), zspec(U_DL + 6), zspec(U_DL + 9), tab, tab],
        out_specs=pl.BlockSpec((seq, DIL_WIDTH), lambda b, g: (b, 0)),
        out_shape=jax.ShapeDtypeStruct((batch * seq, DIL_WIDTH), BF16),
        scratch_shapes=[
            pltpu.VMEM((seq, LANES), F32),
            pltpu.VMEM((seq, LANES), F32),
            pltpu.VMEM((ng, seq, LANES), F32),
            pltpu.VMEM((ng, seq, LANES), F32),
        ],
        compiler_params=pltpu.CompilerParams(
            dimension_semantics=("parallel", "arbitrary"), vmem_limit_bytes=VMEM_LIMIT),
        name="dilated",
    )(z2d, z2d, z2d, z2d, c64, s64)


def _rope_tables(seq, head, half):
    inv = ROPE_THETA ** (-jnp.arange(half, dtype=F32) / half)
    ang = jnp.arange(seq, dtype=F32)[:, None] * inv[None, :]
    cos, sin = jnp.cos(ang), jnp.sin(ang)
    lane = jnp.arange(LANES) % head
    idx = lane % half
    cos_t = jnp.where(lane[None, :] < 2 * half, cos[:, idx], 1.0)
    sin_t = jnp.where(lane[None, :] < half, -sin[:, idx], jnp.where(lane[None, :] < 2 * half, sin[:, idx], 0.0))
    return cos_t, sin_t


def _permute_in_cols(w):
    rw = RWKV_WIDTH
    o = 0
    r, k, v, g = (w[..., o + i * rw:o + (i + 1) * rw] for i in range(4))
    o = 4 * rw
    wd, ad = w[..., o:o + LORA_RANK], w[..., o + LORA_RANK:o + 2 * LORA_RANK]
    o += 2 * LORA_RANK
    widths = (DSA_WIDTH, HEAD_DIM, HEAD_DIM, IDX_HEADS * IDX_DIM, IDX_DIM, IDX_HEADS, DSA_WIDTH)
    parts = []
    for wdt in widths:
        parts.append(w[..., o:o + wdt])
        o += wdt
    dq, dk, dv, diq, dik, diw, dg = parts
    cq, ck, cv, cg = (w[..., o + i * DIL_WIDTH:o + (i + 1) * DIL_WIDTH] for i in range(4))
    pad = jnp.zeros(w.shape[:-1] + (LANES - IDX_DIM - IDX_HEADS,), w.dtype)
    return jnp.concatenate([r, k, v, g, cq, ck, cv, cg, dq, dg, diq, dk, dv, dik, diw, pad, wd, ad], axis=-1)


def kernel(x, p, norm_g, w_in, tshift_mu, rwkv_w0, rwkv_w_up, rwkv_a0, rwkv_a_up, rwkv_k_k, rwkv_k_a, rwkv_r_k,
           rwkv_ln_g, rwkv_ln_b, w_out, ple_norm_g, ple_w_gate, ple_w_proj, final_norm_g):
    batch, seq, d_model = x.shape
    depth = w_in.shape[0]
    n = batch * seq
    assert seq % (ATT_BLOCK * max(DIL_RATES)) == 0 and seq >= 4 * DSA_TOPK

    c64, s64 = _rope_tables(seq, HEAD_DIM, HEAD_DIM // 8)
    c32, s32 = _rope_tables(seq, IDX_DIM, IDX_DIM // 8)
    w_in_k = _permute_in_cols(w_in).astype(BF16)
    rw = RWKV_WIDTH
    row = lambda a: a.reshape(1, -1)

    x2d = x.reshape(n, d_model)
    for i in range(depth):
        z = _in_proj(x2d, row(norm_g[i]), w_in_k[i])
        mu = tshift_mu[i]
        y_a = _rwkv(z.reshape(batch, seq, Z_COLS), mu[:4 * rw].reshape(4, rw), row(mu[4 * rw:]), row(rwkv_w0[i]), rwkv_w_up[i],
                    row(rwkv_a0[i]), rwkv_a_up[i], row(rwkv_k_k[i]), row(rwkv_k_a[i]), row(rwkv_r_k[i]),
                    row(rwkv_ln_g[i]), row(rwkv_ln_b[i])).reshape(n, rw)
        y_b = _dsa(z, batch, seq, c64, s64, c32, s32)
        y_c = _dilated(z, batch, seq, c64, s64)
        wo = w_out[i].astype(BF16)
        x2d = _out_proj(x2d, y_a, y_b, y_c, p[i].reshape(n, -1), wo[:rw], wo[rw:rw + DSA_WIDTH],
                        wo[rw + DSA_WIDTH:], row(ple_norm_g[i]), ple_w_gate[i].astype(BF16),
                        ple_w_proj[i].astype(BF16), row(final_norm_g), final=(i == depth - 1))
    return x2d.reshape(batch, seq, d_model)
```

```python
import functools
import math

import jax
import jax.numpy as jnp
from jax import lax
from jax.experimental import pallas as pl
from jax.experimental.pallas import tpu as pltpu

F32 = jnp.float32
BF16 = jnp.bfloat16
I32 = jnp.int32

HEAD_DIM = 64
ROPE_THETA = 500000.0
NORM_EPS = 1e-6
RWKV_HEADS = 6
RWKV_WIDTH = RWKV_HEADS * HEAD_DIM
LORA_RANK = 64
GN_EPS = 64e-5
DECAY_SCALE = math.exp(-0.5)
DSA_HEADS = 4
DSA_WIDTH = DSA_HEADS * HEAD_DIM
IDX_HEADS = 8
IDX_DIM = 32
DSA_TOPK = 256
DIL_RATES = (1, 4, 16)
DIL_STEPS = 128
DIL_UNROLL = 4
DIL_WIDTH = 384
ATT_BLOCK = 128
DSA_KEY_BLOCK = 512

LANES = 128
MXU_WIDTH = 256
VMEM_LIMIT = 48 * 1024 * 1024

U_RW = 0
U_DL = 12
U_DS_Q = 24
U_DS_G = 26
U_DS_IQ = 28
U_DS_KV = 30
U_DS_MISC = 31
U_RW_WA = 32
Z_UNITS = 33
Z_COLS = Z_UNITS * LANES

RWKV_CHUNK = 128
RWKV_BATCH_ROWS = 4
INV_BASE = 16
QK_SCALE_LOG2 = HEAD_DIM ** -0.5 * math.log2(math.e)
NEG_BIG = -1e30
INT_MIN = -2 ** 31


def _dot(a, b):
    return jnp.dot(a, b, preferred_element_type=F32)


def _dot_nt(a, b):
    return lax.dot_general(a, b, (((1,), (1,)), ((), ())), preferred_element_type=F32)


def _split(x):
    hi = x.astype(BF16)
    lo = (x - hi.astype(F32)).astype(BF16)
    return hi, lo


def _head_sum(x, head_ones):
    return _dot(x.astype(BF16), head_ones)


def _dot_x3(x, w):
    xh, xl = _split(x)
    wh, wl = _split(w)
    return _dot(xh, wh) + (_dot(xh, wl) + _dot(xl, wh))


def _sigmoid(x):
    return 1.0 / (1.0 + jnp.exp(-x))


def _rope(x, cos_t, sin_t, half, head):
    lane = lax.broadcasted_iota(I32, x.shape, 1) % head
    partner = jnp.where(lane < half, pltpu.roll(x, LANES - half, 1), pltpu.roll(x, half, 1))
    return x * cos_t + partner * sin_t


def _in_proj_kernel(x_ref, g_ref, w_ref, z_ref, *, tn):
    x = x_ref[...]
    ms = jnp.mean(x * x, axis=-1, keepdims=True)
    h = (x * lax.rsqrt(ms + NORM_EPS) * g_ref[...]).astype(BF16)
    for c in range(0, Z_COLS, tn):
        wdt = min(tn, Z_COLS - c)
        z_ref[:, c:c + wdt] = _dot(h, w_ref[:, c:c + wdt])


def _in_proj(x2d, g, w, tm=512, tn=MXU_WIDTH):
    n, d = x2d.shape
    return pl.pallas_call(
        functools.partial(_in_proj_kernel, tn=tn),
        grid=(n // tm,),
        in_specs=[
            pl.BlockSpec((tm, d), lambda i: (i, 0)),
            pl.BlockSpec((1, d), lambda i: (0, 0)),
            pl.BlockSpec((d, Z_COLS), lambda i: (0, 0)),
        ],
        out_specs=pl.BlockSpec((tm, Z_COLS), lambda i: (i, 0)),
        out_shape=jax.ShapeDtypeStruct((n, Z_COLS), F32),
        compiler_params=pltpu.CompilerParams(
            dimension_semantics=("parallel",), vmem_limit_bytes=VMEM_LIMIT),
        name="in_proj",
    )(x2d, g, w)


def _out_proj_kernel(x_ref, ya_ref, yb_ref, yc_ref, p_ref, wa_ref, wb_ref, wc_ref, ng_ref, wg_ref,
                     wp_ref, fg_ref, o_ref, *, final):
    acc = x_ref[...] + _dot(ya_ref[...], wa_ref[...])
    acc = acc + _dot(yb_ref[...], wb_ref[...])
    acc = acc + _dot(yc_ref[...], wc_ref[...])
    ms = jnp.mean(acc * acc, axis=-1, keepdims=True)
    hn = (acc * lax.rsqrt(ms + NORM_EPS) * ng_ref[...]).astype(BF16)
    gate = _sigmoid(_dot(hn, wg_ref[...]))
    out = acc + gate * _dot(p_ref[...].astype(BF16), wp_ref[...])
    if final:
        ms = jnp.mean(out * out, axis=-1, keepdims=True)
        out = out * lax.rsqrt(ms + NORM_EPS) * fg_ref[...]
    o_ref[...] = out


def _out_proj(x2d, ya, yb, yc, p2d, wa, wb, wc, ng, wg, wp, fg, final, tm=512):
    n, d = x2d.shape
    row = lambda w: pl.BlockSpec((tm, w), lambda i: (i, 0))
    full = lambda a: pl.BlockSpec(a.shape, lambda i: (0, 0))
    return pl.pallas_call(
        functools.partial(_out_proj_kernel, final=final),
        grid=(n // tm,),
        in_specs=[row(d), row(ya.shape[1]), row(yb.shape[1]), row(yc.shape[1]), row(p2d.shape[1]),
                  full(wa), full(wb), full(wc), full(ng), full(wg), full(wp), full(fg)],
        out_specs=row(d),
        out_shape=jax.ShapeDtypeStruct((n, d), F32),
        compiler_params=pltpu.CompilerParams(
            dimension_semantics=("parallel",), vmem_limit_bytes=VMEM_LIMIT),
        name="out_proj",
    )(x2d, ya, yb, yc, p2d, wa, wb, wc, ng, wg, wp, fg)


def _unit_lower_inverse(n_list, row, col):
    c = n_list[0].shape[0]
    eye = jnp.where(row == col, 1.0, 0.0).astype(F32)
    same = (row // INV_BASE) == (col // INV_BASE)
    pw = [jnp.where(same, n, 0.0) for n in n_list]
    t = [eye - x for x in pw]
    span = 2
    while span < INV_BASE:
        pwb = [x.astype(BF16) for x in pw]
        pw = [_dot(x, x) for x in pwb]
        t = [x + _dot(x.astype(BF16), y.astype(BF16)) for x, y in zip(t, pw)]
        span *= 2
    size = INV_BASE
    while size < c:
        lower_left = ((row // (2 * size)) == (col // (2 * size))) & ((row // size) != (col // size))
        tb = [x.astype(BF16) for x in t]
        et = [_dot(jnp.where(lower_left, n, 0.0).astype(BF16), x).astype(BF16) for n, x in zip(n_list, tb)]
        t = [x - _dot(xb, y) for x, xb, y in zip(t, tb, et)]
        size *= 2
    return t


def _rwkv_kernel(zr_ref, zk_ref, zv_ref, zg_ref, zwa_ref, mu_ref, muwa_ref, w0_ref, wup_ref, a0_ref,
                 aup_ref, kk_ref, ka_ref, rk_ref, lng_ref, lnb_ref, o_ref,
                 state_ref, prev_ref, prevwa_ref, y_ref):
    nb, c = zr_ref.shape[0], zr_ref.shape[1]
    width = RWKV_WIDTH

    @pl.when(pl.program_id(1) == 0)
    def _():
        state_ref[...] = jnp.zeros_like(state_ref)
        prev_ref[...] = jnp.zeros_like(prev_ref)
        prevwa_ref[...] = jnp.zeros_like(prevwa_ref)

    first_row = lax.broadcasted_iota(I32, (c, width), 0) == 0
    first_row_wa = lax.broadcasted_iota(I32, (c, LANES), 0) == 0
    hrow = lax.broadcasted_iota(I32, (width, width), 0) // HEAD_DIM
    hcol = lax.broadcasted_iota(I32, (width, width), 1) // HEAD_DIM
    head_ones = jnp.where(hrow == hcol, 1.0, 0.0).astype(BF16)
    row = lax.broadcasted_iota(I32, (c, c), 0)
    col = lax.broadcasted_iota(I32, (c, c), 1)
    tri = jnp.where(col <= row, 1.0, 0.0).astype(BF16)
    strict = col < row
    incl = col <= row
    hs = [slice(h * HEAD_DIM, (h + 1) * HEAD_DIM) for h in range(RWKV_HEADS)]

    def prepare(bi):
        def shifted(z_ref, idx):
            z = z_ref[bi]
            prev = jnp.where(first_row, prev_ref[bi, idx, 0:1, :], pltpu.roll(z, 1, 0))
            prev_ref[bi, idx, 0:1, :] = z[c - 1:c, :]
            return z + (prev - z) * mu_ref[idx:idx + 1, :]

        r = shifted(zr_ref, 0)
        k = shifted(zk_ref, 1)
        v = shifted(zv_ref, 2)
        g = shifted(zg_ref, 3)
        zwa = zwa_ref[bi]
        prev_wa = jnp.where(first_row_wa, prevwa_ref[bi, 0:1, :], pltpu.roll(zwa, 1, 0))
        prevwa_ref[bi, 0:1, :] = zwa[c - 1:c, :]
        wa = zwa + (prev_wa - zwa) * muwa_ref[...]
        wd = wa[:, :LORA_RANK]
        ad = wa[:, LORA_RANK:]

        lw = -DECAY_SCALE * _sigmoid(w0_ref[...] + _dot_x3(jnp.tanh(wd), wup_ref[...]))
        a = _sigmoid(a0_ref[...] + _dot_x3(ad, aup_ref[...]))
        kk = k * kk_ref[...]
        norm = jnp.sqrt(_head_sum(kk * kk, head_ones))
        kk = kk / jnp.maximum(norm, 1e-12)
        kmod = k * (1.0 + (a - 1.0) * ka_ref[...])
        b = a * kk

        lw_hi, lw_lo = _split(lw)
        cl = _dot(tri, lw_hi) + _dot(tri, lw_lo)
        c_mid = cl[c // 2 - 1:c // 2, :]
        c_end = cl[c - 1:c, :]
        w = jnp.exp(lw)
        e_in1 = jnp.exp(cl - lw)
        e_in2 = e_in1 * w
        e_k = jnp.exp(c_mid - cl)
        e_mid_inv = jnp.exp(-c_mid)
        e_end = e_k * jnp.exp(c_end - c_mid)
        ops = dict(
            q1=kk * e_in1 * e_mid_inv, q2=r * e_in2 * e_mid_inv, k1=b * e_k, k2=kmod * e_k,
            kkd=kk * e_in1, rd=r * e_in2, bd=b * e_end, kd=kmod * e_end, vb=v)
        ops = {name: [x.astype(BF16)[:, s] for s in hs] for name, x in ops.items()}
        return dict(ops, r=r, kmod=kmod, v=v, g=g, d_end=jnp.exp(c_end))

    prep = [prepare(bi) for bi in range(nb)]

    probs = [(bi, h) for bi in range(nb) for h in range(RWKV_HEADS)]
    op = lambda name: [prep[bi][name][h] for bi, h in probs]
    q1, q2, k1, k2, kkd, rd, bd, kd, vh = map(op, ("q1", "q2", "k1", "k2", "kkd", "rd", "bd", "kd", "vb"))
    n = range(len(probs))
    a_ab = [jnp.where(strict, _dot_nt(q1[i], k1[i]), 0.0) for i in n]
    t_inv = _unit_lower_inverse(a_ab, row, col)
    a_ak = [jnp.where(strict, _dot_nt(q1[i], k2[i]), 0.0).astype(BF16) for i in n]
    a_rb = [jnp.where(incl, _dot_nt(q2[i], k1[i]), 0.0).astype(BF16) for i in n]
    a_rk = [jnp.where(incl, _dot_nt(q2[i], k2[i]), 0.0).astype(BF16) for i in n]
    s_old = [state_ref[bi, h] for bi, h in probs]
    s_in = [s.astype(BF16) for s in s_old]
    rhs = [_dot_nt(kkd[i], s_in[i]) + _dot(a_ak[i], vh[i]) for i in n]
    u = [-_dot(t_inv[i].astype(BF16), rhs[i].astype(BF16)) for i in n]
    uv_t = [jnp.concatenate([u[i], prep[bi]["v"][:, hs[h]]], axis=1).T.astype(BF16)
            for i, (bi, h) in enumerate(probs)]
    for i, (bi, h) in enumerate(probs):
        state_ref[bi, h] = (s_old[i] * prep[bi]["d_end"][:, hs[h]]
                            + _dot(uv_t[i][:HEAD_DIM, :], bd[i]) + _dot(uv_t[i][HEAD_DIM:, :], kd[i]))
    for i, (bi, h) in enumerate(probs):
        y_ref[bi, :, hs[h]] = (_dot_nt(rd[i], s_in[i]) + _dot(a_rb[i], u[i].astype(BF16))
                               + _dot(a_rk[i], vh[i]))

    inv_n = 1.0 / HEAD_DIM
    for bi in range(nb):
        pb = prep[bi]
        y = y_ref[bi]
        mean = _head_sum(y, head_ones) * inv_n
        dlt = y - mean
        var = _head_sum(dlt * dlt, head_ones) * inv_n
        yn = dlt * lax.rsqrt(var + GN_EPS) * lng_ref[...] + lnb_ref[...]
        bonus = _head_sum(pb["r"] * pb["kmod"] * rk_ref[...], head_ones) * pb["v"]
        g = pb["g"]
        o_ref[bi] = ((yn + bonus) * (g * _sigmoid(g))).astype(o_ref.dtype)


def _rwkv(z3d, mu4, mu_wa, w0, w_up, a0, a_up, k_k, k_a, r_k, ln_g, ln_b):
    batch, seq, _ = z3d.shape
    c = RWKV_CHUNK
    nb = RWKV_BATCH_ROWS if batch % RWKV_BATCH_ROWS == 0 else 1
    zspec = lambda u: pl.BlockSpec((nb, c, RWKV_WIDTH), lambda b, i, u=u: (b, i, u))
    full = lambda a: pl.BlockSpec(a.shape, lambda b, i: (0,) * a.ndim)
    params = (mu4, mu_wa, w0, w_up, a0, a_up, k_k, k_a, r_k, ln_g, ln_b)
    return pl.pallas_call(
        _rwkv_kernel,
        grid=(batch // nb, seq // c),
        in_specs=[zspec(0), zspec(1), zspec(2), zspec(3),
                  pl.BlockSpec((nb, c, LANES), lambda b, i: (b, i, U_RW_WA))]
                 + [full(a) for a in params],
        out_specs=pl.BlockSpec((nb, c, RWKV_WIDTH), lambda b, i: (b, i, 0)),
        out_shape=jax.ShapeDtypeStruct((batch, seq, RWKV_WIDTH), BF16),
        scratch_shapes=[
            pltpu.VMEM((nb, RWKV_HEADS, HEAD_DIM, HEAD_DIM), F32),
            pltpu.VMEM((nb, 4, 8, RWKV_WIDTH), F32),
            pltpu.VMEM((nb, 8, LANES), F32),
            pltpu.VMEM((nb, c, RWKV_WIDTH), F32),
        ],
        compiler_params=pltpu.CompilerParams(
            dimension_semantics=("parallel", "arbitrary"), vmem_limit_bytes=VMEM_LIMIT),
        name="rwkv7",
    )(z3d, z3d, z3d, z3d, z3d, *params)


def _dsa_kernel(zq_ref, zg_ref, ziq_ref, zkv_ref, zms_ref, c64_ref, s64_ref, c32_ref, s32_ref, o_ref,
                k_ref, vt_ref, ik_ref, key_ref):
    seq = zkv_ref.shape[0]
    blk = ATT_BLOCK
    kb_rows = DSA_KEY_BLOCK
    qb = pl.program_id(1)
    n_kb = (qb * blk) // kb_rows + 1

    @pl.when(qb == 0)
    def _():
        lane = lax.broadcasted_iota(I32, (kb_rows, LANES), 1)

        def body(t, carry):
            rows = pl.ds(pl.multiple_of(t * kb_rows, kb_rows), kb_rows)
            kv = zkv_ref[rows, :]
            kvr = jnp.where(lane < HEAD_DIM, _rope(kv, c64_ref[rows, :], s64_ref[rows, :], 8, HEAD_DIM), kv)
            k_ref[rows, :] = kvr[:, :HEAD_DIM].astype(BF16)
            vt_ref[t] = kvr.T[HEAD_DIM:, :].astype(BF16)
            ms = zms_ref[rows, :]
            ikr = _rope(ms, c32_ref[rows, :], s32_ref[rows, :], 4, IDX_DIM)
            ik_ref[rows, :] = ikr[:, :IDX_DIM].astype(BF16)
            return carry

        lax.fori_loop(0, seq // kb_rows, body, 0)

    r0 = pl.multiple_of(qb * blk, blk)
    qrows = pl.ds(r0, blk)
    c64 = c64_ref[qrows, :]
    s64 = s64_ref[qrows, :]
    c32 = c32_ref[qrows, :]
    s32 = s32_ref[qrows, :]
    q2 = [_rope(zq_ref[:, i * LANES:(i + 1) * LANES], c64, s64, 8, HEAD_DIM) * QK_SCALE_LOG2
          for i in range(2)]
    q_stack = jnp.concatenate(
        [q2[h // 2][:, (h % 2) * HEAD_DIM:(h % 2 + 1) * HEAD_DIM] for h in range(DSA_HEADS)],
        axis=0).astype(BF16)
    iq2 = [_rope(ziq_ref[:, i * LANES:(i + 1) * LANES], c32, s32, 4, IDX_DIM) for i in range(2)]
    iq_stack = jnp.concatenate(
        [iq2[h // 4][:, (h % 4) * IDX_DIM:(h % 4 + 1) * IDX_DIM] for h in range(IDX_HEADS)],
        axis=0).astype(BF16)
    iw_t = zms_ref[qrows, :].T[IDX_DIM:IDX_DIM + IDX_HEADS, :] * ((IDX_HEADS * IDX_DIM) ** -0.5)
    iw_row = jnp.concatenate([iw_t[h:h + 1, :] for h in range(IDX_HEADS)], axis=1)

    kpos_l = lax.broadcasted_iota(I32, (kb_rows, blk), 0)
    qpos = r0 + lax.broadcasted_iota(I32, (kb_rows, blk), 1)

    tri = jnp.where(lax.broadcasted_iota(I32, (kb_rows, kb_rows), 1)
                    <= lax.broadcasted_iota(I32, (kb_rows, kb_rows), 0), 1.0, 0.0).astype(BF16)
    g = zg_ref[...]
    gate = g * _sigmoid(g)

    def step(n):
        rows = [slice(kb * kb_rows, (kb + 1) * kb_rows) for kb in range(n)]
        causal = (n - 1) * kb_rows + kpos_l <= qpos

        for kb in range(n):
            s = jnp.maximum(_dot_nt(ik_ref[rows[kb], :], iq_stack), 0.0) * iw_row
            sc = s[:, 0:blk]
            for h in range(1, IDX_HEADS):
                sc = sc + s[:, h * blk:(h + 1) * blk]
            if kb == n - 1:
                sc = jnp.where(causal, sc, -jnp.inf)
            bits = pltpu.bitcast(sc, I32)
            key_ref[rows[kb], :] = bits ^ ((bits >> 31) & 0x7FFFFFFF)

        def count(pred_fn):
            acc = jnp.zeros((8, blk), I32)
            for kb in range(n):
                hit = jnp.where(pred_fn(key_ref[rows[kb], :]), 1, 0).astype(I32)
                acc = acc + hit.reshape(kb_rows // 8, 8, blk).sum(axis=0)
            return acc.sum(axis=0, keepdims=True)

        def bit_body(i, lo):
            cand = lo + (jnp.int32(1) << (31 - i))
            return jnp.where(count(lambda keys: keys >= cand) >= DSA_TOPK, cand, lo)

        lo0 = jnp.full((1, blk), INT_MIN, I32)
        search = lambda: lax.fori_loop(0, 32, bit_body, lo0)
        if n * kb_rows - blk + 1 > DSA_TOPK:
            thr = search()
        else:
            thr = lax.cond(qb * blk >= DSA_TOPK, search, lambda: lo0)
        need = (DSA_TOPK - count(lambda keys: keys > thr)).astype(F32)

        wide = DSA_HEADS * blk
        m = jnp.full((1, wide), NEG_BIG, F32)
        l = jnp.zeros((1, wide), F32)
        acc = jnp.zeros((HEAD_DIM, wide), F32)
        run = jnp.zeros((1, blk), F32)
        for kb in range(n):
            keys = key_ref[rows[kb], :]
            eq = keys == thr
            rank = _dot(tri, jnp.where(eq, 1.0, 0.0).astype(BF16)) + run
            run = rank[kb_rows - 1:kb_rows, :]
            take = (keys > thr) | (eq & (rank <= need))
            if kb == n - 1:
                take = take & causal
            bias = jnp.where(take, 0.0, NEG_BIG)
            s = _dot_nt(k_ref[rows[kb], :], q_stack)
            s = s + jnp.concatenate([bias] * DSA_HEADS, axis=1)
            m_new = jnp.maximum(m, jnp.max(s, axis=0, keepdims=True))
            p = jnp.exp2(s - m_new)
            alpha = jnp.exp2(m - m_new)
            l = alpha * l + jnp.sum(p, axis=0, keepdims=True)
            acc = alpha * acc + _dot(vt_ref[kb], p.astype(BF16))
            m = m_new

        o_t = acc / l
        o = jnp.concatenate(
            [jnp.concatenate([o_t[:, (2 * i) * blk:(2 * i + 1) * blk], o_t[:, (2 * i + 1) * blk:(2 * i + 2) * blk]],
                             axis=0).T for i in range(DSA_HEADS // 2)], axis=1)
        o_ref[...] = (o * gate).astype(o_ref.dtype)

    for n in range(1, seq // kb_rows + 1):
        pl.when(n_kb == n)(functools.partial(step, n))


def _dsa(z2d, batch, seq, c64, s64, c32, s32):
    blk = ATT_BLOCK
    nq = seq // blk
    qspec = lambda u2: pl.BlockSpec((blk, 2 * LANES), lambda b, i, u2=u2: (b * nq + i, u2))
    seqspec = lambda u: pl.BlockSpec((seq, LANES), lambda b, i, u=u: (b, u))
    tab = pl.BlockSpec((seq, LANES), lambda b, i: (0, 0))
    return pl.pallas_call(
        _dsa_kernel,
        grid=(batch, nq),
        in_specs=[qspec(U_DS_Q // 2), qspec(U_DS_G // 2), qspec(U_DS_IQ // 2),
                  seqspec(U_DS_KV), seqspec(U_DS_MISC), tab, tab, tab, tab],
        out_specs=pl.BlockSpec((blk, DSA_WIDTH), lambda b, i: (b * nq + i, 0)),
        out_shape=jax.ShapeDtypeStruct((batch * seq, DSA_WIDTH), BF16),
        scratch_shapes=[
            pltpu.VMEM((seq, HEAD_DIM), BF16),
            pltpu.VMEM((seq // DSA_KEY_BLOCK, HEAD_DIM, DSA_KEY_BLOCK), BF16),
            pltpu.VMEM((seq, IDX_DIM), BF16),
            pltpu.VMEM((seq, blk), I32),
        ],
        compiler_params=pltpu.CompilerParams(
            dimension_semantics=("parallel", "arbitrary"), vmem_limit_bytes=VMEM_LIMIT),
        name="dsa",
    )(z2d, z2d, z2d, z2d, z2d, c64, s64, c32, s32)


def _dil_group(gi, rate, q_ref, k_ref, v_ref, g_ref, c64_ref, s64_ref, qr_ref, kr_ref, o_s, lse_s):
    seq = q_ref.shape[0]
    blk = ATT_BLOCK
    nrow = seq // blk

    def rope_body(t, carry):
        rows = pl.ds(pl.multiple_of(t * blk, blk), blk)
        ct = c64_ref[rows, :]
        st = s64_ref[rows, :]
        qr_ref[rows, :] = _rope(q_ref[rows, :], ct, st, 8, HEAD_DIM) * QK_SCALE_LOG2
        kr_ref[rows, :] = _rope(k_ref[rows, :], ct, st, 8, HEAD_DIM)
        return carry

    lax.fori_loop(0, nrow, rope_body, 0)

    nblk = seq // (blk * rate)
    qi = lax.broadcasted_iota(I32, (blk, 2 * blk), 0)
    kj = lax.broadcasted_iota(I32, (blk, 2 * blk), 1)
    band = (kj >= qi) & (kj <= qi + DIL_STEPS)

    def strided(ref, start):
        if rate == 1:
            return ref[pl.ds(start, blk), :]
        return ref[pl.ds(start, blk, stride=rate), :]

    def store(ref, start, val):
        if rate == 1:
            ref[gi, pl.ds(start, blk), :] = val
        else:
            ref[gi, pl.ds(start, blk, stride=rate), :] = val

    first_q = lax.broadcasted_iota(I32, (blk, LANES), 1) < HEAD_DIM
    first_kv = lax.broadcasted_iota(I32, (2 * blk, LANES), 1) < HEAD_DIM
    unroll = DIL_UNROLL

    def body(it, carry):
        starts, qs, ks, vs, masks = [], [], [], [], []
        for u in range(unroll):
            cidx = it * unroll + u
            rho = cidx // nblk
            j = cidx % nblk
            start = j * (blk * rate) + rho
            pstart = jnp.maximum(j - 1, 0) * (blk * rate) + rho
            starts.append(start)
            qs.append(strided(qr_ref, start))
            ks.append(jnp.concatenate([strided(kr_ref, pstart), strided(kr_ref, start)], axis=0).astype(BF16))
            vs.append(jnp.concatenate([strided(v_ref, pstart), strided(v_ref, start)], axis=0))
            masks.append(band & ((kj >= blk) | (j > 0)))
        prob = [(u, h) for u in range(unroll) for h in range(2)]
        qh = [jnp.where(first_q == (h == 0), qs[u], 0.0).astype(BF16) for u, h in prob]
        vh = [jnp.where(first_kv == (h == 0), vs[u], 1.0).astype(BF16) for u, h in prob]
        s = [jnp.where(masks[u], _dot_nt(qh[i], ks[u]), -jnp.inf) for i, (u, h) in enumerate(prob)]
        m = [jnp.max(x, axis=-1, keepdims=True) for x in s]
        r = [_dot(jnp.exp2(s[i] - m[i]).astype(BF16), vh[i]) for i in range(len(prob))]
        for u in range(unroll):
            r0, r1 = r[2 * u], r[2 * u + 1]
            o = jnp.where(first_q, r0, r1)
            l = pltpu.roll(jnp.where(first_q, r1, r0), HEAD_DIM, 1)
            lse = jnp.where(first_q, m[2 * u], m[2 * u + 1]) * math.log(2.0) + jnp.log(l)
            gate = strided(g_ref, starts[u])
            store(o_s, starts[u], o / l * (gate * _sigmoid(gate)))
            store(lse_s, starts[u], lse)
        return carry

    lax.fori_loop(0, rate * nblk // unroll, body, 0)


def _dil_kernel(q_ref, k_ref, v_ref, g_ref, c64_ref, s64_ref, o_ref, qr_ref, kr_ref, o_s, lse_s):
    grp = pl.program_id(1)
    for gi, rate in enumerate(DIL_RATES):
        pl.when(grp == gi)(functools.partial(
            _dil_group, gi, rate, q_ref, k_ref, v_ref, g_ref, c64_ref, s64_ref, qr_ref, kr_ref, o_s, lse_s))

    @pl.when(grp == len(DIL_RATES) - 1)
    def _():
        blk = ATT_BLOCK

        def body(t, carry):
            rows = pl.ds(pl.multiple_of(t * blk, blk), blk)
            ls = [lse_s[i, rows, :] for i in range(3)]
            top = jnp.maximum(jnp.maximum(ls[0], ls[1]), ls[2])
            es = [jnp.exp(x - top) for x in ls]
            inv = 1.0 / (es[0] + es[1] + es[2])
            o_ref[rows, :] = jnp.concatenate(
                [o_s[i, rows, :] * (es[i] * inv) for i in range(3)], axis=1).astype(o_ref.dtype)
            return carry

        lax.fori_loop(0, q_ref.shape[0] // blk, body, 0)


def _dilated(z2d, batch, seq, c64, s64):
    ng = len(DIL_RATES)
    zspec = lambda base: pl.BlockSpec((seq, LANES), lambda b, g, base=base: (b, base + g))
    tab = pl.BlockSpec((seq, LANES), lambda b, g: (0, 0))
    return pl.pallas_call(
        _dil_kernel,
        grid=(batch, ng),
        in_specs=[zspec(U_DL), zspec(U_DL + 3), zspec(U_DL + 6), zspec(U_DL + 9), tab, tab],
        out_specs=pl.BlockSpec((seq, DIL_WIDTH), lambda b, g: (b, 0)),
        out_shape=jax.ShapeDtypeStruct((batch * seq, DIL_WIDTH), BF16),
        scratch_shapes=[
            pltpu.VMEM((seq, LANES), F32),
            pltpu.VMEM((seq, LANES), F32),
            pltpu.VMEM((ng, seq, LANES), F32),
            pltpu.VMEM((ng, seq, LANES), F32),
        ],
        compiler_params=pltpu.CompilerParams(
            dimension_semantics=("parallel", "arbitrary"), vmem_limit_bytes=VMEM_LIMIT),
        name="dilated",
    )(z2d, z2d, z2d, z2d, c64, s64)


def _rope_tables(seq, head, half):
    inv = ROPE_THETA ** (-jnp.arange(half, dtype=F32) / half)
    ang = jnp.arange(seq, dtype=F32)[:, None] * inv[None, :]
    cos, sin = jnp.cos(ang), jnp.sin(ang)
    lane = jnp.arange(LANES) % head
    idx = lane % half
    cos_t = jnp.where(lane[None, :] < 2 * half, cos[:, idx], 1.0)
    sin_t = jnp.where(lane[None, :] < half, -sin[:, idx], jnp.where(lane[None, :] < 2 * half, sin[:, idx], 0.0))
    return cos_t, sin_t


def _permute_in_cols(w):
    rw = RWKV_WIDTH
    o = 0
    r, k, v, g = (w[..., o + i * rw:o + (i + 1) * rw] for i in range(4))
    o = 4 * rw
    wd, ad = w[..., o:o + LORA_RANK], w[..., o + LORA_RANK:o + 2 * LORA_RANK]
    o += 2 * LORA_RANK
    widths = (DSA_WIDTH, HEAD_DIM, HEAD_DIM, IDX_HEADS * IDX_DIM, IDX_DIM, IDX_HEADS, DSA_WIDTH)
    parts = []
    for wdt in widths:
        parts.append(w[..., o:o + wdt])
        o += wdt
    dq, dk, dv, diq, dik, diw, dg = parts
    cq, ck, cv, cg = (w[..., o + i * DIL_WIDTH:o + (i + 1) * DIL_WIDTH] for i in range(4))
    pad = jnp.zeros(w.shape[:-1] + (LANES - IDX_DIM - IDX_HEADS,), w.dtype)
    return jnp.concatenate([r, k, v, g, cq, ck, cv, cg, dq, dg, diq, dk, dv, dik, diw, pad, wd, ad], axis=-1)


def kernel(x, p, norm_g, w_in, tshift_mu, rwkv_w0, rwkv_w_up, rwkv_a0, rwkv_a_up, rwkv_k_k, rwkv_k_a, rwkv_r_k,
           rwkv_ln_g, rwkv_ln_b, w_out, ple_norm_g, ple_w_gate, ple_w_proj, final_norm_g):
    batch, seq, d_model = x.shape
    depth = w_in.shape[0]
    n = batch * seq
    assert seq % (ATT_BLOCK * max(DIL_RATES)) == 0 and seq >= 4 * DSA_TOPK

    c64, s64 = _rope_tables(seq, HEAD_DIM, HEAD_DIM // 8)
    c32, s32 = _rope_tables(seq, IDX_DIM, IDX_DIM // 8)
    w_in_k = _permute_in_cols(w_in).astype(BF16)
    rw = RWKV_WIDTH
    row = lambda a: a.reshape(1, -1)

    x2d = x.reshape(n, d_model)
    for i in range(depth):
        z = _in_proj(x2d, row(norm_g[i]), w_in_k[i])
        mu = tshift_mu[i]
        y_a = _rwkv(z.reshape(batch, seq, Z_COLS), mu[:4 * rw].reshape(4, rw), row(mu[4 * rw:]), row(rwkv_w0[i]), rwkv_w_up[i],
                    row(rwkv_a0[i]), rwkv_a_up[i], row(rwkv_k_k[i]), row(rwkv_k_a[i]), row(rwkv_r_k[i]),
                    row(rwkv_ln_g[i]), row(rwkv_ln_b[i])).reshape(n, rw)
        y_b = _dsa(z, batch, seq, c64, s64, c32, s32)
        y_c = _dilated(z, batch, seq, c64, s64)
        wo = w_out[i].astype(BF16)
        x2d = _out_proj(x2d, y_a, y_b, y_c, p[i].reshape(n, -1), wo[:rw], wo[rw:rw + DSA_WIDTH],
                        wo[rw + DSA_WIDTH:], row(ple_norm_g[i]), ple_w_gate[i].astype(BF16),
                        ple_w_proj[i].astype(BF16), row(final_norm_g), final=(i == depth - 1))
    return x2d.reshape(batch, seq, d_model)
```

```python
import functools
import math

import jax
import jax.numpy as jnp
from jax import lax
from jax.experimental import pallas as pl
from jax.experimental.pallas import tpu as pltpu

F32 = jnp.float32
BF16 = jnp.bfloat16
I32 = jnp.int32

HEAD_DIM = 64
ROPE_THETA = 500000.0
NORM_EPS = 1e-6
RWKV_HEADS = 6
RWKV_WIDTH = RWKV_HEADS * HEAD_DIM
LORA_RANK = 64
GN_EPS = 64e-5
DECAY_SCALE = math.exp(-0.5)
DSA_HEADS = 4
DSA_WIDTH = DSA_HEADS * HEAD_DIM
IDX_HEADS = 8
IDX_DIM = 32
DSA_TOPK = 256
DIL_RATES = (1, 4, 16)
DIL_STEPS = 128
DIL_UNROLL = 4
DIL_WIDTH = 384
ATT_BLOCK = 128
DSA_KEY_BLOCK = 512

LANES = 128
MXU_WIDTH = 256
VMEM_LIMIT = 48 * 1024 * 1024

U_RW = 0
U_DL = 12
U_DS_Q = 24
U_DS_G = 26
U_DS_IQ = 28
U_DS_KV = 30
U_DS_MISC = 31
U_RW_WA = 32
Z_UNITS = 33
Z_COLS = Z_UNITS * LANES

RWKV_CHUNK = 128
RWKV_BATCH_ROWS = 4
INV_BASE = 16
QK_SCALE_LOG2 = HEAD_DIM ** -0.5 * math.log2(math.e)
NEG_BIG = -1e30
INT_MIN = -2 ** 31


def _dot(a, b):
    return jnp.dot(a, b, preferred_element_type=F32)


def _dot_nt(a, b):
    return lax.dot_general(a, b, (((1,), (1,)), ((), ())), preferred_element_type=F32)


def _split(x):
    hi = x.astype(BF16)
    lo = (x - hi.astype(F32)).astype(BF16)
    return hi, lo


def _head_sum(x, head_ones):
    return _dot(x.astype(BF16), head_ones)


def _dot_x3(x, w):
    xh, xl = _split(x)
    wh, wl = _split(w)
    return _dot(xh, wh) + (_dot(xh, wl) + _dot(xl, wh))


def _sigmoid(x):
    return 1.0 / (1.0 + jnp.exp(-x))


def _rope(x, cos_t, sin_t, half, head):
    lane = lax.broadcasted_iota(I32, x.shape, 1) % head
    partner = jnp.where(lane < half, pltpu.roll(x, LANES - half, 1), pltpu.roll(x, half, 1))
    return x * cos_t + partner * sin_t


def _in_proj_kernel(x_ref, g_ref, w_ref, z_ref, *, tn):
    x = x_ref[...]
    ms = jnp.mean(x * x, axis=-1, keepdims=True)
    h = (x * lax.rsqrt(ms + NORM_EPS) * g_ref[...]).astype(BF16)
    for c in range(0, Z_COLS, tn):
        wdt = min(tn, Z_COLS - c)
        z_ref[:, c:c + wdt] = _dot(h, w_ref[:, c:c + wdt])


def _in_proj(x2d, g, w_all, layer, tm=512, tn=MXU_WIDTH):
    n, d = x2d.shape
    return pl.pallas_call(
        functools.partial(_in_proj_kernel, tn=tn),
        grid=(n // tm,),
        in_specs=[
            pl.BlockSpec((tm, d), lambda i: (i, 0)),
            pl.BlockSpec((1, d), lambda i: (0, 0)),
            pl.BlockSpec((None, d, Z_COLS), lambda i: (layer, 0, 0)),
        ],
        out_specs=pl.BlockSpec((tm, Z_COLS), lambda i: (i, 0)),
        out_shape=jax.ShapeDtypeStruct((n, Z_COLS), F32),
        compiler_params=pltpu.CompilerParams(
            dimension_semantics=("parallel",), vmem_limit_bytes=VMEM_LIMIT),
        name="in_proj",
    )(x2d, g, w_all)


def _out_proj_kernel(x_ref, ya_ref, yb_ref, yc_ref, p_ref, wa_ref, wb_ref, wc_ref, ng_ref, wg_ref,
                     wp_ref, fg_ref, o_ref, *, final):
    acc = x_ref[...] + _dot(ya_ref[...], wa_ref[...])
    acc = acc + _dot(yb_ref[...], wb_ref[...])
    acc = acc + _dot(yc_ref[...], wc_ref[...])
    ms = jnp.mean(acc * acc, axis=-1, keepdims=True)
    hn = (acc * lax.rsqrt(ms + NORM_EPS) * ng_ref[...]).astype(BF16)
    gate = _sigmoid(_dot(hn, wg_ref[...]))
    out = acc + gate * _dot(p_ref[...].astype(BF16), wp_ref[...])
    if final:
        ms = jnp.mean(out * out, axis=-1, keepdims=True)
        out = out * lax.rsqrt(ms + NORM_EPS) * fg_ref[...]
    o_ref[...] = out


def _out_proj(x2d, ya, yb, yc, p3d, layer, wa, wb, wc, ng, wg, wp, fg, final, tm=512):
    n, d = x2d.shape
    row = lambda w: pl.BlockSpec((tm, w), lambda i: (i, 0))
    full = lambda a: pl.BlockSpec(a.shape, lambda i: (0, 0))
    return pl.pallas_call(
        functools.partial(_out_proj_kernel, final=final),
        grid=(n // tm,),
        in_specs=[row(d), row(ya.shape[1]), row(yb.shape[1]), row(yc.shape[1]),
                  pl.BlockSpec((None, tm, p3d.shape[2]), lambda i: (layer, i, 0)),
                  full(wa), full(wb), full(wc), full(ng), full(wg), full(wp), full(fg)],
        out_specs=row(d),
        out_shape=jax.ShapeDtypeStruct((n, d), F32),
        compiler_params=pltpu.CompilerParams(
            dimension_semantics=("parallel",), vmem_limit_bytes=VMEM_LIMIT),
        name="out_proj",
    )(x2d, ya, yb, yc, p3d, wa, wb, wc, ng, wg, wp, fg)


def _unit_lower_inverse(n_list, row, col):
    c = n_list[0].shape[0]
    eye = jnp.where(row == col, 1.0, 0.0).astype(F32)
    same = (row // INV_BASE) == (col // INV_BASE)
    pw = [jnp.where(same, n, 0.0) for n in n_list]
    t = [eye - x for x in pw]
    span = 2
    while span < INV_BASE:
        pwb = [x.astype(BF16) for x in pw]
        pw = [_dot(x, x) for x in pwb]
        t = [x + _dot(x.astype(BF16), y.astype(BF16)) for x, y in zip(t, pw)]
        span *= 2
    size = INV_BASE
    while size < c:
        lower_left = ((row // (2 * size)) == (col // (2 * size))) & ((row // size) != (col // size))
        tb = [x.astype(BF16) for x in t]
        et = [_dot(jnp.where(lower_left, n, 0.0).astype(BF16), x).astype(BF16) for n, x in zip(n_list, tb)]
        t = [x - _dot(xb, y) for x, xb, y in zip(t, tb, et)]
        size *= 2
    return t


def _rwkv_kernel(zr_ref, zk_ref, zv_ref, zg_ref, zwa_ref, mu_ref, muwa_ref, w0_ref, wup_ref, a0_ref,
                 aup_ref, kk_ref, ka_ref, rk_ref, lng_ref, lnb_ref, o_ref,
                 state_ref, prev_ref, prevwa_ref, y_ref):
    nb, c = zr_ref.shape[0], zr_ref.shape[1]
    width = RWKV_WIDTH

    @pl.when(pl.program_id(1) == 0)
    def _():
        state_ref[...] = jnp.zeros_like(state_ref)
        prev_ref[...] = jnp.zeros_like(prev_ref)
        prevwa_ref[...] = jnp.zeros_like(prevwa_ref)

    first_row = lax.broadcasted_iota(I32, (c, width), 0) == 0
    first_row_wa = lax.broadcasted_iota(I32, (c, LANES), 0) == 0
    hrow = lax.broadcasted_iota(I32, (width, width), 0) // HEAD_DIM
    hcol = lax.broadcasted_iota(I32, (width, width), 1) // HEAD_DIM
    head_ones = jnp.where(hrow == hcol, 1.0, 0.0).astype(BF16)
    row = lax.broadcasted_iota(I32, (c, c), 0)
    col = lax.broadcasted_iota(I32, (c, c), 1)
    tri = jnp.where(col <= row, 1.0, 0.0).astype(BF16)
    strict = col < row
    incl = col <= row
    hs = [slice(h * HEAD_DIM, (h + 1) * HEAD_DIM) for h in range(RWKV_HEADS)]

    def prepare(bi):
        def shifted(z_ref, idx):
            z = z_ref[bi]
            prev = jnp.where(first_row, prev_ref[bi, idx, 0:1, :], pltpu.roll(z, 1, 0))
            prev_ref[bi, idx, 0:1, :] = z[c - 1:c, :]
            return z + (prev - z) * mu_ref[idx:idx + 1, :]

        r = shifted(zr_ref, 0)
        k = shifted(zk_ref, 1)
        v = shifted(zv_ref, 2)
        g = shifted(zg_ref, 3)
        zwa = zwa_ref[bi]
        prev_wa = jnp.where(first_row_wa, prevwa_ref[bi, 0:1, :], pltpu.roll(zwa, 1, 0))
        prevwa_ref[bi, 0:1, :] = zwa[c - 1:c, :]
        wa = zwa + (prev_wa - zwa) * muwa_ref[...]
        wd = wa[:, :LORA_RANK]
        ad = wa[:, LORA_RANK:]

        lw = -DECAY_SCALE * _sigmoid(w0_ref[...] + _dot_x3(jnp.tanh(wd), wup_ref[...]))
        a = _sigmoid(a0_ref[...] + _dot_x3(ad, aup_ref[...]))
        kk = k * kk_ref[...]
        norm = jnp.sqrt(_head_sum(kk * kk, head_ones))
        kk = kk / jnp.maximum(norm, 1e-12)
        kmod = k * (1.0 + (a - 1.0) * ka_ref[...])
        b = a * kk

        lw_hi, lw_lo = _split(lw)
        cl = _dot(tri, lw_hi) + _dot(tri, lw_lo)
        c_mid = cl[c // 2 - 1:c // 2, :]
        c_end = cl[c - 1:c, :]
        w = jnp.exp(lw)
        e_in1 = jnp.exp(cl - lw)
        e_in2 = e_in1 * w
        e_k = jnp.exp(c_mid - cl)
        e_mid_inv = jnp.exp(-c_mid)
        e_end = e_k * jnp.exp(c_end - c_mid)
        ops = dict(
            q1=kk * e_in1 * e_mid_inv, q2=r * e_in2 * e_mid_inv, k1=b * e_k, k2=kmod * e_k,
            kkd=kk * e_in1, rd=r * e_in2, bd=b * e_end, kd=kmod * e_end, vb=v)
        ops = {name: [x.astype(BF16)[:, s] for s in hs] for name, x in ops.items()}
        return dict(ops, r=r, kmod=kmod, v=v, g=g, d_end=jnp.exp(c_end))

    prep = [prepare(bi) for bi in range(nb)]

    probs = [(bi, h) for bi in range(nb) for h in range(RWKV_HEADS)]
    op = lambda name: [prep[bi][name][h] for bi, h in probs]
    q1, q2, k1, k2, kkd, rd, bd, kd, vh = map(op, ("q1", "q2", "k1", "k2", "kkd", "rd", "bd", "kd", "vb"))
    n = range(len(probs))
    a_ab = [jnp.where(strict, _dot_nt(q1[i], k1[i]), 0.0) for i in n]
    t_inv = _unit_lower_inverse(a_ab, row, col)
    a_ak = [jnp.where(strict, _dot_nt(q1[i], k2[i]), 0.0).astype(BF16) for i in n]
    a_rb = [jnp.where(incl, _dot_nt(q2[i], k1[i]), 0.0).astype(BF16) for i in n]
    a_rk = [jnp.where(incl, _dot_nt(q2[i], k2[i]), 0.0).astype(BF16) for i in n]
    s_old = [state_ref[bi, h] for bi, h in probs]
    s_in = [s.astype(BF16) for s in s_old]
    rhs = [_dot_nt(kkd[i], s_in[i]) + _dot(a_ak[i], vh[i]) for i in n]
    u = [-_dot(t_inv[i].astype(BF16), rhs[i].astype(BF16)) for i in n]
    uv_t = [jnp.concatenate([u[i], prep[bi]["v"][:, hs[h]]], axis=1).T.astype(BF16)
            for i, (bi, h) in enumerate(probs)]
    for i, (bi, h) in enumerate(probs):
        state_ref[bi, h] = (s_old[i] * prep[bi]["d_end"][:, hs[h]]
                            + _dot(uv_t[i][:HEAD_DIM, :], bd[i]) + _dot(uv_t[i][HEAD_DIM:, :], kd[i]))
    for i, (bi, h) in enumerate(probs):
        y_ref[bi, :, hs[h]] = (_dot_nt(rd[i], s_in[i]) + _dot(a_rb[i], u[i].astype(BF16))
                               + _dot(a_rk[i], vh[i]))

    inv_n = 1.0 / HEAD_DIM
    for bi in range(nb):
        pb = prep[bi]
        y = y_ref[bi]
        mean = _head_sum(y, head_ones) * inv_n
        dlt = y - mean
        var = _head_sum(dlt * dlt, head_ones) * inv_n
        yn = dlt * lax.rsqrt(var + GN_EPS) * lng_ref[...] + lnb_ref[...]
        bonus = _head_sum(pb["r"] * pb["kmod"] * rk_ref[...], head_ones) * pb["v"]
        g = pb["g"]
        o_ref[bi] = ((yn + bonus) * (g * _sigmoid(g))).astype(o_ref.dtype)


def _rwkv(z3d, mu4, mu_wa, w0, w_up, a0, a_up, k_k, k_a, r_k, ln_g, ln_b):
    batch, seq, _ = z3d.shape
    c = RWKV_CHUNK
    nb = RWKV_BATCH_ROWS if batch % RWKV_BATCH_ROWS == 0 else 1
    zspec = lambda u: pl.BlockSpec((nb, c, RWKV_WIDTH), lambda b, i, u=u: (b, i, u))
    full = lambda a: pl.BlockSpec(a.shape, lambda b, i: (0,) * a.ndim)
    params = (mu4, mu_wa, w0, w_up, a0, a_up, k_k, k_a, r_k, ln_g, ln_b)
    return pl.pallas_call(
        _rwkv_kernel,
        grid=(batch // nb, seq // c),
        in_specs=[zspec(0), zspec(1), zspec(2), zspec(3),
                  pl.BlockSpec((nb, c, LANES), lambda b, i: (b, i, U_RW_WA))]
                 + [full(a) for a in params],
        out_specs=pl.BlockSpec((nb, c, RWKV_WIDTH), lambda b, i: (b, i, 0)),
        out_shape=jax.ShapeDtypeStruct((batch, seq, RWKV_WIDTH), BF16),
        scratch_shapes=[
            pltpu.VMEM((nb, RWKV_HEADS, HEAD_DIM, HEAD_DIM), F32),
            pltpu.VMEM((nb, 4, 8, RWKV_WIDTH), F32),
            pltpu.VMEM((nb, 8, LANES), F32),
            pltpu.VMEM((nb, c, RWKV_WIDTH), F32),
        ],
        compiler_params=pltpu.CompilerParams(
            dimension_semantics=("parallel", "arbitrary"), vmem_limit_bytes=VMEM_LIMIT),
        name="rwkv7",
    )(z3d, z3d, z3d, z3d, z3d, *params)


def _dsa_kernel(zq_ref, zg_ref, ziq_ref, zkv_ref, zms_ref, c64_ref, s64_ref, c32_ref, s32_ref, o_ref,
                k_ref, vt_ref, ik_ref, key_ref):
    seq = zkv_ref.shape[0]
    blk = ATT_BLOCK
    kb_rows = DSA_KEY_BLOCK
    qb = pl.program_id(1)
    n_kb = (qb * blk) // kb_rows + 1

    @pl.when(qb == 0)
    def _():
        lane = lax.broadcasted_iota(I32, (kb_rows, LANES), 1)

        def body(t, carry):
            rows = pl.ds(pl.multiple_of(t * kb_rows, kb_rows), kb_rows)
            kv = zkv_ref[rows, :]
            kvr = jnp.where(lane < HEAD_DIM, _rope(kv, c64_ref[rows, :], s64_ref[rows, :], 8, HEAD_DIM), kv)
            k_ref[rows, :] = kvr[:, :HEAD_DIM].astype(BF16)
            vt_ref[t] = kvr.T[HEAD_DIM:, :].astype(BF16)
            ms = zms_ref[rows, :]
            ikr = _rope(ms, c32_ref[rows, :], s32_ref[rows, :], 4, IDX_DIM)
            ik_ref[rows, :] = ikr[:, :IDX_DIM].astype(BF16)
            return carry

        lax.fori_loop(0, seq // kb_rows, body, 0)

    r0 = pl.multiple_of(qb * blk, blk)
    qrows = pl.ds(r0, blk)
    c64 = c64_ref[qrows, :]
    s64 = s64_ref[qrows, :]
    c32 = c32_ref[qrows, :]
    s32 = s32_ref[qrows, :]
    q2 = [_rope(zq_ref[:, i * LANES:(i + 1) * LANES], c64, s64, 8, HEAD_DIM) * QK_SCALE_LOG2
          for i in range(2)]
    q_stack = jnp.concatenate(
        [q2[h // 2][:, (h % 2) * HEAD_DIM:(h % 2 + 1) * HEAD_DIM] for h in range(DSA_HEADS)],
        axis=0).astype(BF16)
    iq2 = [_rope(ziq_ref[:, i * LANES:(i + 1) * LANES], c32, s32, 4, IDX_DIM) for i in range(2)]
    iq_stack = jnp.concatenate(
        [iq2[h // 4][:, (h % 4) * IDX_DIM:(h % 4 + 1) * IDX_DIM] for h in range(IDX_HEADS)],
        axis=0).astype(BF16)
    iw_t = zms_ref[qrows, :].T[IDX_DIM:IDX_DIM + IDX_HEADS, :] * ((IDX_HEADS * IDX_DIM) ** -0.5)
    iw_row = jnp.concatenate([iw_t[h:h + 1, :] for h in range(IDX_HEADS)], axis=1)

    kpos_l = lax.broadcasted_iota(I32, (kb_rows, blk), 0)
    qpos = r0 + lax.broadcasted_iota(I32, (kb_rows, blk), 1)

    tri = jnp.where(lax.broadcasted_iota(I32, (kb_rows, kb_rows), 1)
                    <= lax.broadcasted_iota(I32, (kb_rows, kb_rows), 0), 1.0, 0.0).astype(BF16)
    g = zg_ref[...]
    gate = g * _sigmoid(g)

    def step(n):
        rows = [slice(kb * kb_rows, (kb + 1) * kb_rows) for kb in range(n)]
        causal = (n - 1) * kb_rows + kpos_l <= qpos

        for kb in range(n):
            s = jnp.maximum(_dot_nt(ik_ref[rows[kb], :], iq_stack), 0.0) * iw_row
            sc = s[:, 0:blk]
            for h in range(1, IDX_HEADS):
                sc = sc + s[:, h * blk:(h + 1) * blk]
            if kb == n - 1:
                sc = jnp.where(causal, sc, -jnp.inf)
            bits = pltpu.bitcast(sc, I32)
            key_ref[rows[kb], :] = bits ^ ((bits >> 31) & 0x7FFFFFFF)

        def count(pred_fn):
            acc = jnp.zeros((8, blk), I32)
            for kb in range(n):
                hit = jnp.where(pred_fn(key_ref[rows[kb], :]), 1, 0).astype(I32)
                acc = acc + hit.reshape(kb_rows // 8, 8, blk).sum(axis=0)
            return acc.sum(axis=0, keepdims=True)

        def bit_body(i, lo):
            cand = lo + (jnp.int32(1) << (31 - i))
            return jnp.where(count(lambda keys: keys >= cand) >= DSA_TOPK, cand, lo)

        lo0 = jnp.full((1, blk), INT_MIN, I32)
        search = lambda: lax.fori_loop(0, 32, bit_body, lo0)
        if n * kb_rows - blk + 1 > DSA_TOPK:
            thr = search()
        else:
            thr = lax.cond(qb * blk >= DSA_TOPK, search, lambda: lo0)
        need = (DSA_TOPK - count(lambda keys: keys > thr)).astype(F32)

        wide = DSA_HEADS * blk
        m = jnp.full((1, wide), NEG_BIG, F32)
        l = jnp.zeros((1, wide), F32)
        acc = jnp.zeros((HEAD_DIM, wide), F32)
        run = jnp.zeros((1, blk), F32)
        for kb in range(n):
            keys = key_ref[rows[kb], :]
            eq = keys == thr
            rank = _dot(tri, jnp.where(eq, 1.0, 0.0).astype(BF16)) + run
            run = rank[kb_rows - 1:kb_rows, :]
            take = (keys > thr) | (eq & (rank <= need))
            if kb == n - 1:
                take = take & causal
            bias = jnp.where(take, 0.0, NEG_BIG)
            s = _dot_nt(k_ref[rows[kb], :], q_stack)
            s = s + jnp.concatenate([bias] * DSA_HEADS, axis=1)
            m_new = jnp.maximum(m, jnp.max(s, axis=0, keepdims=True))
            p = jnp.exp2(s - m_new)
            alpha = jnp.exp2(m - m_new)
            l = alpha * l + jnp.sum(p, axis=0, keepdims=True)
            acc = alpha * acc + _dot(vt_ref[kb], p.astype(BF16))
            m = m_new

        o_t = acc / l
        o = jnp.concatenate(
            [jnp.concatenate([o_t[:, (2 * i) * blk:(2 * i + 1) * blk], o_t[:, (2 * i + 1) * blk:(2 * i + 2) * blk]],
                             axis=0).T for i in range(DSA_HEADS // 2)], axis=1)
        o_ref[...] = (o * gate).astype(o_ref.dtype)

    for n in range(1, seq // kb_rows + 1):
        pl.when(n_kb == n)(functools.partial(step, n))


def _dsa(z2d, batch, seq, c64, s64, c32, s32):
    blk = ATT_BLOCK
    nq = seq // blk
    qspec = lambda u2: pl.BlockSpec((blk, 2 * LANES), lambda b, i, u2=u2: (b * nq + i, u2))
    seqspec = lambda u: pl.BlockSpec((seq, LANES), lambda b, i, u=u: (b, u))
    tab = pl.BlockSpec((seq, LANES), lambda b, i: (0, 0))
    return pl.pallas_call(
        _dsa_kernel,
        grid=(batch, nq),
        in_specs=[qspec(U_DS_Q // 2), qspec(U_DS_G // 2), qspec(U_DS_IQ // 2),
                  seqspec(U_DS_KV), seqspec(U_DS_MISC), tab, tab, tab, tab],
        out_specs=pl.BlockSpec((blk, DSA_WIDTH), lambda b, i: (b * nq + i, 0)),
        out_shape=jax.ShapeDtypeStruct((batch * seq, DSA_WIDTH), BF16),
        scratch_shapes=[
            pltpu.VMEM((seq, HEAD_DIM), BF16),
            pltpu.VMEM((seq // DSA_KEY_BLOCK, HEAD_DIM, DSA_KEY_BLOCK), BF16),
            pltpu.VMEM((seq, IDX_DIM), BF16),
            pltpu.VMEM((seq, blk), I32),
        ],
        compiler_params=pltpu.CompilerParams(
            dimension_semantics=("parallel", "arbitrary"), vmem_limit_bytes=VMEM_LIMIT),
        name="dsa",
    )(z2d, z2d, z2d, z2d, z2d, c64, s64, c32, s32)


def _dil_group(gi, rate, q_ref, k_ref, v_ref, g_ref, c64_ref, s64_ref, qr_ref, kr_ref, o_s, lse_s):
    seq = q_ref.shape[0]
    blk = ATT_BLOCK
    rope_rows = 4 * blk
    nrow = seq // rope_rows

    def rope_body(t, carry):
        rows = pl.ds(pl.multiple_of(t * rope_rows, rope_rows), rope_rows)
        ct = c64_ref[rows, :]
        st = s64_ref[rows, :]
        q = q_ref[rows, :]
        k = k_ref[rows, :]
        qr_ref[rows, :] = (q * ct + pltpu.roll(q, HEAD_DIM, 1) * st) * QK_SCALE_LOG2
        kr_ref[rows, :] = k * ct + pltpu.roll(k, HEAD_DIM, 1) * st
        return carry

    lax.fori_loop(0, nrow, rope_body, 0)

    nblk = seq // (blk * rate)
    qi = lax.broadcasted_iota(I32, (blk, 2 * blk), 0)
    kj = lax.broadcasted_iota(I32, (blk, 2 * blk), 1)
    band = (kj >= qi) & (kj <= qi + DIL_STEPS)

    def strided(ref, start):
        if rate == 1:
            return ref[pl.ds(start, blk), :]
        return ref[pl.ds(start, blk, stride=rate), :]

    def store(ref, start, val):
        if rate == 1:
            ref[gi, pl.ds(start, blk), :] = val
        else:
            ref[gi, pl.ds(start, blk, stride=rate), :] = val

    first_q = lax.broadcasted_iota(I32, (blk, LANES), 1) < HEAD_DIM
    first_kv = lax.broadcasted_iota(I32, (2 * blk, LANES), 1) < HEAD_DIM
    lane64 = lax.broadcasted_iota(I32, (blk, LANES), 1) % HEAD_DIM
    first_qk = (lane64 < 8) | ((lane64 >= 16) & (lane64 < 40))
    unroll = DIL_UNROLL

    def body(it, carry):
        starts, qs, ks, vs, masks = [], [], [], [], []
        for u in range(unroll):
            cidx = it * unroll + u
            rho = cidx // nblk
            j = cidx % nblk
            start = j * (blk * rate) + rho
            pstart = jnp.maximum(j - 1, 0) * (blk * rate) + rho
            starts.append(start)
            qs.append(strided(qr_ref, start))
            ks.append(jnp.concatenate([strided(kr_ref, pstart), strided(kr_ref, start)], axis=0).astype(BF16))
            vs.append(jnp.concatenate([strided(v_ref, pstart), strided(v_ref, start)], axis=0))
            masks.append(band & ((kj >= blk) | (j > 0)))
        prob = [(u, h) for u in range(unroll) for h in range(2)]
        qh = [jnp.where(first_qk == (h == 0), qs[u], 0.0).astype(BF16) for u, h in prob]
        vh = [jnp.where(first_kv == (h == 0), vs[u], 1.0).astype(BF16) for u, h in prob]
        s = [jnp.where(masks[u], _dot_nt(qh[i], ks[u]), -jnp.inf) for i, (u, h) in enumerate(prob)]
        m = [jnp.max(x, axis=-1, keepdims=True) for x in s]
        r = [_dot(jnp.exp2(s[i] - m[i]).astype(BF16), vh[i]) for i in range(len(prob))]
        for u in range(unroll):
            r0, r1 = r[2 * u], r[2 * u + 1]
            o = jnp.where(first_q, r0, r1)
            l = pltpu.roll(jnp.where(first_q, r1, r0), HEAD_DIM, 1)
            lse = jnp.where(first_q, m[2 * u], m[2 * u + 1]) * math.log(2.0) + jnp.log(l)
            gate = strided(g_ref, starts[u])
            store(o_s, starts[u], o / l * (gate * _sigmoid(gate)))
            store(lse_s, starts[u], lse)
        return carry

    lax.fori_loop(0, rate * nblk // unroll, body, 0)


def _dil_kernel(q_ref, k_ref, v_ref, g_ref, c64_ref, s64_ref, o_ref, qr_ref, kr_ref, o_s, lse_s):
    grp = pl.program_id(1)
    for gi, rate in enumerate(DIL_RATES):
        pl.when(grp == gi)(functools.partial(
            _dil_group, gi, rate, q_ref, k_ref, v_ref, g_ref, c64_ref, s64_ref, qr_ref, kr_ref, o_s, lse_s))

    @pl.when(grp == len(DIL_RATES) - 1)
    def _():
        blk = ATT_BLOCK

        def body(t, carry):
            rows = pl.ds(pl.multiple_of(t * blk, blk), blk)
            ls = [lse_s[i, rows, :] for i in range(3)]
            top = jnp.maximum(jnp.maximum(ls[0], ls[1]), ls[2])
            es = [jnp.exp(x - top) for x in ls]
            inv = 1.0 / (es[0] + es[1] + es[2])
            o_ref[rows, :] = jnp.concatenate(
                [o_s[i, rows, :] * (es[i] * inv) for i in range(3)], axis=1).astype(o_ref.dtype)
            return carry

        lax.fori_loop(0, q_ref.shape[0] // blk, body, 0)


def _dilated(z2d, batch, seq, c64, s64):
    ng = len(DIL_RATES)
    zspec = lambda base: pl.BlockSpec((seq, LANES), lambda b, g, base=base: (b, base + g))
    tab = pl.BlockSpec((seq, LANES), lambda b, g: (0, 0))
    return pl.pallas_call(
        _dil_kernel,
        grid=(batch, ng),
        in_specs=[zspec(U_DL), zspec(U_DL + 3), zspec(U_DL + 6), zspec(U_DL + 9), tab, tab],
        out_specs=pl.BlockSpec((seq, DIL_WIDTH), lambda b, g: (b, 0)),
        out_shape=jax.ShapeDtypeStruct((batch * seq, DIL_WIDTH), BF16),
        scratch_shapes=[
            pltpu.VMEM((seq, LANES), F32),
            pltpu.VMEM((seq, LANES), F32),
            pltpu.VMEM((ng, seq, LANES), F32),
            pltpu.VMEM((ng, seq, LANES), F32),
        ],
        compiler_params=pltpu.CompilerParams(
            dimension_semantics=("parallel", "arbitrary"), vmem_limit_bytes=VMEM_LIMIT),
        name="dilated",
    )(z2d, z2d, z2d, z2d, c64, s64)


def _rope_tables(seq, head, half):
    inv = ROPE_THETA ** (-jnp.arange(half, dtype=F32) / half)
    ang = jnp.arange(seq, dtype=F32)[:, None] * inv[None, :]
    cos, sin = jnp.cos(ang), jnp.sin(ang)
    lane = jnp.arange(LANES) % head
    idx = lane % half
    cos_t = jnp.where(lane[None, :] < 2 * half, cos[:, idx], 1.0)
    sin_t = jnp.where(lane[None, :] < half, -sin[:, idx], jnp.where(lane[None, :] < 2 * half, sin[:, idx], 0.0))
    return cos_t, sin_t


def _dil_pair_lanes():
    half, rest = HEAD_DIM // 8, (HEAD_DIM - HEAD_DIM // 4) // 2
    src = []
    for part in range(2):
        for head in range(2):
            src += [head * HEAD_DIM + part * half + i for i in range(half)]
        for head in range(2):
            src += [head * HEAD_DIM + 2 * half + part * rest + i for i in range(rest)]
    return src


def _rope_tables_paired(seq):
    half = HEAD_DIM // 8
    inv = ROPE_THETA ** (-jnp.arange(half, dtype=F32) / half)
    ang = jnp.arange(seq, dtype=F32)[:, None] * inv[None, :]
    cos, sin = jnp.cos(ang), jnp.sin(ang)
    lane = jnp.arange(LANES)[None, :]
    idx = jnp.arange(LANES) % half
    rotated = lane % HEAD_DIM < 2 * half
    cos_t = jnp.where(rotated, cos[:, idx], 1.0)
    sin_t = jnp.where(rotated, jnp.where(lane < HEAD_DIM, -sin[:, idx], sin[:, idx]), 0.0)
    return cos_t, sin_t


def _permute_in_cols(w):
    rw = RWKV_WIDTH
    o = 0
    r, k, v, g = (w[..., o + i * rw:o + (i + 1) * rw] for i in range(4))
    o = 4 * rw
    wd, ad = w[..., o:o + LORA_RANK], w[..., o + LORA_RANK:o + 2 * LORA_RANK]
    o += 2 * LORA_RANK
    widths = (DSA_WIDTH, HEAD_DIM, HEAD_DIM, IDX_HEADS * IDX_DIM, IDX_DIM, IDX_HEADS, DSA_WIDTH)
    parts = []
    for wdt in widths:
        parts.append(w[..., o:o + wdt])
        o += wdt
    dq, dk, dv, diq, dik, diw, dg = parts
    cq, ck, cv, cg = (w[..., o + i * DIL_WIDTH:o + (i + 1) * DIL_WIDTH] for i in range(4))
    pair = jnp.asarray([t * LANES + s for t in range(DIL_WIDTH // LANES) for s in _dil_pair_lanes()])
    cq, ck = cq[..., pair], ck[..., pair]
    pad = jnp.zeros(w.shape[:-1] + (LANES - IDX_DIM - IDX_HEADS,), w.dtype)
    return jnp.concatenate([r, k, v, g, cq, ck, cv, cg, dq, dg, diq, dk, dv, dik, diw, pad, wd, ad], axis=-1)


def kernel(x, p, norm_g, w_in, tshift_mu, rwkv_w0, rwkv_w_up, rwkv_a0, rwkv_a_up, rwkv_k_k, rwkv_k_a, rwkv_r_k,
           rwkv_ln_g, rwkv_ln_b, w_out, ple_norm_g, ple_w_gate, ple_w_proj, final_norm_g):
    batch, seq, d_model = x.shape
    depth = w_in.shape[0]
    n = batch * seq
    assert seq % (ATT_BLOCK * max(DIL_RATES)) == 0 and seq >= 4 * DSA_TOPK

    c64, s64 = _rope_tables(seq, HEAD_DIM, HEAD_DIM // 8)
    c32, s32 = _rope_tables(seq, IDX_DIM, IDX_DIM // 8)
    c64p, s64p = _rope_tables_paired(seq)
    w_in_k = _permute_in_cols(w_in.astype(BF16))
    rw = RWKV_WIDTH
    row = lambda a: a.reshape(1, -1)

    x2d = x.reshape(n, d_model)
    p3d = p.reshape(depth, n, p.shape[-1])
    for i in range(depth):
        z = _in_proj(x2d, row(norm_g[i]), w_in_k, i)
        mu = tshift_mu[i]
        y_a = _rwkv(z.reshape(batch, seq, Z_COLS), mu[:4 * rw].reshape(4, rw), row(mu[4 * rw:]), row(rwkv_w0[i]), rwkv_w_up[i],
                    row(rwkv_a0[i]), rwkv_a_up[i], row(rwkv_k_k[i]), row(rwkv_k_a[i]), row(rwkv_r_k[i]),
                    row(rwkv_ln_g[i]), row(rwkv_ln_b[i])).reshape(n, rw)
        y_b = _dsa(z, batch, seq, c64, s64, c32, s32)
        y_c = _dilated(z, batch, seq, c64p, s64p)
        wo = w_out[i].astype(BF16)
        x2d = _out_proj(x2d, y_a, y_b, y_c, p3d, i, wo[:rw], wo[rw:rw + DSA_WIDTH],
                        wo[rw + DSA_WIDTH:], row(ple_norm_g[i]), ple_w_gate[i].astype(BF16),
                        ple_w_proj[i].astype(BF16), row(final_norm_g), final=(i == depth - 1))
    return x2d.reshape(batch, seq, d_model)
```

```python
import functools
import math

import jax
import jax.numpy as jnp
from jax import lax
from jax.experimental import pallas as pl
from jax.experimental.pallas import tpu as pltpu

F32 = jnp.float32
BF16 = jnp.bfloat16
I32 = jnp.int32

HEAD_DIM = 64
ROPE_THETA = 500000.0
NORM_EPS = 1e-6
RWKV_HEADS = 6
RWKV_WIDTH = RWKV_HEADS * HEAD_DIM
LORA_RANK = 64
GN_EPS = 64e-5
DECAY_SCALE = math.exp(-0.5)
DSA_HEADS = 4
DSA_WIDTH = DSA_HEADS * HEAD_DIM
IDX_HEADS = 8
IDX_DIM = 32
DSA_TOPK = 256
DIL_RATES = (1, 4, 16)
DIL_STEPS = 128
DIL_UNROLL = 4
DIL_WIDTH = 384
ATT_BLOCK = 128
DSA_KEY_BLOCK = 512

LANES = 128
MXU_WIDTH = 256
VMEM_LIMIT = 48 * 1024 * 1024

U_RW = 0
U_DL = 12
U_DS_Q = 24
U_DS_G = 26
U_DS_IQ = 28
U_DS_KV = 30
U_DS_MISC = 31
U_RW_WA = 32
Z_UNITS = 33
Z_COLS = Z_UNITS * LANES

RWKV_CHUNK = 128
RWKV_BATCH_ROWS = 4
INV_BASE = 16
QK_SCALE_LOG2 = HEAD_DIM ** -0.5 * math.log2(math.e)
NEG_BIG = -1e30
INT_MIN = -2 ** 31


def _dot(a, b):
    return jnp.dot(a, b, preferred_element_type=F32)


def _dot_nt(a, b):
    return lax.dot_general(a, b, (((1,), (1,)), ((), ())), preferred_element_type=F32)


def _split(x):
    hi = x.astype(BF16)
    lo = (x - hi.astype(F32)).astype(BF16)
    return hi, lo


def _head_sum(x, head_ones):
    return _dot(x.astype(BF16), head_ones)


def _dot_x3(x, w):
    xh, xl = _split(x)
    wh, wl = _split(w)
    return _dot(xh, wh) + (_dot(xh, wl) + _dot(xl, wh))


def _sigmoid(x):
    return 1.0 / (1.0 + jnp.exp(-x))


def _rope(x, cos_t, sin_t, half, head):
    lane = lax.broadcasted_iota(I32, x.shape, 1) % head
    partner = jnp.where(lane < half, pltpu.roll(x, LANES - half, 1), pltpu.roll(x, half, 1))
    return x * cos_t + partner * sin_t


def _in_proj_kernel(x_ref, g_ref, w_ref, z_ref, *, tn):
    x = x_ref[...]
    ms = jnp.mean(x * x, axis=-1, keepdims=True)
    h = (x * lax.rsqrt(ms + NORM_EPS) * g_ref[...]).astype(BF16)
    for c in range(0, Z_COLS, tn):
        wdt = min(tn, Z_COLS - c)
        z_ref[:, c:c + wdt] = _dot(h, w_ref[:, c:c + wdt])


def _in_proj(x2d, g, w_all, layer, tm=512, tn=MXU_WIDTH):
    n, d = x2d.shape
    return pl.pallas_call(
        functools.partial(_in_proj_kernel, tn=tn),
        grid=(n // tm,),
        in_specs=[
            pl.BlockSpec((tm, d), lambda i: (i, 0)),
            pl.BlockSpec((1, d), lambda i: (0, 0)),
            pl.BlockSpec((None, d, Z_COLS), lambda i: (layer, 0, 0)),
        ],
        out_specs=pl.BlockSpec((tm, Z_COLS), lambda i: (i, 0)),
        out_shape=jax.ShapeDtypeStruct((n, Z_COLS), F32),
        compiler_params=pltpu.CompilerParams(
            dimension_semantics=("parallel",), vmem_limit_bytes=VMEM_LIMIT),
        name="in_proj",
    )(x2d, g, w_all)


def _out_proj_kernel(x_ref, ya_ref, yb_ref, yc_ref, p_ref, wa_ref, wb_ref, wc_ref, ng_ref, wg_ref,
                     wp_ref, fg_ref, o_ref, *, final):
    acc = x_ref[...] + _dot(ya_ref[...], wa_ref[...])
    acc = acc + _dot(yb_ref[...], wb_ref[...])
    acc = acc + _dot(yc_ref[...], wc_ref[...])
    ms = jnp.mean(acc * acc, axis=-1, keepdims=True)
    hn = (acc * lax.rsqrt(ms + NORM_EPS) * ng_ref[...]).astype(BF16)
    gate = _sigmoid(_dot(hn, wg_ref[...]))
    out = acc + gate * _dot(p_ref[...].astype(BF16), wp_ref[...])
    if final:
        ms = jnp.mean(out * out, axis=-1, keepdims=True)
        out = out * lax.rsqrt(ms + NORM_EPS) * fg_ref[...]
    o_ref[...] = out


def _out_proj(x2d, ya, yb, yc, p3d, layer, wa, wb, wc, ng, wg, wp, fg, final, tm=512):
    n, d = x2d.shape
    row = lambda w: pl.BlockSpec((tm, w), lambda i: (i, 0))
    full = lambda a: pl.BlockSpec(a.shape, lambda i: (0, 0))
    return pl.pallas_call(
        functools.partial(_out_proj_kernel, final=final),
        grid=(n // tm,),
        in_specs=[row(d), row(ya.shape[1]), row(yb.shape[1]), row(yc.shape[1]),
                  pl.BlockSpec((None, tm, p3d.shape[2]), lambda i: (layer, i, 0)),
                  full(wa), full(wb), full(wc), full(ng), full(wg), full(wp), full(fg)],
        out_specs=row(d),
        out_shape=jax.ShapeDtypeStruct((n, d), F32),
        compiler_params=pltpu.CompilerParams(
            dimension_semantics=("parallel",), vmem_limit_bytes=VMEM_LIMIT),
        name="out_proj",
    )(x2d, ya, yb, yc, p3d, wa, wb, wc, ng, wg, wp, fg)


def _unit_lower_inverse(n_list, row, col):
    c = n_list[0].shape[0]
    eye = jnp.where(row == col, 1.0, 0.0).astype(F32)
    same = (row // INV_BASE) == (col // INV_BASE)
    pw = [jnp.where(same, n, 0.0) for n in n_list]
    t = [eye - x for x in pw]
    span = 2
    while span < INV_BASE:
        pwb = [x.astype(BF16) for x in pw]
        pw = [_dot(x, x) for x in pwb]
        t = [x + _dot(x.astype(BF16), y.astype(BF16)) for x, y in zip(t, pw)]
        span *= 2
    size = INV_BASE
    while size < c:
        lower_left = ((row // (2 * size)) == (col // (2 * size))) & ((row // size) != (col // size))
        tb = [x.astype(BF16) for x in t]
        et = [_dot(jnp.where(lower_left, n, 0.0).astype(BF16), x).astype(BF16) for n, x in zip(n_list, tb)]
        t = [x - _dot(xb, y) for x, xb, y in zip(t, tb, et)]
        size *= 2
    return t


def _rwkv_kernel(zr_ref, zk_ref, zv_ref, zg_ref, zwa_ref, mu_ref, muwa_ref, w0_ref, wup_ref, a0_ref,
                 aup_ref, kk_ref, ka_ref, rk_ref, lng_ref, lnb_ref, o_ref,
                 state_ref, prev_ref, prevwa_ref, y_ref):
    nb, c = zr_ref.shape[0], zr_ref.shape[1]
    width = RWKV_WIDTH

    @pl.when(pl.program_id(1) == 0)
    def _():
        state_ref[...] = jnp.zeros_like(state_ref)
        prev_ref[...] = jnp.zeros_like(prev_ref)
        prevwa_ref[...] = jnp.zeros_like(prevwa_ref)

    first_row = lax.broadcasted_iota(I32, (c, width), 0) == 0
    first_row_wa = lax.broadcasted_iota(I32, (c, LANES), 0) == 0
    hrow = lax.broadcasted_iota(I32, (width, width), 0) // HEAD_DIM
    hcol = lax.broadcasted_iota(I32, (width, width), 1) // HEAD_DIM
    head_ones = jnp.where(hrow == hcol, 1.0, 0.0).astype(BF16)
    row = lax.broadcasted_iota(I32, (c, c), 0)
    col = lax.broadcasted_iota(I32, (c, c), 1)
    tri = jnp.where(col <= row, 1.0, 0.0).astype(BF16)
    strict = col < row
    incl = col <= row
    hs = [slice(h * HEAD_DIM, (h + 1) * HEAD_DIM) for h in range(RWKV_HEADS)]

    def prepare(bi):
        def shifted(z_ref, idx):
            z = z_ref[bi]
            prev = jnp.where(first_row, prev_ref[bi, idx, 0:1, :], pltpu.roll(z, 1, 0))
            prev_ref[bi, idx, 0:1, :] = z[c - 1:c, :]
            return z + (prev - z) * mu_ref[idx:idx + 1, :]

        r = shifted(zr_ref, 0)
        k = shifted(zk_ref, 1)
        v = shifted(zv_ref, 2)
        g = shifted(zg_ref, 3)
        zwa = zwa_ref[bi]
        prev_wa = jnp.where(first_row_wa, prevwa_ref[bi, 0:1, :], pltpu.roll(zwa, 1, 0))
        prevwa_ref[bi, 0:1, :] = zwa[c - 1:c, :]
        wa = zwa + (prev_wa - zwa) * muwa_ref[...]
        wd = wa[:, :LORA_RANK]
        ad = wa[:, LORA_RANK:]

        lw = -DECAY_SCALE * _sigmoid(w0_ref[...] + _dot_x3(jnp.tanh(wd), wup_ref[...]))
        a = _sigmoid(a0_ref[...] + _dot_x3(ad, aup_ref[...]))
        kk = k * kk_ref[...]
        norm = jnp.sqrt(_head_sum(kk * kk, head_ones))
        kk = kk / jnp.maximum(norm, 1e-12)
        kmod = k * (1.0 + (a - 1.0) * ka_ref[...])
        b = a * kk

        lw_hi, lw_lo = _split(lw)
        cl = _dot(tri, lw_hi) + _dot(tri, lw_lo)
        c_mid = cl[c // 2 - 1:c // 2, :]
        c_end = cl[c - 1:c, :]
        w = jnp.exp(lw)
        e_in1 = jnp.exp(cl - lw)
        e_in2 = e_in1 * w
        e_k = jnp.exp(c_mid - cl)
        e_mid_inv = jnp.exp(-c_mid)
        e_end = e_k * jnp.exp(c_end - c_mid)
        ops = dict(
            q1=kk * e_in1 * e_mid_inv, q2=r * e_in2 * e_mid_inv, k1=b * e_k, k2=kmod * e_k,
            kkd=kk * e_in1, rd=r * e_in2, bd=b * e_end, kd=kmod * e_end, vb=v)
        ops = {name: [x.astype(BF16)[:, s] for s in hs] for name, x in ops.items()}
        return dict(ops, r=r, kmod=kmod, v=v, g=g, d_end=jnp.exp(c_end))

    prep = [prepare(bi) for bi in range(nb)]

    probs = [(bi, h) for bi in range(nb) for h in range(RWKV_HEADS)]
    op = lambda name: [prep[bi][name][h] for bi, h in probs]
    q1, q2, k1, k2, kkd, rd, bd, kd, vh = map(op, ("q1", "q2", "k1", "k2", "kkd", "rd", "bd", "kd", "vb"))
    n = range(len(probs))
    a_ab = [jnp.where(strict, _dot_nt(q1[i], k1[i]), 0.0) for i in n]
    t_inv = _unit_lower_inverse(a_ab, row, col)
    a_ak = [jnp.where(strict, _dot_nt(q1[i], k2[i]), 0.0).astype(BF16) for i in n]
    a_rb = [jnp.where(incl, _dot_nt(q2[i], k1[i]), 0.0).astype(BF16) for i in n]
    a_rk = [jnp.where(incl, _dot_nt(q2[i], k2[i]), 0.0).astype(BF16) for i in n]
    s_old = [state_ref[bi, h] for bi, h in probs]
    s_in = [s.astype(BF16) for s in s_old]
    rhs = [_dot_nt(kkd[i], s_in[i]) + _dot(a_ak[i], vh[i]) for i in n]
    u = [-_dot(t_inv[i].astype(BF16), rhs[i].astype(BF16)) for i in n]
    uv_t = [jnp.concatenate([u[i], prep[bi]["v"][:, hs[h]]], axis=1).T.astype(BF16)
            for i, (bi, h) in enumerate(probs)]
    for i, (bi, h) in enumerate(probs):
        state_ref[bi, h] = (s_old[i] * prep[bi]["d_end"][:, hs[h]]
                            + _dot(uv_t[i][:HEAD_DIM, :], bd[i]) + _dot(uv_t[i][HEAD_DIM:, :], kd[i]))
    for i, (bi, h) in enumerate(probs):
        y_ref[bi, :, hs[h]] = (_dot_nt(rd[i], s_in[i]) + _dot(a_rb[i], u[i].astype(BF16))
                               + _dot(a_rk[i], vh[i]))

    inv_n = 1.0 / HEAD_DIM
    for bi in range(nb):
        pb = prep[bi]
        y = y_ref[bi]
        mean = _head_sum(y, head_ones) * inv_n
        dlt = y - mean
        var = _head_sum(dlt * dlt, head_ones) * inv_n
        yn = dlt * lax.rsqrt(var + GN_EPS) * lng_ref[...] + lnb_ref[...]
        bonus = _head_sum(pb["r"] * pb["kmod"] * rk_ref[...], head_ones) * pb["v"]
        g = pb["g"]
        o_ref[bi] = ((yn + bonus) * (g * _sigmoid(g))).astype(o_ref.dtype)


def _rwkv(z3d, mu4, mu_wa, w0, w_up, a0, a_up, k_k, k_a, r_k, ln_g, ln_b):
    batch, seq, _ = z3d.shape
    c = RWKV_CHUNK
    nb = RWKV_BATCH_ROWS if batch % RWKV_BATCH_ROWS == 0 else 1
    zspec = lambda u: pl.BlockSpec((nb, c, RWKV_WIDTH), lambda b, i, u=u: (b, i, u))
    full = lambda a: pl.BlockSpec(a.shape, lambda b, i: (0,) * a.ndim)
    params = (mu4, mu_wa, w0, w_up, a0, a_up, k_k, k_a, r_k, ln_g, ln_b)
    return pl.pallas_call(
        _rwkv_kernel,
        grid=(batch // nb, seq // c),
        in_specs=[zspec(0), zspec(1), zspec(2), zspec(3),
                  pl.BlockSpec((nb, c, LANES), lambda b, i: (b, i, U_RW_WA))]
                 + [full(a) for a in params],
        out_specs=pl.BlockSpec((nb, c, RWKV_WIDTH), lambda b, i: (b, i, 0)),
        out_shape=jax.ShapeDtypeStruct((batch, seq, RWKV_WIDTH), BF16),
        scratch_shapes=[
            pltpu.VMEM((nb, RWKV_HEADS, HEAD_DIM, HEAD_DIM), F32),
            pltpu.VMEM((nb, 4, 8, RWKV_WIDTH), F32),
            pltpu.VMEM((nb, 8, LANES), F32),
            pltpu.VMEM((nb, c, RWKV_WIDTH), F32),
        ],
        compiler_params=pltpu.CompilerParams(
            dimension_semantics=("parallel", "arbitrary"), vmem_limit_bytes=VMEM_LIMIT),
        name="rwkv7",
    )(z3d, z3d, z3d, z3d, z3d, *params)


def _dsa_kernel(zq_ref, zg_ref, ziq_ref, zkv_ref, zms_ref, c64_ref, s64_ref, c32_ref, s32_ref, o_ref,
                k_ref, vt_ref, ik_ref, key_ref):
    seq = zkv_ref.shape[0]
    blk = ATT_BLOCK
    kb_rows = DSA_KEY_BLOCK
    qb = pl.program_id(1)
    n_kb = (qb * blk) // kb_rows + 1

    @pl.when(qb == 0)
    def _():
        lane = lax.broadcasted_iota(I32, (kb_rows, LANES), 1)

        def body(t, carry):
            rows = pl.ds(pl.multiple_of(t * kb_rows, kb_rows), kb_rows)
            kv = zkv_ref[rows, :]
            kvr = jnp.where(lane < HEAD_DIM, _rope(kv, c64_ref[rows, :], s64_ref[rows, :], 8, HEAD_DIM), kv)
            k_ref[rows, :] = kvr[:, :HEAD_DIM].astype(BF16)
            vt_ref[t] = kvr.T[HEAD_DIM:, :].astype(BF16)
            ms = zms_ref[rows, :]
            ikr = _rope(ms, c32_ref[rows, :], s32_ref[rows, :], 4, IDX_DIM)
            ik_ref[rows, :] = ikr[:, :IDX_DIM].astype(BF16)
            return carry

        lax.fori_loop(0, seq // kb_rows, body, 0)

    r0 = pl.multiple_of(qb * blk, blk)
    qrows = pl.ds(r0, blk)
    c64 = c64_ref[qrows, :]
    s64 = s64_ref[qrows, :]
    c32 = c32_ref[qrows, :]
    s32 = s32_ref[qrows, :]
    q2 = [_rope(zq_ref[:, i * LANES:(i + 1) * LANES], c64, s64, 8, HEAD_DIM) * QK_SCALE_LOG2
          for i in range(2)]
    q_stack = jnp.concatenate(
        [q2[h // 2][:, (h % 2) * HEAD_DIM:(h % 2 + 1) * HEAD_DIM] for h in range(DSA_HEADS)],
        axis=0).astype(BF16)
    iq2 = [_rope(ziq_ref[:, i * LANES:(i + 1) * LANES], c32, s32, 4, IDX_DIM) for i in range(2)]
    iq_stack = jnp.concatenate(
        [iq2[h // 4][:, (h % 4) * IDX_DIM:(h % 4 + 1) * IDX_DIM] for h in range(IDX_HEADS)],
        axis=0).astype(BF16)
    iw_t = zms_ref[qrows, :].T[IDX_DIM:IDX_DIM + IDX_HEADS, :] * ((IDX_HEADS * IDX_DIM) ** -0.5)
    iw_row = jnp.concatenate([iw_t[h:h + 1, :] for h in range(IDX_HEADS)], axis=1)

    kpos_l = lax.broadcasted_iota(I32, (kb_rows, blk), 0)
    qpos = r0 + lax.broadcasted_iota(I32, (kb_rows, blk), 1)

    tri = jnp.where(lax.broadcasted_iota(I32, (blk, blk), 1)
                    <= lax.broadcasted_iota(I32, (blk, blk), 0), 1.0, 0.0).astype(BF16)
    g = zg_ref[...]
    gate = g * _sigmoid(g)

    def step(n):
        rows = [slice(kb * kb_rows, (kb + 1) * kb_rows) for kb in range(n)]
        causal = (n - 1) * kb_rows + kpos_l <= qpos

        for kb in range(n):
            s = jnp.maximum(_dot_nt(ik_ref[rows[kb], :], iq_stack), 0.0) * iw_row
            sc = s[:, 0:blk]
            for h in range(1, IDX_HEADS):
                sc = sc + s[:, h * blk:(h + 1) * blk]
            if kb == n - 1:
                sc = jnp.where(causal, sc, -jnp.inf)
            bits = pltpu.bitcast(sc, I32)
            key_ref[rows[kb], :] = bits ^ ((bits >> 31) & 0x7FFFFFFF)

        def count(pred_fn):
            acc = jnp.zeros((8, blk), I32)
            for kb in range(n):
                hit = jnp.where(pred_fn(key_ref[rows[kb], :]), 1, 0).astype(I32)
                acc = acc + hit.reshape(kb_rows // 8, 8, blk).sum(axis=0)
            return acc.sum(axis=0, keepdims=True)

        def bit_body(i, lo):
            cand = lo + (jnp.int32(1) << (31 - i))
            return jnp.where(count(lambda keys: keys >= cand) >= DSA_TOPK, cand, lo)

        lo0 = jnp.full((1, blk), INT_MIN, I32)
        search = lambda: lax.fori_loop(0, 32, bit_body, lo0)
        if n * kb_rows - blk + 1 > DSA_TOPK:
            thr = search()
        else:
            thr = lax.cond(qb * blk >= DSA_TOPK, search, lambda: lo0)
        need = (DSA_TOPK - count(lambda keys: keys > thr)).astype(F32)

        wide = DSA_HEADS * blk
        m = jnp.full((1, wide), NEG_BIG, F32)
        l = jnp.zeros((1, wide), F32)
        acc = jnp.zeros((HEAD_DIM, wide), F32)
        run = jnp.zeros((1, blk), F32)
        for kb in range(n):
            keys = key_ref[rows[kb], :]
            eq = keys == thr
            eq_b = jnp.where(eq, 1.0, 0.0).astype(BF16)
            local = [_dot(tri, eq_b[sub * blk:(sub + 1) * blk, :]) for sub in range(kb_rows // blk)]
            ranks = []
            for x in local:
                ranks.append(x + run)
                run = run + x[blk - 1:blk, :]
            rank = jnp.concatenate(ranks, axis=0)
            take = (keys > thr) | (eq & (rank <= need))
            if kb == n - 1:
                take = take & causal
            bias = jnp.where(take, 0.0, NEG_BIG)
            s = _dot_nt(k_ref[rows[kb], :], q_stack)
            s = s + jnp.concatenate([bias] * DSA_HEADS, axis=1)
            m_new = jnp.maximum(m, jnp.max(s, axis=0, keepdims=True))
            p = jnp.exp2(s - m_new)
            alpha = jnp.exp2(m - m_new)
            l = alpha * l + jnp.sum(p, axis=0, keepdims=True)
            acc = alpha * acc + _dot(vt_ref[kb], p.astype(BF16))
            m = m_new

        o_t = acc / l
        o = jnp.concatenate(
            [jnp.concatenate([o_t[:, (2 * i) * blk:(2 * i + 1) * blk], o_t[:, (2 * i + 1) * blk:(2 * i + 2) * blk]],
                             axis=0).T for i in range(DSA_HEADS // 2)], axis=1)
        o_ref[...] = (o * gate).astype(o_ref.dtype)

    for n in range(1, seq // kb_rows + 1):
        pl.when(n_kb == n)(functools.partial(step, n))


def _dsa(z2d, batch, seq, c64, s64, c32, s32):
    blk = ATT_BLOCK
    nq = seq // blk
    qspec = lambda u2: pl.BlockSpec((blk, 2 * LANES), lambda b, i, u2=u2: (b * nq + i, u2))
    seqspec = lambda u: pl.BlockSpec((seq, LANES), lambda b, i, u=u: (b, u))
    tab = pl.BlockSpec((seq, LANES), lambda b, i: (0, 0))
    return pl.pallas_call(
        _dsa_kernel,
        grid=(batch, nq),
        in_specs=[qspec(U_DS_Q // 2), qspec(U_DS_G // 2), qspec(U_DS_IQ // 2),
                  seqspec(U_DS_KV), seqspec(U_DS_MISC), tab, tab, tab, tab],
        out_specs=pl.BlockSpec((blk, DSA_WIDTH), lambda b, i: (b * nq + i, 0)),
        out_shape=jax.ShapeDtypeStruct((batch * seq, DSA_WIDTH), BF16),
        scratch_shapes=[
            pltpu.VMEM((seq, HEAD_DIM), BF16),
            pltpu.VMEM((seq // DSA_KEY_BLOCK, HEAD_DIM, DSA_KEY_BLOCK), BF16),
            pltpu.VMEM((seq, IDX_DIM), BF16),
            pltpu.VMEM((seq, blk), I32),
        ],
        compiler_params=pltpu.CompilerParams(
            dimension_semantics=("parallel", "arbitrary"), vmem_limit_bytes=VMEM_LIMIT),
        name="dsa",
    )(z2d, z2d, z2d, z2d, z2d, c64, s64, c32, s32)


def _dil_group(gi, rate, q_ref, k_ref, v_ref, g_ref, c64_ref, s64_ref, qr_ref, kr_ref, o_s, lse_s):
    seq = q_ref.shape[0]
    blk = ATT_BLOCK
    rope_rows = 4 * blk
    nrow = seq // rope_rows

    def rope_body(t, carry):
        rows = pl.ds(pl.multiple_of(t * rope_rows, rope_rows), rope_rows)
        ct = c64_ref[rows, :]
        st = s64_ref[rows, :]
        q = q_ref[rows, :]
        k = k_ref[rows, :]
        qr_ref[rows, :] = (q * ct + pltpu.roll(q, HEAD_DIM, 1) * st) * QK_SCALE_LOG2
        kr_ref[rows, :] = k * ct + pltpu.roll(k, HEAD_DIM, 1) * st
        return carry

    lax.fori_loop(0, nrow, rope_body, 0)

    nblk = seq // (blk * rate)
    qi = lax.broadcasted_iota(I32, (blk, 2 * blk), 0)
    kj = lax.broadcasted_iota(I32, (blk, 2 * blk), 1)
    band = (kj >= qi) & (kj <= qi + DIL_STEPS)

    def strided(ref, start):
        if rate == 1:
            return ref[pl.ds(start, blk), :]
        return ref[pl.ds(start, blk, stride=rate), :]

    def store(ref, start, val):
        if rate == 1:
            ref[gi, pl.ds(start, blk), :] = val
        else:
            ref[gi, pl.ds(start, blk, stride=rate), :] = val

    first_q = lax.broadcasted_iota(I32, (blk, LANES), 1) < HEAD_DIM
    first_kv = lax.broadcasted_iota(I32, (2 * blk, LANES), 1) < HEAD_DIM
    lane64 = lax.broadcasted_iota(I32, (blk, LANES), 1) % HEAD_DIM
    first_qk = (lane64 < 8) | ((lane64 >= 16) & (lane64 < 40))
    unroll = DIL_UNROLL

    def body(it, carry):
        starts, qs, ks, vs, masks = [], [], [], [], []
        for u in range(unroll):
            cidx = it * unroll + u
            rho = cidx // nblk
            j = cidx % nblk
            start = j * (blk * rate) + rho
            pstart = jnp.maximum(j - 1, 0) * (blk * rate) + rho
            starts.append(start)
            qs.append(strided(qr_ref, start))
            ks.append(jnp.concatenate([strided(kr_ref, pstart), strided(kr_ref, start)], axis=0).astype(BF16))
            vs.append(jnp.concatenate([strided(v_ref, pstart), strided(v_ref, start)], axis=0))
            masks.append(band & ((kj >= blk) | (j > 0)))
        prob = [(u, h) for u in range(unroll) for h in range(2)]
        qh = [jnp.where(first_qk == (h == 0), qs[u], 0.0).astype(BF16) for u, h in prob]
        vh = [jnp.where(first_kv == (h == 0), vs[u], 1.0).astype(BF16) for u, h in prob]
        s = [jnp.where(masks[u], _dot_nt(qh[i], ks[u]), -jnp.inf) for i, (u, h) in enumerate(prob)]
        m = [jnp.max(x, axis=-1, keepdims=True) for x in s]
        r = [_dot(jnp.exp2(s[i] - m[i]).astype(BF16), vh[i]) for i in range(len(prob))]
        for u in range(unroll):
            r0, r1 = r[2 * u], r[2 * u + 1]
            o = jnp.where(first_q, r0, r1)
            l = pltpu.roll(jnp.where(first_q, r1, r0), HEAD_DIM, 1)
            lse = jnp.where(first_q, m[2 * u], m[2 * u + 1]) * math.log(2.0) + jnp.log(l)
            gate = strided(g_ref, starts[u])
            store(o_s, starts[u], o / l * (gate * _sigmoid(gate)))
            store(lse_s, starts[u], lse)
        return carry

    lax.fori_loop(0, rate * nblk // unroll, body, 0)


def _dil_kernel(q_ref, k_ref, v_ref, g_ref, c64_ref, s64_ref, o_ref, qr_ref, kr_ref, o_s, lse_s):
    grp = pl.program_id(1)
    for gi, rate in enumerate(DIL_RATES):
        pl.when(grp == gi)(functools.partial(
            _dil_group, gi, rate, q_ref, k_ref, v_ref, g_ref, c64_ref, s64_ref, qr_ref, kr_ref, o_s, lse_s))

    @pl.when(grp == len(DIL_RATES) - 1)
    def _():
        blk = ATT_BLOCK

        def body(t, carry):
            rows = pl.ds(pl.multiple_of(t * blk, blk), blk)
            ls = [lse_s[i, rows, :] for i in range(3)]
            top = jnp.maximum(jnp.maximum(ls[0], ls[1]), ls[2])
            es = [jnp.exp(x - top) for x in ls]
            inv = 1.0 / (es[0] + es[1] + es[2])
            o_ref[rows, :] = jnp.concatenate(
                [o_s[i, rows, :] * (es[i] * inv) for i in range(3)], axis=1).astype(o_ref.dtype)
            return carry

        lax.fori_loop(0, q_ref.shape[0] // blk, body, 0)


def _dilated(z2d, batch, seq, c64, s64):
    ng = len(DIL_RATES)
    zspec = lambda base: pl.BlockSpec((seq, LANES), lambda b, g, base=base: (b, base + g))
    tab = pl.BlockSpec((seq, LANES), lambda b, g: (0, 0))
    return pl.pallas_call(
        _dil_kernel,
        grid=(batch, ng),
        in_specs=[zspec(U_DL), zspec(U_DL + 3), zspec(U_DL + 6), zspec(U_DL + 9), tab, tab],
        out_specs=pl.BlockSpec((seq, DIL_WIDTH), lambda b, g: (b, 0)),
        out_shape=jax.ShapeDtypeStruct((batch * seq, DIL_WIDTH), BF16),
        scratch_shapes=[
            pltpu.VMEM((seq, LANES), F32),
            pltpu.VMEM((seq, LANES), F32),
            pltpu.VMEM((ng, seq, LANES), F32),
            pltpu.VMEM((ng, seq, LANES), F32),
        ],
        compiler_params=pltpu.CompilerParams(
            dimension_semantics=("parallel", "arbitrary"), vmem_limit_bytes=VMEM_LIMIT),
        name="dilated",
    )(z2d, z2d, z2d, z2d, c64, s64)


def _rope_tables(seq, head, half):
    inv = ROPE_THETA ** (-jnp.arange(half, dtype=F32) / half)
    ang = jnp.arange(seq, dtype=F32)[:, None] * inv[None, :]
    cos, sin = jnp.cos(ang), jnp.sin(ang)
    lane = jnp.arange(LANES) % head
    idx = lane % half
    cos_t = jnp.where(lane[None, :] < 2 * half, cos[:, idx], 1.0)
    sin_t = jnp.where(lane[None, :] < half, -sin[:, idx], jnp.where(lane[None, :] < 2 * half, sin[:, idx], 0.0))
    return cos_t, sin_t


def _dil_pair_lanes():
    half, rest = HEAD_DIM // 8, (HEAD_DIM - HEAD_DIM // 4) // 2
    src = []
    for part in range(2):
        for head in range(2):
            src += [head * HEAD_DIM + part * half + i for i in range(half)]
        for head in range(2):
            src += [head * HEAD_DIM + 2 * half + part * rest + i for i in range(rest)]
    return src


def _rope_tables_paired(seq):
    half = HEAD_DIM // 8
    inv = ROPE_THETA ** (-jnp.arange(half, dtype=F32) / half)
    ang = jnp.arange(seq, dtype=F32)[:, None] * inv[None, :]
    cos, sin = jnp.cos(ang), jnp.sin(ang)
    lane = jnp.arange(LANES)[None, :]
    idx = jnp.arange(LANES) % half
    rotated = lane % HEAD_DIM < 2 * half
    cos_t = jnp.where(rotated, cos[:, idx], 1.0)
    sin_t = jnp.where(rotated, jnp.where(lane < HEAD_DIM, -sin[:, idx], sin[:, idx]), 0.0)
    return cos_t, sin_t


def _permute_in_cols(w):
    rw = RWKV_WIDTH
    o = 0
    r, k, v, g = (w[..., o + i * rw:o + (i + 1) * rw] for i in range(4))
    o = 4 * rw
    wd, ad = w[..., o:o + LORA_RANK], w[..., o + LORA_RANK:o + 2 * LORA_RANK]
    o += 2 * LORA_RANK
    widths = (DSA_WIDTH, HEAD_DIM, HEAD_DIM, IDX_HEADS * IDX_DIM, IDX_DIM, IDX_HEADS, DSA_WIDTH)
    parts = []
    for wdt in widths:
        parts.append(w[..., o:o + wdt])
        o += wdt
    dq, dk, dv, diq, dik, diw, dg = parts
    cq, ck, cv, cg = (w[..., o + i * DIL_WIDTH:o + (i + 1) * DIL_WIDTH] for i in range(4))
    pair = jnp.asarray([t * LANES + s for t in range(DIL_WIDTH // LANES) for s in _dil_pair_lanes()])
    cq, ck = cq[..., pair], ck[..., pair]
    pad = jnp.zeros(w.shape[:-1] + (LANES - IDX_DIM - IDX_HEADS,), w.dtype)
    return jnp.concatenate([r, k, v, g, cq, ck, cv, cg, dq, dg, diq, dk, dv, dik, diw, pad, wd, ad], axis=-1)


def kernel(x, p, norm_g, w_in, tshift_mu, rwkv_w0, rwkv_w_up, rwkv_a0, rwkv_a_up, rwkv_k_k, rwkv_k_a, rwkv_r_k,
           rwkv_ln_g, rwkv_ln_b, w_out, ple_norm_g, ple_w_gate, ple_w_proj, final_norm_g):
    batch, seq, d_model = x.shape
    depth = w_in.shape[0]
    n = batch * seq
    assert seq % (ATT_BLOCK * max(DIL_RATES)) == 0 and seq >= 4 * DSA_TOPK

    c64, s64 = _rope_tables(seq, HEAD_DIM, HEAD_DIM // 8)
    c32, s32 = _rope_tables(seq, IDX_DIM, IDX_DIM // 8)
    c64p, s64p = _rope_tables_paired(seq)
    w_in_k = _permute_in_cols(w_in.astype(BF16))
    rw = RWKV_WIDTH
    row = lambda a: a.reshape(1, -1)

    x2d = x.reshape(n, d_model)
    p3d = p.reshape(depth, n, p.shape[-1])
    for i in range(depth):
        z = _in_proj(x2d, row(norm_g[i]), w_in_k, i)
        mu = tshift_mu[i]
        y_a = _rwkv(z.reshape(batch, seq, Z_COLS), mu[:4 * rw].reshape(4, rw), row(mu[4 * rw:]), row(rwkv_w0[i]), rwkv_w_up[i],
                    row(rwkv_a0[i]), rwkv_a_up[i], row(rwkv_k_k[i]), row(rwkv_k_a[i]), row(rwkv_r_k[i]),
                    row(rwkv_ln_g[i]), row(rwkv_ln_b[i])).reshape(n, rw)
        y_b = _dsa(z, batch, seq, c64, s64, c32, s32)
        y_c = _dilated(z, batch, seq, c64p, s64p)
        wo = w_out[i].astype(BF16)
        x2d = _out_proj(x2d, y_a, y_b, y_c, p3d, i, wo[:rw], wo[rw:rw + DSA_WIDTH],
                        wo[rw + DSA_WIDTH:], row(ple_norm_g[i]), ple_w_gate[i].astype(BF16),
                        ple_w_proj[i].astype(BF16), row(final_norm_g), final=(i == depth - 1))
    return x2d.reshape(batch, seq, d_model)
```

```python
import functools
import math

import jax
import jax.numpy as jnp
from jax import lax
from jax.experimental import pallas as pl
from jax.experimental.pallas import tpu as pltpu

F32 = jnp.float32
BF16 = jnp.bfloat16
I32 = jnp.int32

HEAD_DIM = 64
ROPE_THETA = 500000.0
NORM_EPS = 1e-6
RWKV_HEADS = 6
RWKV_WIDTH = RWKV_HEADS * HEAD_DIM
LORA_RANK = 64
GN_EPS = 64e-5
DECAY_SCALE = math.exp(-0.5)
DSA_HEADS = 4
DSA_WIDTH = DSA_HEADS * HEAD_DIM
IDX_HEADS = 8
IDX_DIM = 32
DSA_TOPK = 256
DIL_RATES = (1, 4, 16)
DIL_STEPS = 128
DIL_UNROLL = 4
DIL_WIDTH = 384
ATT_BLOCK = 128
DSA_KEY_BLOCK = 512

LANES = 128
MXU_WIDTH = 256
VMEM_LIMIT = 48 * 1024 * 1024

U_RW = 0
U_DL = 12
U_DS_Q = 24
U_DS_G = 26
U_DS_IQ = 28
U_DS_KV = 30
U_DS_MISC = 31
U_RW_WA = 32
Z_UNITS = 33
Z_COLS = Z_UNITS * LANES

RWKV_CHUNK = 128
RWKV_BATCH_ROWS = 4
INV_BASE = 16
QK_SCALE_LOG2 = HEAD_DIM ** -0.5 * math.log2(math.e)
NEG_BIG = -1e30
INT_MIN = -2 ** 31


def _dot(a, b):
    return jnp.dot(a, b, preferred_element_type=F32)


def _dot_nt(a, b):
    return lax.dot_general(a, b, (((1,), (1,)), ((), ())), preferred_element_type=F32)


def _split(x):
    hi = x.astype(BF16)
    lo = (x - hi.astype(F32)).astype(BF16)
    return hi, lo


def _head_sum(x, head_ones):
    return _dot(x.astype(BF16), head_ones)


def _dot_x3(x, w):
    xh, xl = _split(x)
    wh, wl = _split(w)
    return _dot(xh, wh) + (_dot(xh, wl) + _dot(xl, wh))


def _sigmoid(x):
    return 1.0 / (1.0 + jnp.exp(-x))


def _rope(x, cos_t, sin_t, half, head):
    lane = lax.broadcasted_iota(I32, x.shape, 1) % head
    partner = jnp.where(lane < half, pltpu.roll(x, LANES - half, 1), pltpu.roll(x, half, 1))
    return x * cos_t + partner * sin_t


def _in_proj_kernel(x_ref, g_ref, w_ref, z_ref, *, tn):
    x = x_ref[...]
    ms = jnp.mean(x * x, axis=-1, keepdims=True)
    h = (x * lax.rsqrt(ms + NORM_EPS) * g_ref[...]).astype(BF16)
    for c in range(0, Z_COLS, tn):
        wdt = min(tn, Z_COLS - c)
        z_ref[:, c:c + wdt] = _dot(h, w_ref[:, c:c + wdt])


def _in_proj(x2d, g, w_all, layer, tm=512, tn=MXU_WIDTH):
    n, d = x2d.shape
    return pl.pallas_call(
        functools.partial(_in_proj_kernel, tn=tn),
        grid=(n // tm,),
        in_specs=[
            pl.BlockSpec((tm, d), lambda i: (i, 0)),
            pl.BlockSpec((1, d), lambda i: (0, 0)),
            pl.BlockSpec((None, d, Z_COLS), lambda i: (layer, 0, 0)),
        ],
        out_specs=pl.BlockSpec((tm, Z_COLS), lambda i: (i, 0)),
        out_shape=jax.ShapeDtypeStruct((n, Z_COLS), F32),
        compiler_params=pltpu.CompilerParams(
            dimension_semantics=("parallel",), vmem_limit_bytes=VMEM_LIMIT),
        name="in_proj",
    )(x2d, g, w_all)


def _out_proj_kernel(x_ref, ya_ref, yb_ref, yc_ref, p_ref, wa_ref, wb_ref, wc_ref, ng_ref, wg_ref,
                     wp_ref, fg_ref, o_ref, *, final):
    acc = x_ref[...] + _dot(ya_ref[...], wa_ref[...])
    acc = acc + _dot(yb_ref[...], wb_ref[...])
    acc = acc + _dot(yc_ref[...], wc_ref[...])
    ms = jnp.mean(acc * acc, axis=-1, keepdims=True)
    hn = (acc * lax.rsqrt(ms + NORM_EPS) * ng_ref[...]).astype(BF16)
    gate = _sigmoid(_dot(hn, wg_ref[...]))
    out = acc + gate * _dot(p_ref[...].astype(BF16), wp_ref[...])
    if final:
        ms = jnp.mean(out * out, axis=-1, keepdims=True)
        out = out * lax.rsqrt(ms + NORM_EPS) * fg_ref[...]
    o_ref[...] = out


def _out_proj(x2d, ya, yb, yc, p3d, layer, wa, wb, wc, ng, wg, wp, fg, final, tm=512):
    n, d = x2d.shape
    row = lambda w: pl.BlockSpec((tm, w), lambda i: (i, 0))
    full = lambda a: pl.BlockSpec(a.shape, lambda i: (0, 0))
    return pl.pallas_call(
        functools.partial(_out_proj_kernel, final=final),
        grid=(n // tm,),
        in_specs=[row(d), row(ya.shape[1]), row(yb.shape[1]), row(yc.shape[1]),
                  pl.BlockSpec((None, tm, p3d.shape[2]), lambda i: (layer, i, 0)),
                  full(wa), full(wb), full(wc), full(ng), full(wg), full(wp), full(fg)],
        out_specs=row(d),
        out_shape=jax.ShapeDtypeStruct((n, d), F32),
        compiler_params=pltpu.CompilerParams(
            dimension_semantics=("parallel",), vmem_limit_bytes=VMEM_LIMIT),
        name="out_proj",
    )(x2d, ya, yb, yc, p3d, wa, wb, wc, ng, wg, wp, fg)


def _unit_lower_inverse(n_list, row, col):
    c = n_list[0].shape[0]
    eye = jnp.where(row == col, 1.0, 0.0).astype(F32)
    same = (row // INV_BASE) == (col // INV_BASE)
    pw = [jnp.where(same, n, 0.0) for n in n_list]
    t = [eye - x for x in pw]
    span = 2
    while span < INV_BASE:
        pwb = [x.astype(BF16) for x in pw]
        pw = [_dot(x, x) for x in pwb]
        t = [x + _dot(x.astype(BF16), y.astype(BF16)) for x, y in zip(t, pw)]
        span *= 2
    size = INV_BASE
    while size < c:
        lower_left = ((row // (2 * size)) == (col // (2 * size))) & ((row // size) != (col // size))
        tb = [x.astype(BF16) for x in t]
        et = [_dot(jnp.where(lower_left, n, 0.0).astype(BF16), x).astype(BF16) for n, x in zip(n_list, tb)]
        t = [x - _dot(xb, y) for x, xb, y in zip(t, tb, et)]
        size *= 2
    return t


def _rwkv_kernel(zr_ref, zk_ref, zv_ref, zg_ref, zwa_ref, mu_ref, muwa_ref, w0_ref, wup_ref, a0_ref,
                 aup_ref, kk_ref, ka_ref, rk_ref, lng_ref, lnb_ref, o_ref,
                 state_ref, prev_ref, prevwa_ref, y_ref):
    nb, c = zr_ref.shape[0], zr_ref.shape[1]
    width = RWKV_WIDTH

    @pl.when(pl.program_id(1) == 0)
    def _():
        state_ref[...] = jnp.zeros_like(state_ref)
        prev_ref[...] = jnp.zeros_like(prev_ref)
        prevwa_ref[...] = jnp.zeros_like(prevwa_ref)

    first_row = lax.broadcasted_iota(I32, (c, width), 0) == 0
    first_row_wa = lax.broadcasted_iota(I32, (c, LANES), 0) == 0
    hrow = lax.broadcasted_iota(I32, (width, width), 0) // HEAD_DIM
    hcol = lax.broadcasted_iota(I32, (width, width), 1) // HEAD_DIM
    head_ones = jnp.where(hrow == hcol, 1.0, 0.0).astype(BF16)
    row = lax.broadcasted_iota(I32, (c, c), 0)
    col = lax.broadcasted_iota(I32, (c, c), 1)
    tri = jnp.where(col <= row, 1.0, 0.0).astype(BF16)
    strict = col < row
    incl = col <= row
    hs = [slice(h * HEAD_DIM, (h + 1) * HEAD_DIM) for h in range(RWKV_HEADS)]

    def prepare(bi):
        def shifted(z_ref, idx):
            z = z_ref[bi]
            prev = jnp.where(first_row, prev_ref[bi, idx, 0:1, :], pltpu.roll(z, 1, 0))
            prev_ref[bi, idx, 0:1, :] = z[c - 1:c, :]
            return z + (prev - z) * mu_ref[idx:idx + 1, :]

        r = shifted(zr_ref, 0)
        k = shifted(zk_ref, 1)
        v = shifted(zv_ref, 2)
        g = shifted(zg_ref, 3)
        zwa = zwa_ref[bi]
        prev_wa = jnp.where(first_row_wa, prevwa_ref[bi, 0:1, :], pltpu.roll(zwa, 1, 0))
        prevwa_ref[bi, 0:1, :] = zwa[c - 1:c, :]
        wa = zwa + (prev_wa - zwa) * muwa_ref[...]
        wd = wa[:, :LORA_RANK]
        ad = wa[:, LORA_RANK:]

        lw = -DECAY_SCALE * _sigmoid(w0_ref[...] + _dot_x3(jnp.tanh(wd), wup_ref[...]))
        a = _sigmoid(a0_ref[...] + _dot_x3(ad, aup_ref[...]))
        kk = k * kk_ref[...]
        norm = jnp.sqrt(_head_sum(kk * kk, head_ones))
        kk = kk / jnp.maximum(norm, 1e-12)
        kmod = k * (1.0 + (a - 1.0) * ka_ref[...])
        b = a * kk

        lw_hi, lw_lo = _split(lw)
        cl = _dot(tri, lw_hi) + _dot(tri, lw_lo)
        c_mid = cl[c // 2 - 1:c // 2, :]
        c_end = cl[c - 1:c, :]
        w = jnp.exp(lw)
        e_in1 = jnp.exp(cl - lw)
        e_in2 = e_in1 * w
        e_k = jnp.exp(c_mid - cl)
        e_mid_inv = jnp.exp(-c_mid)
        e_end = e_k * jnp.exp(c_end - c_mid)
        ops = dict(
            q1=kk * e_in1 * e_mid_inv, q2=r * e_in2 * e_mid_inv, k1=b * e_k, k2=kmod * e_k,
            kkd=kk * e_in1, rd=r * e_in2, bd=b * e_end, kd=kmod * e_end, vb=v)
        ops = {name: [x.astype(BF16)[:, s] for s in hs] for name, x in ops.items()}
        return dict(ops, r=r, kmod=kmod, v=v, g=g, d_end=jnp.exp(c_end))

    prep = [prepare(bi) for bi in range(nb)]

    probs = [(bi, h) for bi in range(nb) for h in range(RWKV_HEADS)]
    op = lambda name: [prep[bi][name][h] for bi, h in probs]
    q1, q2, k1, k2, kkd, rd, bd, kd, vh = map(op, ("q1", "q2", "k1", "k2", "kkd", "rd", "bd", "kd", "vb"))
    n = range(len(probs))
    a_ab = [jnp.where(strict, _dot_nt(q1[i], k1[i]), 0.0) for i in n]
    t_inv = _unit_lower_inverse(a_ab, row, col)
    a_ak = [jnp.where(strict, _dot_nt(q1[i], k2[i]), 0.0).astype(BF16) for i in n]
    a_rb = [jnp.where(incl, _dot_nt(q2[i], k1[i]), 0.0).astype(BF16) for i in n]
    a_rk = [jnp.where(incl, _dot_nt(q2[i], k2[i]), 0.0).astype(BF16) for i in n]
    s_old = [state_ref[bi, h] for bi, h in probs]
    s_in = [s.astype(BF16) for s in s_old]
    rhs = [_dot_nt(kkd[i], s_in[i]) + _dot(a_ak[i], vh[i]) for i in n]
    u = [-_dot(t_inv[i].astype(BF16), rhs[i].astype(BF16)) for i in n]
    uv_t = [jnp.concatenate([u[i], prep[bi]["v"][:, hs[h]]], axis=1).T.astype(BF16)
            for i, (bi, h) in enumerate(probs)]
    for i, (bi, h) in enumerate(probs):
        state_ref[bi, h] = (s_old[i] * prep[bi]["d_end"][:, hs[h]]
                            + _dot(uv_t[i][:HEAD_DIM, :], bd[i]) + _dot(uv_t[i][HEAD_DIM:, :], kd[i]))
    for i, (bi, h) in enumerate(probs):
        y_ref[bi, :, hs[h]] = (_dot_nt(rd[i], s_in[i]) + _dot(a_rb[i], u[i].astype(BF16))
                               + _dot(a_rk[i], vh[i]))

    inv_n = 1.0 / HEAD_DIM
    for bi in range(nb):
        pb = prep[bi]
        y = y_ref[bi]
        mean = _head_sum(y, head_ones) * inv_n
        dlt = y - mean
        var = _head_sum(dlt * dlt, head_ones) * inv_n
        yn = dlt * lax.rsqrt(var + GN_EPS) * lng_ref[...] + lnb_ref[...]
        bonus = _head_sum(pb["r"] * pb["kmod"] * rk_ref[...], head_ones) * pb["v"]
        g = pb["g"]
        o_ref[bi] = ((yn + bonus) * (g * _sigmoid(g))).astype(o_ref.dtype)


def _rwkv(z3d, mu4, mu_wa, w0, w_up, a0, a_up, k_k, k_a, r_k, ln_g, ln_b):
    batch, seq, _ = z3d.shape
    c = RWKV_CHUNK
    nb = RWKV_BATCH_ROWS if batch % RWKV_BATCH_ROWS == 0 else 1
    zspec = lambda u: pl.BlockSpec((nb, c, RWKV_WIDTH), lambda b, i, u=u: (b, i, u))
    full = lambda a: pl.BlockSpec(a.shape, lambda b, i: (0,) * a.ndim)
    params = (mu4, mu_wa, w0, w_up, a0, a_up, k_k, k_a, r_k, ln_g, ln_b)
    return pl.pallas_call(
        _rwkv_kernel,
        grid=(batch // nb, seq // c),
        in_specs=[zspec(0), zspec(1), zspec(2), zspec(3),
                  pl.BlockSpec((nb, c, LANES), lambda b, i: (b, i, U_RW_WA))]
                 + [full(a) for a in params],
        out_specs=pl.BlockSpec((nb, c, RWKV_WIDTH), lambda b, i: (b, i, 0)),
        out_shape=jax.ShapeDtypeStruct((batch, seq, RWKV_WIDTH), BF16),
        scratch_shapes=[
            pltpu.VMEM((nb, RWKV_HEADS, HEAD_DIM, HEAD_DIM), F32),
            pltpu.VMEM((nb, 4, 8, RWKV_WIDTH), F32),
            pltpu.VMEM((nb, 8, LANES), F32),
            pltpu.VMEM((nb, c, RWKV_WIDTH), F32),
        ],
        compiler_params=pltpu.CompilerParams(
            dimension_semantics=("parallel", "arbitrary"), vmem_limit_bytes=VMEM_LIMIT),
        name="rwkv7",
    )(z3d, z3d, z3d, z3d, z3d, *params)


def _dsa_kernel(*refs):
    seq = refs[3].shape[0]
    lax.fori_loop(0, seq // ATT_BLOCK, lambda qb, carry: (_dsa_query_block(qb, *refs), carry)[1], 0)


def _dsa_query_block(qb, zq_ref, zg_ref, ziq_ref, zkv_ref, zms_ref, c64_ref, s64_ref, c32_ref, s32_ref, o_ref,
                     k_ref, vt_ref, ik_ref, key_ref):
    seq = zkv_ref.shape[0]
    blk = ATT_BLOCK
    kb_rows = DSA_KEY_BLOCK
    n_kb = (qb * blk) // kb_rows + 1

    @pl.when(qb == 0)
    def _():
        lane = lax.broadcasted_iota(I32, (kb_rows, LANES), 1)

        def body(t, carry):
            rows = pl.ds(pl.multiple_of(t * kb_rows, kb_rows), kb_rows)
            kv = zkv_ref[rows, :]
            kvr = jnp.where(lane < HEAD_DIM, _rope(kv, c64_ref[rows, :], s64_ref[rows, :], 8, HEAD_DIM), kv)
            k_ref[rows, :] = kvr[:, :HEAD_DIM].astype(BF16)
            vt_ref[t] = kvr.T[HEAD_DIM:, :].astype(BF16)
            ms = zms_ref[rows, :]
            ikr = _rope(ms, c32_ref[rows, :], s32_ref[rows, :], 4, IDX_DIM)
            ik_ref[rows, :] = ikr[:, :IDX_DIM].astype(BF16)
            return carry

        lax.fori_loop(0, seq // kb_rows, body, 0)

    r0 = pl.multiple_of(qb * blk, blk)
    qrows = pl.ds(r0, blk)
    c64 = c64_ref[qrows, :]
    s64 = s64_ref[qrows, :]
    c32 = c32_ref[qrows, :]
    s32 = s32_ref[qrows, :]
    q2 = [_rope(zq_ref[qrows, i * LANES:(i + 1) * LANES], c64, s64, 8, HEAD_DIM) * QK_SCALE_LOG2
          for i in range(2)]
    q_stack = jnp.concatenate(
        [q2[h // 2][:, (h % 2) * HEAD_DIM:(h % 2 + 1) * HEAD_DIM] for h in range(DSA_HEADS)],
        axis=0).astype(BF16)
    iq2 = [_rope(ziq_ref[qrows, i * LANES:(i + 1) * LANES], c32, s32, 4, IDX_DIM) for i in range(2)]
    iq_stack = jnp.concatenate(
        [iq2[h // 4][:, (h % 4) * IDX_DIM:(h % 4 + 1) * IDX_DIM] for h in range(IDX_HEADS)],
        axis=0).astype(BF16)
    iw_t = zms_ref[qrows, :].T[IDX_DIM:IDX_DIM + IDX_HEADS, :] * ((IDX_HEADS * IDX_DIM) ** -0.5)
    iw_row = jnp.concatenate([iw_t[h:h + 1, :] for h in range(IDX_HEADS)], axis=1)

    kpos_l = lax.broadcasted_iota(I32, (kb_rows, blk), 0)
    qpos = r0 + lax.broadcasted_iota(I32, (kb_rows, blk), 1)

    tri = jnp.where(lax.broadcasted_iota(I32, (blk, blk), 1)
                    <= lax.broadcasted_iota(I32, (blk, blk), 0), 1.0, 0.0).astype(BF16)
    g = zg_ref[qrows, :]
    gate = g * _sigmoid(g)

    def step(n):
        rows = [slice(kb * kb_rows, (kb + 1) * kb_rows) for kb in range(n)]
        causal = (n - 1) * kb_rows + kpos_l <= qpos

        for kb in range(n):
            s = jnp.maximum(_dot_nt(ik_ref[rows[kb], :], iq_stack), 0.0) * iw_row
            sc = s[:, 0:blk]
            for h in range(1, IDX_HEADS):
                sc = sc + s[:, h * blk:(h + 1) * blk]
            if kb == n - 1:
                sc = jnp.where(causal, sc, -jnp.inf)
            bits = pltpu.bitcast(sc, I32)
            key_ref[rows[kb], :] = bits ^ ((bits >> 31) & 0x7FFFFFFF)

        def count(pred_fn):
            acc = jnp.zeros((8, blk), I32)
            for kb in range(n):
                hit = jnp.where(pred_fn(key_ref[rows[kb], :]), 1, 0).astype(I32)
                acc = acc + hit.reshape(kb_rows // 8, 8, blk).sum(axis=0)
            return acc.sum(axis=0, keepdims=True)

        def bit_body(i, lo):
            cand = lo + (jnp.int32(1) << (31 - i))
            return jnp.where(count(lambda keys: keys >= cand) >= DSA_TOPK, cand, lo)

        lo0 = jnp.full((1, blk), INT_MIN, I32)
        search = lambda: lax.fori_loop(0, 32, bit_body, lo0)
        if n * kb_rows - blk + 1 > DSA_TOPK:
            thr = search()
        else:
            thr = lax.cond(qb * blk >= DSA_TOPK, search, lambda: lo0)
        need = (DSA_TOPK - count(lambda keys: keys > thr)).astype(F32)

        wide = DSA_HEADS * blk
        m = jnp.full((1, wide), NEG_BIG, F32)
        l = jnp.zeros((1, wide), F32)
        acc = jnp.zeros((HEAD_DIM, wide), F32)
        run = jnp.zeros((1, blk), F32)
        for kb in range(n):
            keys = key_ref[rows[kb], :]
            eq = keys == thr
            eq_b = jnp.where(eq, 1.0, 0.0).astype(BF16)
            local = [_dot(tri, eq_b[sub * blk:(sub + 1) * blk, :]) for sub in range(kb_rows // blk)]
            ranks = []
            for x in local:
                ranks.append(x + run)
                run = run + x[blk - 1:blk, :]
            rank = jnp.concatenate(ranks, axis=0)
            take = (keys > thr) | (eq & (rank <= need))
            if kb == n - 1:
                take = take & causal
            bias = jnp.where(take, 0.0, NEG_BIG)
            s = _dot_nt(k_ref[rows[kb], :], q_stack)
            s = s + jnp.concatenate([bias] * DSA_HEADS, axis=1)
            m_new = jnp.maximum(m, jnp.max(s, axis=0, keepdims=True))
            p = jnp.exp2(s - m_new)
            alpha = jnp.exp2(m - m_new)
            l = alpha * l + jnp.sum(p, axis=0, keepdims=True)
            acc = alpha * acc + _dot(vt_ref[kb], p.astype(BF16))
            m = m_new

        o_t = acc / l
        o = jnp.concatenate(
            [jnp.concatenate([o_t[:, (2 * i) * blk:(2 * i + 1) * blk], o_t[:, (2 * i + 1) * blk:(2 * i + 2) * blk]],
                             axis=0).T for i in range(DSA_HEADS // 2)], axis=1)
        o_ref[qrows, :] = (o * gate).astype(o_ref.dtype)

    for n in range(1, seq // kb_rows + 1):
        pl.when(n_kb == n)(functools.partial(step, n))


def _dsa(z2d, batch, seq, c64, s64, c32, s32):
    blk = ATT_BLOCK
    qspec = lambda u2: pl.BlockSpec((seq, 2 * LANES), lambda b, u2=u2: (b, u2))
    seqspec = lambda u: pl.BlockSpec((seq, LANES), lambda b, u=u: (b, u))
    tab = pl.BlockSpec((seq, LANES), lambda b: (0, 0))
    return pl.pallas_call(
        _dsa_kernel,
        grid=(batch,),
        in_specs=[qspec(U_DS_Q // 2), qspec(U_DS_G // 2), qspec(U_DS_IQ // 2),
                  seqspec(U_DS_KV), seqspec(U_DS_MISC), tab, tab, tab, tab],
        out_specs=pl.BlockSpec((seq, DSA_WIDTH), lambda b: (b, 0)),
        out_shape=jax.ShapeDtypeStruct((batch * seq, DSA_WIDTH), BF16),
        scratch_shapes=[
            pltpu.VMEM((seq, HEAD_DIM), BF16),
            pltpu.VMEM((seq // DSA_KEY_BLOCK, HEAD_DIM, DSA_KEY_BLOCK), BF16),
            pltpu.VMEM((seq, IDX_DIM), BF16),
            pltpu.VMEM((seq, blk), I32),
        ],
        compiler_params=pltpu.CompilerParams(
            dimension_semantics=("parallel",), vmem_limit_bytes=VMEM_LIMIT),
        name="dsa",
    )(z2d, z2d, z2d, z2d, z2d, c64, s64, c32, s32)


def _dil_group(gi, rate, q_ref, k_ref, v_ref, g_ref, c64_ref, s64_ref, qr_ref, kr_ref, o_s, lse_s):
    seq = q_ref.shape[0]
    blk = ATT_BLOCK
    rope_rows = 4 * blk
    nrow = seq // rope_rows

    def rope_body(t, carry):
        rows = pl.ds(pl.multiple_of(t * rope_rows, rope_rows), rope_rows)
        ct = c64_ref[rows, :]
        st = s64_ref[rows, :]
        q = q_ref[rows, :]
        k = k_ref[rows, :]
        qr_ref[rows, :] = (q * ct + pltpu.roll(q, HEAD_DIM, 1) * st) * QK_SCALE_LOG2
        kr_ref[rows, :] = k * ct + pltpu.roll(k, HEAD_DIM, 1) * st
        return carry

    lax.fori_loop(0, nrow, rope_body, 0)

    nblk = seq // (blk * rate)
    qi = lax.broadcasted_iota(I32, (blk, 2 * blk), 0)
    kj = lax.broadcasted_iota(I32, (blk, 2 * blk), 1)
    band = (kj >= qi) & (kj <= qi + DIL_STEPS)

    def strided(ref, start):
        if rate == 1:
            return ref[pl.ds(start, blk), :]
        return ref[pl.ds(start, blk, stride=rate), :]

    def store(ref, start, val):
        if rate == 1:
            ref[gi, pl.ds(start, blk), :] = val
        else:
            ref[gi, pl.ds(start, blk, stride=rate), :] = val

    first_q = lax.broadcasted_iota(I32, (blk, LANES), 1) < HEAD_DIM
    first_kv = lax.broadcasted_iota(I32, (2 * blk, LANES), 1) < HEAD_DIM
    lane64 = lax.broadcasted_iota(I32, (blk, LANES), 1) % HEAD_DIM
    first_qk = (lane64 < 8) | ((lane64 >= 16) & (lane64 < 40))
    unroll = DIL_UNROLL

    def body(it, carry):
        starts, qs, ks, vs, masks = [], [], [], [], []
        for u in range(unroll):
            cidx = it * unroll + u
            rho = cidx // nblk
            j = cidx % nblk
            start = j * (blk * rate) + rho
            pstart = jnp.maximum(j - 1, 0) * (blk * rate) + rho
            starts.append(start)
            qs.append(strided(qr_ref, start))
            ks.append(jnp.concatenate([strided(kr_ref, pstart), strided(kr_ref, start)], axis=0).astype(BF16))
            vs.append(jnp.concatenate([strided(v_ref, pstart), strided(v_ref, start)], axis=0))
            masks.append(band & ((kj >= blk) | (j > 0)))
        prob = [(u, h) for u in range(unroll) for h in range(2)]
        qh = [jnp.where(first_qk == (h == 0), qs[u], 0.0).astype(BF16) for u, h in prob]
        vh = [jnp.where(first_kv == (h == 0), vs[u], 1.0).astype(BF16) for u, h in prob]
        s = [jnp.where(masks[u], _dot_nt(qh[i], ks[u]), -jnp.inf) for i, (u, h) in enumerate(prob)]
        m = [jnp.max(x, axis=-1, keepdims=True) for x in s]
        r = [_dot(jnp.exp2(s[i] - m[i]).astype(BF16), vh[i]) for i in range(len(prob))]
        for u in range(unroll):
            r0, r1 = r[2 * u], r[2 * u + 1]
            o = jnp.where(first_q, r0, r1)
            l = pltpu.roll(jnp.where(first_q, r1, r0), HEAD_DIM, 1)
            lse = jnp.where(first_q, m[2 * u], m[2 * u + 1]) * math.log(2.0) + jnp.log(l)
            gate = strided(g_ref, starts[u])
            store(o_s, starts[u], o / l * (gate * _sigmoid(gate)))
            store(lse_s, starts[u], lse)
        return carry

    lax.fori_loop(0, rate * nblk // unroll, body, 0)


def _dil_kernel(q_ref, k_ref, v_ref, g_ref, c64_ref, s64_ref, o_ref, qr_ref, kr_ref, o_s, lse_s):
    grp = pl.program_id(1)
    for gi, rate in enumerate(DIL_RATES):
        pl.when(grp == gi)(functools.partial(
            _dil_group, gi, rate, q_ref, k_ref, v_ref, g_ref, c64_ref, s64_ref, qr_ref, kr_ref, o_s, lse_s))

    @pl.when(grp == len(DIL_RATES) - 1)
    def _():
        blk = ATT_BLOCK

        def body(t, carry):
            rows = pl.ds(pl.multiple_of(t * blk, blk), blk)
            ls = [lse_s[i, rows, :] for i in range(3)]
            top = jnp.maximum(jnp.maximum(ls[0], ls[1]), ls[2])
            es = [jnp.exp(x - top) for x in ls]
            inv = 1.0 / (es[0] + es[1] + es[2])
            o_ref[rows, :] = jnp.concatenate(
                [o_s[i, rows, :] * (es[i] * inv) for i in range(3)], axis=1).astype(o_ref.dtype)
            return carry

        lax.fori_loop(0, q_ref.shape[0] // blk, body, 0)


def _dilated(z2d, batch, seq, c64, s64):
    ng = len(DIL_RATES)
    zspec = lambda base: pl.BlockSpec((seq, LANES), lambda b, g, base=base: (b, base + g))
    tab = pl.BlockSpec((seq, LANES), lambda b, g: (0, 0))
    return pl.pallas_call(
        _dil_kernel,
        grid=(batch, ng),
        in_specs=[zspec(U_DL), zspec(U_DL + 3), zspec(U_DL + 6), zspec(U_DL + 9), tab, tab],
        out_specs=pl.BlockSpec((seq, DIL_WIDTH), lambda b, g: (b, 0)),
        out_shape=jax.ShapeDtypeStruct((batch * seq, DIL_WIDTH), BF16),
        scratch_shapes=[
            pltpu.VMEM((seq, LANES), F32),
            pltpu.VMEM((seq, LANES), F32),
            pltpu.VMEM((ng, seq, LANES), F32),
            pltpu.VMEM((ng, seq, LANES), F32),
        ],
        compiler_params=pltpu.CompilerParams(
            dimension_semantics=("parallel", "arbitrary"), vmem_limit_bytes=VMEM_LIMIT),
        name="dilated",
    )(z2d, z2d, z2d, z2d, c64, s64)


def _rope_tables(seq, head, half):
    inv = ROPE_THETA ** (-jnp.arange(half, dtype=F32) / half)
    ang = jnp.arange(seq, dtype=F32)[:, None] * inv[None, :]
    cos, sin = jnp.cos(ang), jnp.sin(ang)
    lane = jnp.arange(LANES) % head
    idx = lane % half
    cos_t = jnp.where(lane[None, :] < 2 * half, cos[:, idx], 1.0)
    sin_t = jnp.where(lane[None, :] < half, -sin[:, idx], jnp.where(lane[None, :] < 2 * half, sin[:, idx], 0.0))
    return cos_t, sin_t


def _dil_pair_lanes():
    half, rest = HEAD_DIM // 8, (HEAD_DIM - HEAD_DIM // 4) // 2
    src = []
    for part in range(2):
        for head in range(2):
            src += [head * HEAD_DIM + part * half + i for i in range(half)]
        for head in range(2):
            src += [head * HEAD_DIM + 2 * half + part * rest + i for i in range(rest)]
    return src


def _rope_tables_paired(seq):
    half = HEAD_DIM // 8
    inv = ROPE_THETA ** (-jnp.arange(half, dtype=F32) / half)
    ang = jnp.arange(seq, dtype=F32)[:, None] * inv[None, :]
    cos, sin = jnp.cos(ang), jnp.sin(ang)
    lane = jnp.arange(LANES)[None, :]
    idx = jnp.arange(LANES) % half
    rotated = lane % HEAD_DIM < 2 * half
    cos_t = jnp.where(rotated, cos[:, idx], 1.0)
    sin_t = jnp.where(rotated, jnp.where(lane < HEAD_DIM, -sin[:, idx], sin[:, idx]), 0.0)
    return cos_t, sin_t


def _permute_in_cols(w):
    rw = RWKV_WIDTH
    o = 0
    r, k, v, g = (w[..., o + i * rw:o + (i + 1) * rw] for i in range(4))
    o = 4 * rw
    wd, ad = w[..., o:o + LORA_RANK], w[..., o + LORA_RANK:o + 2 * LORA_RANK]
    o += 2 * LORA_RANK
    widths = (DSA_WIDTH, HEAD_DIM, HEAD_DIM, IDX_HEADS * IDX_DIM, IDX_DIM, IDX_HEADS, DSA_WIDTH)
    parts = []
    for wdt in widths:
        parts.append(w[..., o:o + wdt])
        o += wdt
    dq, dk, dv, diq, dik, diw, dg = parts
    cq, ck, cv, cg = (w[..., o + i * DIL_WIDTH:o + (i + 1) * DIL_WIDTH] for i in range(4))
    pair = jnp.asarray([t * LANES + s for t in range(DIL_WIDTH // LANES) for s in _dil_pair_lanes()])
    cq, ck = cq[..., pair], ck[..., pair]
    pad = jnp.zeros(w.shape[:-1] + (LANES - IDX_DIM - IDX_HEADS,), w.dtype)
    return jnp.concatenate([r, k, v, g, cq, ck, cv, cg, dq, dg, diq, dk, dv, dik, diw, pad, wd, ad], axis=-1)


def kernel(x, p, norm_g, w_in, tshift_mu, rwkv_w0, rwkv_w_up, rwkv_a0, rwkv_a_up, rwkv_k_k, rwkv_k_a, rwkv_r_k,
           rwkv_ln_g, rwkv_ln_b, w_out, ple_norm_g, ple_w_gate, ple_w_proj, final_norm_g):
    batch, seq, d_model = x.shape
    depth = w_in.shape[0]
    n = batch * seq
    assert seq % (ATT_BLOCK * max(DIL_RATES)) == 0 and seq >= 4 * DSA_TOPK

    c64, s64 = _rope_tables(seq, HEAD_DIM, HEAD_DIM // 8)
    c32, s32 = _rope_tables(seq, IDX_DIM, IDX_DIM // 8)
    c64p, s64p = _rope_tables_paired(seq)
    w_in_k = _permute_in_cols(w_in.astype(BF16))
    rw = RWKV_WIDTH
    row = lambda a: a.reshape(1, -1)

    x2d = x.reshape(n, d_model)
    p3d = p.reshape(depth, n, p.shape[-1])
    for i in range(depth):
        z = _in_proj(x2d, row(norm_g[i]), w_in_k, i)
        mu = tshift_mu[i]
        y_a = _rwkv(z.reshape(batch, seq, Z_COLS), mu[:4 * rw].reshape(4, rw), row(mu[4 * rw:]), row(rwkv_w0[i]), rwkv_w_up[i],
                    row(rwkv_a0[i]), rwkv_a_up[i], row(rwkv_k_k[i]), row(rwkv_k_a[i]), row(rwkv_r_k[i]),
                    row(rwkv_ln_g[i]), row(rwkv_ln_b[i])).reshape(n, rw)
        y_b = _dsa(z, batch, seq, c64, s64, c32, s32)
        y_c = _dilated(z, batch, seq, c64p, s64p)
        wo = w_out[i].astype(BF16)
        x2d = _out_proj(x2d, y_a, y_b, y_c, p3d, i, wo[:rw], wo[rw:rw + DSA_WIDTH],
                        wo[rw + DSA_WIDTH:], row(ple_norm_g[i]), ple_w_gate[i].astype(BF16),
                        ple_w_proj[i].astype(BF16), row(final_norm_g), final=(i == depth - 1))
    return x2d.reshape(batch, seq, d_model)
```

```python
import functools
import math

import jax
import jax.numpy as jnp
from jax import lax
from jax.experimental import pallas as pl
from jax.experimental.pallas import tpu as pltpu

F32 = jnp.float32
BF16 = jnp.bfloat16
I32 = jnp.int32

HEAD_DIM = 64
ROPE_THETA = 500000.0
NORM_EPS = 1e-6
RWKV_HEADS = 6
RWKV_WIDTH = RWKV_HEADS * HEAD_DIM
LORA_RANK = 64
GN_EPS = 64e-5
DECAY_SCALE = math.exp(-0.5)
DSA_HEADS = 4
DSA_WIDTH = DSA_HEADS * HEAD_DIM
IDX_HEADS = 8
IDX_DIM = 32
DSA_TOPK = 256
DIL_RATES = (1, 4, 16)
DIL_STEPS = 128
DIL_UNROLL = 4
DIL_WIDTH = 384
ATT_BLOCK = 128
DSA_KEY_BLOCK = 512

LANES = 128
MXU_WIDTH = 256
VMEM_LIMIT = 48 * 1024 * 1024

U_RW = 0
U_DL = 12
U_DS_Q = 24
U_DS_G = 26
U_DS_IQ = 28
U_DS_KV = 30
U_DS_MISC = 31
U_RW_WA = 32
Z_UNITS = 33
Z_COLS = Z_UNITS * LANES

RWKV_CHUNK = 128
RWKV_BATCH_ROWS = 4
INV_BASE = 16
QK_SCALE_LOG2 = HEAD_DIM ** -0.5 * math.log2(math.e)
NEG_BIG = -1e30
BIT_GROUP_ROWS = 256
INT_MIN = -2 ** 31


def _dot(a, b):
    return jnp.dot(a, b, preferred_element_type=F32)


def _dot_nt(a, b):
    return lax.dot_general(a, b, (((1,), (1,)), ((), ())), preferred_element_type=F32)


def _split(x):
    hi = x.astype(BF16)
    lo = (x - hi.astype(F32)).astype(BF16)
    return hi, lo


def _head_sum(x, head_ones):
    return _dot(x.astype(BF16), head_ones)


def _dot_x3(x, w):
    xh, xl = _split(x)
    wh, wl = _split(w)
    return _dot(xh, wh) + (_dot(xh, wl) + _dot(xl, wh))


def _sigmoid(x):
    return 1.0 / (1.0 + jnp.exp(-x))


def _rope(x, cos_t, sin_t, half, head):
    lane = lax.broadcasted_iota(I32, x.shape, 1) % head
    partner = jnp.where(lane < half, pltpu.roll(x, LANES - half, 1), pltpu.roll(x, half, 1))
    return x * cos_t + partner * sin_t


def _in_proj_kernel(x_ref, g_ref, w_ref, z_ref, *, tn):
    x = x_ref[...]
    ms = jnp.mean(x * x, axis=-1, keepdims=True)
    h = (x * lax.rsqrt(ms + NORM_EPS) * g_ref[...]).astype(BF16)
    for c in range(0, Z_COLS, tn):
        wdt = min(tn, Z_COLS - c)
        z_ref[:, c:c + wdt] = _dot(h, w_ref[:, c:c + wdt])


def _in_proj(x2d, g, w_all, layer, tm=512, tn=MXU_WIDTH):
    n, d = x2d.shape
    return pl.pallas_call(
        functools.partial(_in_proj_kernel, tn=tn),
        grid=(n // tm,),
        in_specs=[
            pl.BlockSpec((tm, d), lambda i: (i, 0)),
            pl.BlockSpec((1, d), lambda i: (0, 0)),
            pl.BlockSpec((None, d, Z_COLS), lambda i: (layer, 0, 0)),
        ],
        out_specs=pl.BlockSpec((tm, Z_COLS), lambda i: (i, 0)),
        out_shape=jax.ShapeDtypeStruct((n, Z_COLS), F32),
        compiler_params=pltpu.CompilerParams(
            dimension_semantics=("parallel",), vmem_limit_bytes=VMEM_LIMIT),
        name="in_proj",
    )(x2d, g, w_all)


def _out_proj_kernel(x_ref, ya_ref, yb_ref, yc_ref, p_ref, wa_ref, wb_ref, wc_ref, ng_ref, wg_ref,
                     wp_ref, fg_ref, o_ref, *, final):
    acc = x_ref[...] + _dot(ya_ref[...], wa_ref[...])
    acc = acc + _dot(yb_ref[...], wb_ref[...])
    acc = acc + _dot(yc_ref[...], wc_ref[...])
    ms = jnp.mean(acc * acc, axis=-1, keepdims=True)
    hn = (acc * lax.rsqrt(ms + NORM_EPS) * ng_ref[...]).astype(BF16)
    gate = _sigmoid(_dot(hn, wg_ref[...]))
    out = acc + gate * _dot(p_ref[...].astype(BF16), wp_ref[...])
    if final:
        ms = jnp.mean(out * out, axis=-1, keepdims=True)
        out = out * lax.rsqrt(ms + NORM_EPS) * fg_ref[...]
    o_ref[...] = out


def _out_proj(x2d, ya, yb, yc, p3d, layer, wa, wb, wc, ng, wg, wp, fg, final, tm=512):
    n, d = x2d.shape
    row = lambda w: pl.BlockSpec((tm, w), lambda i: (i, 0))
    full = lambda a: pl.BlockSpec(a.shape, lambda i: (0, 0))
    return pl.pallas_call(
        functools.partial(_out_proj_kernel, final=final),
        grid=(n // tm,),
        in_specs=[row(d), row(ya.shape[1]), row(yb.shape[1]), row(yc.shape[1]),
                  pl.BlockSpec((None, tm, p3d.shape[2]), lambda i: (layer, i, 0)),
                  full(wa), full(wb), full(wc), full(ng), full(wg), full(wp), full(fg)],
        out_specs=row(d),
        out_shape=jax.ShapeDtypeStruct((n, d), F32),
        compiler_params=pltpu.CompilerParams(
            dimension_semantics=("parallel",), vmem_limit_bytes=VMEM_LIMIT),
        name="out_proj",
    )(x2d, ya, yb, yc, p3d, wa, wb, wc, ng, wg, wp, fg)


def _unit_lower_inverse(n_list, row, col):
    c = n_list[0].shape[0]
    eye = jnp.where(row == col, 1.0, 0.0).astype(F32)
    same = (row // INV_BASE) == (col // INV_BASE)
    pw = [jnp.where(same, n, 0.0) for n in n_list]
    t = [eye - x for x in pw]
    span = 2
    while span < INV_BASE:
        pwb = [x.astype(BF16) for x in pw]
        pw = [_dot(x, x) for x in pwb]
        t = [x + _dot(x.astype(BF16), y.astype(BF16)) for x, y in zip(t, pw)]
        span *= 2
    size = INV_BASE
    while size < c:
        lower_left = ((row // (2 * size)) == (col // (2 * size))) & ((row // size) != (col // size))
        tb = [x.astype(BF16) for x in t]
        et = [_dot(jnp.where(lower_left, n, 0.0).astype(BF16), x).astype(BF16) for n, x in zip(n_list, tb)]
        t = [x - _dot(xb, y) for x, xb, y in zip(t, tb, et)]
        size *= 2
    return t


def _rwkv_kernel(zr_ref, zk_ref, zv_ref, zg_ref, zwa_ref, mu_ref, muwa_ref, w0_ref, wup_ref, a0_ref,
                 aup_ref, kk_ref, ka_ref, rk_ref, lng_ref, lnb_ref, o_ref,
                 state_ref, prev_ref, prevwa_ref, y_ref):
    nb, c = zr_ref.shape[0], zr_ref.shape[1]
    width = RWKV_WIDTH

    @pl.when(pl.program_id(1) == 0)
    def _():
        state_ref[...] = jnp.zeros_like(state_ref)
        prev_ref[...] = jnp.zeros_like(prev_ref)
        prevwa_ref[...] = jnp.zeros_like(prevwa_ref)

    first_row = lax.broadcasted_iota(I32, (c, width), 0) == 0
    first_row_wa = lax.broadcasted_iota(I32, (c, LANES), 0) == 0
    hrow = lax.broadcasted_iota(I32, (width, width), 0) // HEAD_DIM
    hcol = lax.broadcasted_iota(I32, (width, width), 1) // HEAD_DIM
    head_ones = jnp.where(hrow == hcol, 1.0, 0.0).astype(BF16)
    row = lax.broadcasted_iota(I32, (c, c), 0)
    col = lax.broadcasted_iota(I32, (c, c), 1)
    tri = jnp.where(col <= row, 1.0, 0.0).astype(BF16)
    strict = col < row
    incl = col <= row
    hs = [slice(h * HEAD_DIM, (h + 1) * HEAD_DIM) for h in range(RWKV_HEADS)]

    def prepare(bi):
        def shifted(z_ref, idx):
            z = z_ref[bi]
            prev = jnp.where(first_row, prev_ref[bi, idx, 0:1, :], pltpu.roll(z, 1, 0))
            prev_ref[bi, idx, 0:1, :] = z[c - 1:c, :]
            return z + (prev - z) * mu_ref[idx:idx + 1, :]

        r = shifted(zr_ref, 0)
        k = shifted(zk_ref, 1)
        v = shifted(zv_ref, 2)
        g = shifted(zg_ref, 3)
        zwa = zwa_ref[bi]
        prev_wa = jnp.where(first_row_wa, prevwa_ref[bi, 0:1, :], pltpu.roll(zwa, 1, 0))
        prevwa_ref[bi, 0:1, :] = zwa[c - 1:c, :]
        wa = zwa + (prev_wa - zwa) * muwa_ref[...]
        wd = wa[:, :LORA_RANK]
        ad = wa[:, LORA_RANK:]

        lw = -DECAY_SCALE * _sigmoid(w0_ref[...] + _dot_x3(jnp.tanh(wd), wup_ref[...]))
        a = _sigmoid(a0_ref[...] + _dot_x3(ad, aup_ref[...]))
        kk = k * kk_ref[...]
        norm = jnp.sqrt(_head_sum(kk * kk, head_ones))
        kk = kk / jnp.maximum(norm, 1e-12)
        kmod = k * (1.0 + (a - 1.0) * ka_ref[...])
        b = a * kk

        lw_hi, lw_lo = _split(lw)
        cl = _dot(tri, lw_hi) + _dot(tri, lw_lo)
        c_mid = cl[c // 2 - 1:c // 2, :]
        c_end = cl[c - 1:c, :]
        w = jnp.exp(lw)
        e_in1 = jnp.exp(cl - lw)
        e_in2 = e_in1 * w
        e_k = jnp.exp(c_mid - cl)
        e_mid_inv = jnp.exp(-c_mid)
        e_end = e_k * jnp.exp(c_end - c_mid)
        ops = dict(
            q1=kk * e_in1 * e_mid_inv, q2=r * e_in2 * e_mid_inv, k1=b * e_k, k2=kmod * e_k,
            kkd=kk * e_in1, rd=r * e_in2, bd=b * e_end, kd=kmod * e_end, vb=v)
        ops = {name: [x.astype(BF16)[:, s] for s in hs] for name, x in ops.items()}
        return dict(ops, r=r, kmod=kmod, v=v, g=g, d_end=jnp.exp(c_end))

    prep = [prepare(bi) for bi in range(nb)]

    probs = [(bi, h) for bi in range(nb) for h in range(RWKV_HEADS)]
    op = lambda name: [prep[bi][name][h] for bi, h in probs]
    q1, q2, k1, k2, kkd, rd, bd, kd, vh = map(op, ("q1", "q2", "k1", "k2", "kkd", "rd", "bd", "kd", "vb"))
    n = range(len(probs))
    a_ab = [jnp.where(strict, _dot_nt(q1[i], k1[i]), 0.0) for i in n]
    t_inv = _unit_lower_inverse(a_ab, row, col)
    a_ak = [jnp.where(strict, _dot_nt(q1[i], k2[i]), 0.0).astype(BF16) for i in n]
    a_rb = [jnp.where(incl, _dot_nt(q2[i], k1[i]), 0.0).astype(BF16) for i in n]
    a_rk = [jnp.where(incl, _dot_nt(q2[i], k2[i]), 0.0).astype(BF16) for i in n]
    s_old = [state_ref[bi, h] for bi, h in probs]
    s_in = [s.astype(BF16) for s in s_old]
    rhs = [_dot_nt(kkd[i], s_in[i]) + _dot(a_ak[i], vh[i]) for i in n]
    u = [-_dot(t_inv[i].astype(BF16), rhs[i].astype(BF16)) for i in n]
    uv_t = [jnp.concatenate([u[i], prep[bi]["v"][:, hs[h]]], axis=1).T.astype(BF16)
            for i, (bi, h) in enumerate(probs)]
    for i, (bi, h) in enumerate(probs):
        state_ref[bi, h] = (s_old[i] * prep[bi]["d_end"][:, hs[h]]
                            + _dot(uv_t[i][:HEAD_DIM, :], bd[i]) + _dot(uv_t[i][HEAD_DIM:, :], kd[i]))
    for i, (bi, h) in enumerate(probs):
        y_ref[bi, :, hs[h]] = (_dot_nt(rd[i], s_in[i]) + _dot(a_rb[i], u[i].astype(BF16))
                               + _dot(a_rk[i], vh[i]))

    inv_n = 1.0 / HEAD_DIM
    for bi in range(nb):
        pb = prep[bi]
        y = y_ref[bi]
        mean = _head_sum(y, head_ones) * inv_n
        dlt = y - mean
        var = _head_sum(dlt * dlt, head_ones) * inv_n
        yn = dlt * lax.rsqrt(var + GN_EPS) * lng_ref[...] + lnb_ref[...]
        bonus = _head_sum(pb["r"] * pb["kmod"] * rk_ref[...], head_ones) * pb["v"]
        g = pb["g"]
        o_ref[bi] = ((yn + bonus) * (g * _sigmoid(g))).astype(o_ref.dtype)


def _rwkv(z3d, mu4, mu_wa, w0, w_up, a0, a_up, k_k, k_a, r_k, ln_g, ln_b):
    batch, seq, _ = z3d.shape
    c = RWKV_CHUNK
    nb = RWKV_BATCH_ROWS if batch % RWKV_BATCH_ROWS == 0 else 1
    zspec = lambda u: pl.BlockSpec((nb, c, RWKV_WIDTH), lambda b, i, u=u: (b, i, u))
    full = lambda a: pl.BlockSpec(a.shape, lambda b, i: (0,) * a.ndim)
    params = (mu4, mu_wa, w0, w_up, a0, a_up, k_k, k_a, r_k, ln_g, ln_b)
    return pl.pallas_call(
        _rwkv_kernel,
        grid=(batch // nb, seq // c),
        in_specs=[zspec(0), zspec(1), zspec(2), zspec(3),
                  pl.BlockSpec((nb, c, LANES), lambda b, i: (b, i, U_RW_WA))]
                 + [full(a) for a in params],
        out_specs=pl.BlockSpec((nb, c, RWKV_WIDTH), lambda b, i: (b, i, 0)),
        out_shape=jax.ShapeDtypeStruct((batch, seq, RWKV_WIDTH), BF16),
        scratch_shapes=[
            pltpu.VMEM((nb, RWKV_HEADS, HEAD_DIM, HEAD_DIM), F32),
            pltpu.VMEM((nb, 4, 8, RWKV_WIDTH), F32),
            pltpu.VMEM((nb, 8, LANES), F32),
            pltpu.VMEM((nb, c, RWKV_WIDTH), F32),
        ],
        compiler_params=pltpu.CompilerParams(
            dimension_semantics=("parallel", "arbitrary"), vmem_limit_bytes=VMEM_LIMIT),
        name="rwkv7",
    )(z3d, z3d, z3d, z3d, z3d, *params)


def _bit_transpose32(words):
    a = list(words)
    j, mask = 16, 0x0000FFFF
    while j:
        for k in range(32):
            if k & j == 0:
                t = (a[k] ^ lax.shift_right_logical(a[k + j], jnp.int32(j))) & jnp.int32(mask)
                a[k] = a[k] ^ t
                a[k + j] = a[k + j] ^ (t << j)
        j >>= 1
        mask ^= mask << j
    return a


def _dsa_kernel(*refs):
    seq = refs[3].shape[0]
    lax.fori_loop(0, seq // ATT_BLOCK, lambda qb, carry: (_dsa_query_block(qb, *refs), carry)[1], 0)


def _dsa_query_block(qb, zq_ref, zg_ref, ziq_ref, zkv_ref, zms_ref, c64_ref, s64_ref, c32_ref, s32_ref, o_ref,
                     k_ref, vt_ref, ik_ref, key_ref, plane_ref):
    seq = zkv_ref.shape[0]
    blk = ATT_BLOCK
    kb_rows = DSA_KEY_BLOCK
    n_kb = (qb * blk) // kb_rows + 1

    @pl.when(qb == 0)
    def _():
        lane = lax.broadcasted_iota(I32, (kb_rows, LANES), 1)

        def body(t, carry):
            rows = pl.ds(pl.multiple_of(t * kb_rows, kb_rows), kb_rows)
            kv = zkv_ref[rows, :]
            kvr = jnp.where(lane < HEAD_DIM, _rope(kv, c64_ref[rows, :], s64_ref[rows, :], 8, HEAD_DIM), kv)
            k_ref[rows, :] = kvr[:, :HEAD_DIM].astype(BF16)
            vt_ref[t] = kvr.T[HEAD_DIM:, :].astype(BF16)
            ms = zms_ref[rows, :]
            ikr = _rope(ms, c32_ref[rows, :], s32_ref[rows, :], 4, IDX_DIM)
            ik_ref[rows, :] = ikr[:, :IDX_DIM].astype(BF16)
            return carry

        lax.fori_loop(0, seq // kb_rows, body, 0)

    r0 = pl.multiple_of(qb * blk, blk)
    qrows = pl.ds(r0, blk)
    c64 = c64_ref[qrows, :]
    s64 = s64_ref[qrows, :]
    c32 = c32_ref[qrows, :]
    s32 = s32_ref[qrows, :]
    q2 = [_rope(zq_ref[qrows, i * LANES:(i + 1) * LANES], c64, s64, 8, HEAD_DIM) * QK_SCALE_LOG2
          for i in range(2)]
    q_stack = jnp.concatenate(
        [q2[h // 2][:, (h % 2) * HEAD_DIM:(h % 2 + 1) * HEAD_DIM] for h in range(DSA_HEADS)],
        axis=0).astype(BF16)
    iq2 = [_rope(ziq_ref[qrows, i * LANES:(i + 1) * LANES], c32, s32, 4, IDX_DIM) for i in range(2)]
    iq_stack = jnp.concatenate(
        [iq2[h // 4][:, (h % 4) * IDX_DIM:(h % 4 + 1) * IDX_DIM] for h in range(IDX_HEADS)],
        axis=0).astype(BF16)
    iw_t = zms_ref[qrows, :].T[IDX_DIM:IDX_DIM + IDX_HEADS, :] * ((IDX_HEADS * IDX_DIM) ** -0.5)
    iw_row = jnp.concatenate([iw_t[h:h + 1, :] for h in range(IDX_HEADS)], axis=1)

    kpos_l = lax.broadcasted_iota(I32, (kb_rows, blk), 0)
    qpos = r0 + lax.broadcasted_iota(I32, (kb_rows, blk), 1)

    tri = jnp.where(lax.broadcasted_iota(I32, (blk, blk), 1)
                    <= lax.broadcasted_iota(I32, (blk, blk), 0), 1.0, 0.0).astype(BF16)
    g = zg_ref[qrows, :]
    gate = g * _sigmoid(g)

    def step(n):
        rows = [slice(kb * kb_rows, (kb + 1) * kb_rows) for kb in range(n)]
        groups = kb_rows // BIT_GROUP_ROWS
        causal = (n - 1) * kb_rows + kpos_l <= qpos

        for kb in range(n):
            s = jnp.maximum(_dot_nt(ik_ref[rows[kb], :], iq_stack), 0.0) * iw_row
            sc = s[:, 0:blk]
            for h in range(1, IDX_HEADS):
                sc = sc + s[:, h * blk:(h + 1) * blk]
            if kb == n - 1:
                sc = jnp.where(causal, sc, -jnp.inf)
            bits = pltpu.bitcast(sc, I32)
            key = bits ^ ((bits >> 31) & 0x7FFFFFFF)
            key_ref[rows[kb], :] = key
            ku = key ^ INT_MIN
            for grp in range(groups):
                words = [ku[(grp * 32 + v) * 8:(grp * 32 + v + 1) * 8, :] for v in range(32)]
                for b, plane in enumerate(_bit_transpose32(words)):
                    plane_ref[kb * groups + grp, b] = plane

        n_groups = n * groups

        def bit_body(i, carry):
            active, k_rem, thr_u = carry
            ones = [active[g] & plane_ref[g, i] for g in range(n_groups)]
            acc = lax.population_count(ones[0])
            for g in range(1, n_groups):
                acc = acc + lax.population_count(ones[g])
            cnt = acc.sum(axis=0, keepdims=True)
            dec = cnt >= k_rem
            k_rem = jnp.where(dec, k_rem, k_rem - cnt)
            thr_u = thr_u | jnp.where(dec, jnp.int32(1) << (31 - i), 0)
            dec8 = jnp.broadcast_to(dec, (8, blk))
            active = tuple(jnp.where(dec8, ones[g], active[g] ^ ones[g]) for g in range(n_groups))
            return active, k_rem, thr_u

        def search():
            init = (tuple(jnp.full((8, blk), -1, I32) for _ in range(n_groups)),
                    jnp.full((1, blk), DSA_TOPK, I32), jnp.zeros((1, blk), I32))
            _, k_rem, thr_u = lax.fori_loop(0, 32, bit_body, init)
            return thr_u ^ INT_MIN, k_rem

        keep_all = lambda: (jnp.full((1, blk), INT_MIN, I32), jnp.zeros((1, blk), I32))
        if n * kb_rows - blk + 1 > DSA_TOPK:
            thr, need = search()
        else:
            thr, need = lax.cond(qb * blk >= DSA_TOPK, search, keep_all)
        need = need.astype(F32)

        wide = DSA_HEADS * blk
        m = jnp.full((1, wide), NEG_BIG, F32)
        l = jnp.zeros((1, wide), F32)
        acc = jnp.zeros((HEAD_DIM, wide), F32)
        run = jnp.zeros((1, blk), F32)
        for kb in range(n):
            keys = key_ref[rows[kb], :]
            eq = keys == thr
            eq_b = jnp.where(eq, 1.0, 0.0).astype(BF16)
            local = [_dot(tri, eq_b[sub * blk:(sub + 1) * blk, :]) for sub in range(kb_rows // blk)]
            ranks = []
            for x in local:
                ranks.append(x + run)
                run = run + x[blk - 1:blk, :]
            rank = jnp.concatenate(ranks, axis=0)
            take = (keys > thr) | (eq & (rank <= need))
            if kb == n - 1:
                take = take & causal
            bias = jnp.where(take, 0.0, NEG_BIG)
            s = _dot_nt(k_ref[rows[kb], :], q_stack)
            s = s + jnp.concatenate([bias] * DSA_HEADS, axis=1)
            m_new = jnp.maximum(m, jnp.max(s, axis=0, keepdims=True))
            p = jnp.exp2(s - m_new)
            alpha = jnp.exp2(m - m_new)
            l = alpha * l + jnp.sum(p, axis=0, keepdims=True)
            acc = alpha * acc + _dot(vt_ref[kb], p.astype(BF16))
            m = m_new

        o_t = acc / l
        o = jnp.concatenate(
            [jnp.concatenate([o_t[:, (2 * i) * blk:(2 * i + 1) * blk], o_t[:, (2 * i + 1) * blk:(2 * i + 2) * blk]],
                             axis=0).T for i in range(DSA_HEADS // 2)], axis=1)
        o_ref[qrows, :] = (o * gate).astype(o_ref.dtype)

    for n in range(1, seq // kb_rows + 1):
        pl.when(n_kb == n)(functools.partial(step, n))


def _dsa(z2d, batch, seq, c64, s64, c32, s32):
    blk = ATT_BLOCK
    qspec = lambda u2: pl.BlockSpec((seq, 2 * LANES), lambda b, u2=u2: (b, u2))
    seqspec = lambda u: pl.BlockSpec((seq, LANES), lambda b, u=u: (b, u))
    tab = pl.BlockSpec((seq, LANES), lambda b: (0, 0))
    return pl.pallas_call(
        _dsa_kernel,
        grid=(batch,),
        in_specs=[qspec(U_DS_Q // 2), qspec(U_DS_G // 2), qspec(U_DS_IQ // 2),
                  seqspec(U_DS_KV), seqspec(U_DS_MISC), tab, tab, tab, tab],
        out_specs=pl.BlockSpec((seq, DSA_WIDTH), lambda b: (b, 0)),
        out_shape=jax.ShapeDtypeStruct((batch * seq, DSA_WIDTH), BF16),
        scratch_shapes=[
            pltpu.VMEM((seq, HEAD_DIM), BF16),
            pltpu.VMEM((seq // DSA_KEY_BLOCK, HEAD_DIM, DSA_KEY_BLOCK), BF16),
            pltpu.VMEM((seq, IDX_DIM), BF16),
            pltpu.VMEM((seq, blk), I32),
            pltpu.VMEM((seq // BIT_GROUP_ROWS, 32, 8, blk), I32),
        ],
        compiler_params=pltpu.CompilerParams(
            dimension_semantics=("parallel",), vmem_limit_bytes=VMEM_LIMIT),
        name="dsa",
    )(z2d, z2d, z2d, z2d, z2d, c64, s64, c32, s32)


def _dil_group(gi, rate, q_ref, k_ref, v_ref, g_ref, c64_ref, s64_ref, qr_ref, kr_ref, o_s, lse_s):
    seq = q_ref.shape[0]
    blk = ATT_BLOCK
    rope_rows = 4 * blk
    nrow = seq // rope_rows

    def rope_body(t, carry):
        rows = pl.ds(pl.multiple_of(t * rope_rows, rope_rows), rope_rows)
        ct = c64_ref[rows, :]
        st = s64_ref[rows, :]
        q = q_ref[rows, :]
        k = k_ref[rows, :]
        qr_ref[rows, :] = (q * ct + pltpu.roll(q, HEAD_DIM, 1) * st) * QK_SCALE_LOG2
        kr_ref[rows, :] = k * ct + pltpu.roll(k, HEAD_DIM, 1) * st
        return carry

    lax.fori_loop(0, nrow, rope_body, 0)

    nblk = seq // (blk * rate)
    qi = lax.broadcasted_iota(I32, (blk, 2 * blk), 0)
    kj = lax.broadcasted_iota(I32, (blk, 2 * blk), 1)
    band = (kj >= qi) & (kj <= qi + DIL_STEPS)

    def strided(ref, start):
        if rate == 1:
            return ref[pl.ds(start, blk), :]
        return ref[pl.ds(start, blk, stride=rate), :]

    def store(ref, start, val):
        if rate == 1:
            ref[gi, pl.ds(start, blk), :] = val
        else:
            ref[gi, pl.ds(start, blk, stride=rate), :] = val

    first_q = lax.broadcasted_iota(I32, (blk, LANES), 1) < HEAD_DIM
    first_kv = lax.broadcasted_iota(I32, (2 * blk, LANES), 1) < HEAD_DIM
    lane64 = lax.broadcasted_iota(I32, (blk, LANES), 1) % HEAD_DIM
    first_qk = (lane64 < 8) | ((lane64 >= 16) & (lane64 < 40))
    unroll = DIL_UNROLL

    def body(it, carry):
        starts, qs, ks, vs, masks = [], [], [], [], []
        for u in range(unroll):
            cidx = it * unroll + u
            rho = cidx // nblk
            j = cidx % nblk
            start = j * (blk * rate) + rho
            pstart = jnp.maximum(j - 1, 0) * (blk * rate) + rho
            starts.append(start)
            qs.append(strided(qr_ref, start))
            ks.append(jnp.concatenate([strided(kr_ref, pstart), strided(kr_ref, start)], axis=0).astype(BF16))
            vs.append(jnp.concatenate([strided(v_ref, pstart), strided(v_ref, start)], axis=0))
            masks.append(band & ((kj >= blk) | (j > 0)))
        prob = [(u, h) for u in range(unroll) for h in range(2)]
        qh = [jnp.where(first_qk == (h == 0), qs[u], 0.0).astype(BF16) for u, h in prob]
        vh = [jnp.where(first_kv == (h == 0), vs[u], 1.0).astype(BF16) for u, h in prob]
        s = [jnp.where(masks[u], _dot_nt(qh[i], ks[u]), -jnp.inf) for i, (u, h) in enumerate(prob)]
        m = [jnp.max(x, axis=-1, keepdims=True) for x in s]
        r = [_dot(jnp.exp2(s[i] - m[i]).astype(BF16), vh[i]) for i in range(len(prob))]
        for u in range(unroll):
            r0, r1 = r[2 * u], r[2 * u + 1]
            o = jnp.where(first_q, r0, r1)
            l = pltpu.roll(jnp.where(first_q, r1, r0), HEAD_DIM, 1)
            lse = jnp.where(first_q, m[2 * u], m[2 * u + 1]) * math.log(2.0) + jnp.log(l)
            gate = strided(g_ref, starts[u])
            store(o_s, starts[u], o / l * (gate * _sigmoid(gate)))
            store(lse_s, starts[u], lse)
        return carry

    lax.fori_loop(0, rate * nblk // unroll, body, 0)


def _dil_kernel(q_ref, k_ref, v_ref, g_ref, c64_ref, s64_ref, o_ref, qr_ref, kr_ref, o_s, lse_s):
    grp = pl.program_id(1)
    for gi, rate in enumerate(DIL_RATES):
        pl.when(grp == gi)(functools.partial(
            _dil_group, gi, rate, q_ref, k_ref, v_ref, g_ref, c64_ref, s64_ref, qr_ref, kr_ref, o_s, lse_s))

    @pl.when(grp == len(DIL_RATES) - 1)
    def _():
        blk = ATT_BLOCK

        def body(t, carry):
            rows = pl.ds(pl.multiple_of(t * blk, blk), blk)
            ls = [lse_s[i, rows, :] for i in range(3)]
            top = jnp.maximum(jnp.maximum(ls[0], ls[1]), ls[2])
            es = [jnp.exp(x - top) for x in ls]
            inv = 1.0 / (es[0] + es[1] + es[2])
            o_ref[rows, :] = jnp.concatenate(
                [o_s[i, rows, :] * (es[i] * inv) for i in range(3)], axis=1).astype(o_ref.dtype)
            return carry

        lax.fori_loop(0, q_ref.shape[0] // blk, body, 0)


def _dilated(z2d, batch, seq, c64, s64):
    ng = len(DIL_RATES)
    zspec = lambda base: pl.BlockSpec((seq, LANES), lambda b, g, base=base: (b, base + g))
    tab = pl.BlockSpec((seq, LANES), lambda b, g: (0, 0))
    return pl.pallas_call(
        _dil_kernel,
        grid=(batch, ng),
        in_specs=[zspec(U_DL), zspec(U_DL + 3), zspec(U_DL + 6), zspec(U_DL + 9), tab, tab],
        out_specs=pl.BlockSpec((seq, DIL_WIDTH), lambda b, g: (b, 0)),
        out_shape=jax.ShapeDtypeStruct((batch * seq, DIL_WIDTH), BF16),
        scratch_shapes=[
            pltpu.VMEM((seq, LANES), F32),
            pltpu.VMEM((seq, LANES), F32),
            pltpu.VMEM((ng, seq, LANES), F32),
            pltpu.VMEM((ng, seq, LANES), F32),
        ],
        compiler_params=pltpu.CompilerParams(
            dimension_semantics=("parallel", "arbitrary"), vmem_limit_bytes=VMEM_LIMIT),
        name="dilated",
    )(z2d, z2d, z2d, z2d, c64, s64)


def _rope_tables(seq, head, half):
    inv = ROPE_THETA ** (-jnp.arange(half, dtype=F32) / half)
    ang = jnp.arange(seq, dtype=F32)[:, None] * inv[None, :]
    cos, sin = jnp.cos(ang), jnp.sin(ang)
    lane = jnp.arange(LANES) % head
    idx = lane % half
    cos_t = jnp.where(lane[None, :] < 2 * half, cos[:, idx], 1.0)
    sin_t = jnp.where(lane[None, :] < half, -sin[:, idx], jnp.where(lane[None, :] < 2 * half, sin[:, idx], 0.0))
    return cos_t, sin_t


def _dil_pair_lanes():
    half, rest = HEAD_DIM // 8, (HEAD_DIM - HEAD_DIM // 4) // 2
    src = []
    for part in range(2):
        for head in range(2):
            src += [head * HEAD_DIM + part * half + i for i in range(half)]
        for head in range(2):
            src += [head * HEAD_DIM + 2 * half + part * rest + i for i in range(rest)]
    return src


def _rope_tables_paired(seq):
    half = HEAD_DIM // 8
    inv = ROPE_THETA ** (-jnp.arange(half, dtype=F32) / half)
    ang = jnp.arange(seq, dtype=F32)[:, None] * inv[None, :]
    cos, sin = jnp.cos(ang), jnp.sin(ang)
    lane = jnp.arange(LANES)[None, :]
    idx = jnp.arange(LANES) % half
    rotated = lane % HEAD_DIM < 2 * half
    cos_t = jnp.where(rotated, cos[:, idx], 1.0)
    sin_t = jnp.where(rotated, jnp.where(lane < HEAD_DIM, -sin[:, idx], sin[:, idx]), 0.0)
    return cos_t, sin_t


def _permute_in_cols(w):
    rw = RWKV_WIDTH
    o = 0
    r, k, v, g = (w[..., o + i * rw:o + (i + 1) * rw] for i in range(4))
    o = 4 * rw
    wd, ad = w[..., o:o + LORA_RANK], w[..., o + LORA_RANK:o + 2 * LORA_RANK]
    o += 2 * LORA_RANK
    widths = (DSA_WIDTH, HEAD_DIM, HEAD_DIM, IDX_HEADS * IDX_DIM, IDX_DIM, IDX_HEADS, DSA_WIDTH)
    parts = []
    for wdt in widths:
        parts.append(w[..., o:o + wdt])
        o += wdt
    dq, dk, dv, diq, dik, diw, dg = parts
    cq, ck, cv, cg = (w[..., o + i * DIL_WIDTH:o + (i + 1) * DIL_WIDTH] for i in range(4))
    pair = jnp.asarray([t * LANES + s for t in range(DIL_WIDTH // LANES) for s in _dil_pair_lanes()])
    cq, ck = cq[..., pair], ck[..., pair]
    pad = jnp.zeros(w.shape[:-1] + (LANES - IDX_DIM - IDX_HEADS,), w.dtype)
    return jnp.concatenate([r, k, v, g, cq, ck, cv, cg, dq, dg, diq, dk, dv, dik, diw, pad, wd, ad], axis=-1)


def kernel(x, p, norm_g, w_in, tshift_mu, rwkv_w0, rwkv_w_up, rwkv_a0, rwkv_a_up, rwkv_k_k, rwkv_k_a, rwkv_r_k,
           rwkv_ln_g, rwkv_ln_b, w_out, ple_norm_g, ple_w_gate, ple_w_proj, final_norm_g):
    batch, seq, d_model = x.shape
    depth = w_in.shape[0]
    n = batch * seq
    assert seq % (ATT_BLOCK * max(DIL_RATES)) == 0 and seq >= 4 * DSA_TOPK

    c64, s64 = _rope_tables(seq, HEAD_DIM, HEAD_DIM // 8)
    c32, s32 = _rope_tables(seq, IDX_DIM, IDX_DIM // 8)
    c64p, s64p = _rope_tables_paired(seq)
    w_in_k = _permute_in_cols(w_in.astype(BF16))
    rw = RWKV_WIDTH
    row = lambda a: a.reshape(1, -1)

    x2d = x.reshape(n, d_model)
    p3d = p.reshape(depth, n, p.shape[-1])
    for i in range(depth):
        z = _in_proj(x2d, row(norm_g[i]), w_in_k, i)
        mu = tshift_mu[i]
        y_a = _rwkv(z.reshape(batch, seq, Z_COLS), mu[:4 * rw].reshape(4, rw), row(mu[4 * rw:]), row(rwkv_w0[i]), rwkv_w_up[i],
                    row(rwkv_a0[i]), rwkv_a_up[i], row(rwkv_k_k[i]), row(rwkv_k_a[i]), row(rwkv_r_k[i]),
                    row(rwkv_ln_g[i]), row(rwkv_ln_b[i])).reshape(n, rw)
        y_b = _dsa(z, batch, seq, c64, s64, c32, s32)
        y_c = _dilated(z, batch, seq, c64p, s64p)
        wo = w_out[i].astype(BF16)
        x2d = _out_proj(x2d, y_a, y_b, y_c, p3d, i, wo[:rw], wo[rw:rw + DSA_WIDTH],
                        wo[rw + DSA_WIDTH:], row(ple_norm_g[i]), ple_w_gate[i].astype(BF16),
                        ple_w_proj[i].astype(BF16), row(final_norm_g), final=(i == depth - 1))
    return x2d.reshape(batch, seq, d_model)
```

```python
import functools
import math

import jax
import jax.numpy as jnp
from jax import lax
from jax.experimental import pallas as pl
from jax.experimental.pallas import tpu as pltpu

F32 = jnp.float32
BF16 = jnp.bfloat16
I32 = jnp.int32

HEAD_DIM = 64
ROPE_THETA = 500000.0
NORM_EPS = 1e-6
RWKV_HEADS = 6
RWKV_WIDTH = RWKV_HEADS * HEAD_DIM
LORA_RANK = 64
GN_EPS = 64e-5
DECAY_SCALE = math.exp(-0.5)
DSA_HEADS = 4
DSA_WIDTH = DSA_HEADS * HEAD_DIM
IDX_HEADS = 8
IDX_DIM = 32
DSA_TOPK = 256
DIL_RATES = (1, 4, 16)
DIL_STEPS = 128
DIL_UNROLL = 4
DIL_WIDTH = 384
ATT_BLOCK = 128
DSA_KEY_BLOCK = 512

LANES = 128
MXU_WIDTH = 256
VMEM_LIMIT = 48 * 1024 * 1024

U_RW = 0
U_DL = 12
U_DS_Q = 24
U_DS_G = 26
U_DS_IQ = 28
U_DS_KV = 30
U_DS_MISC = 31
U_RW_WA = 32
Z_UNITS = 33
Z_COLS = Z_UNITS * LANES

RWKV_CHUNK = 128
RWKV_BATCH_ROWS = 4
INV_BASE = 16
QK_SCALE_LOG2 = HEAD_DIM ** -0.5 * math.log2(math.e)
NEG_BIG = -1e30
BIT_GROUP_ROWS = 256
INT_MIN = -2 ** 31


def _dot(a, b):
    return jnp.dot(a, b, preferred_element_type=F32)


def _dot_nt(a, b):
    return lax.dot_general(a, b, (((1,), (1,)), ((), ())), preferred_element_type=F32)


def _split(x):
    hi = x.astype(BF16)
    lo = (x - hi.astype(F32)).astype(BF16)
    return hi, lo


def _head_sum(x, head_ones):
    return _dot(x.astype(BF16), head_ones)


def _dot_x3(x, w):
    xh, xl = _split(x)
    wh, wl = _split(w)
    return _dot(xh, wh) + (_dot(xh, wl) + _dot(xl, wh))


def _sigmoid(x):
    return 1.0 / (1.0 + jnp.exp(-x))


def _rope(x, cos_t, sin_t, half, head):
    lane = lax.broadcasted_iota(I32, x.shape, 1) % head
    partner = jnp.where(lane < half, pltpu.roll(x, LANES - half, 1), pltpu.roll(x, half, 1))
    return x * cos_t + partner * sin_t


def _in_proj_kernel(x_ref, g_ref, w_ref, z_ref, *, tn):
    x = x_ref[...]
    ms = jnp.mean(x * x, axis=-1, keepdims=True)
    h = (x * lax.rsqrt(ms + NORM_EPS) * g_ref[...]).astype(BF16)
    for c in range(0, Z_COLS, tn):
        wdt = min(tn, Z_COLS - c)
        z_ref[:, c:c + wdt] = _dot(h, w_ref[:, c:c + wdt])


def _in_proj(x2d, g, w_all, layer, tm=512, tn=MXU_WIDTH):
    n, d = x2d.shape
    return pl.pallas_call(
        functools.partial(_in_proj_kernel, tn=tn),
        grid=(n // tm,),
        in_specs=[
            pl.BlockSpec((tm, d), lambda i: (i, 0)),
            pl.BlockSpec((1, d), lambda i: (0, 0)),
            pl.BlockSpec((None, d, Z_COLS), lambda i: (layer, 0, 0)),
        ],
        out_specs=pl.BlockSpec((tm, Z_COLS), lambda i: (i, 0)),
        out_shape=jax.ShapeDtypeStruct((n, Z_COLS), F32),
        compiler_params=pltpu.CompilerParams(
            dimension_semantics=("parallel",), vmem_limit_bytes=VMEM_LIMIT),
        name="in_proj",
    )(x2d, g, w_all)


def _out_proj_kernel(x_ref, ya_ref, yb_ref, yc_ref, p_ref, wa_ref, wb_ref, wc_ref, ng_ref, wg_ref,
                     wp_ref, fg_ref, o_ref, *, final):
    acc = x_ref[...] + _dot(ya_ref[...], wa_ref[...])
    acc = acc + _dot(yb_ref[...], wb_ref[...])
    acc = acc + _dot(yc_ref[...], wc_ref[...])
    ms = jnp.mean(acc * acc, axis=-1, keepdims=True)
    hn = (acc * lax.rsqrt(ms + NORM_EPS) * ng_ref[...]).astype(BF16)
    gate = _sigmoid(_dot(hn, wg_ref[...]))
    out = acc + gate * _dot(p_ref[...].astype(BF16), wp_ref[...])
    if final:
        ms = jnp.mean(out * out, axis=-1, keepdims=True)
        out = out * lax.rsqrt(ms + NORM_EPS) * fg_ref[...]
    o_ref[...] = out


def _out_proj(x2d, ya, yb, yc, p3d, layer, wa, wb, wc, ng, wg, wp, fg, final, tm=1024):
    n, d = x2d.shape
    row = lambda w: pl.BlockSpec((tm, w), lambda i: (i, 0))
    full = lambda a: pl.BlockSpec(a.shape, lambda i: (0, 0))
    return pl.pallas_call(
        functools.partial(_out_proj_kernel, final=final),
        grid=(n // tm,),
        in_specs=[row(d), row(ya.shape[1]), row(yb.shape[1]), row(yc.shape[1]),
                  pl.BlockSpec((None, tm, p3d.shape[2]), lambda i: (layer, i, 0)),
                  full(wa), full(wb), full(wc), full(ng), full(wg), full(wp), full(fg)],
        out_specs=row(d),
        out_shape=jax.ShapeDtypeStruct((n, d), F32),
        compiler_params=pltpu.CompilerParams(
            dimension_semantics=("parallel",), vmem_limit_bytes=VMEM_LIMIT),
        name="out_proj",
    )(x2d, ya, yb, yc, p3d, wa, wb, wc, ng, wg, wp, fg)


def _unit_lower_inverse(n_list, row, col):
    c = n_list[0].shape[0]
    eye = jnp.where(row == col, 1.0, 0.0).astype(F32)
    same = (row // INV_BASE) == (col // INV_BASE)
    pw = [jnp.where(same, n, 0.0) for n in n_list]
    t = [eye - x for x in pw]
    span = 2
    while span < INV_BASE:
        pwb = [x.astype(BF16) for x in pw]
        pw = [_dot(x, x) for x in pwb]
        t = [x + _dot(x.astype(BF16), y.astype(BF16)) for x, y in zip(t, pw)]
        span *= 2
    size = INV_BASE
    while size < c:
        lower_left = ((row // (2 * size)) == (col // (2 * size))) & ((row // size) != (col // size))
        tb = [x.astype(BF16) for x in t]
        et = [_dot(jnp.where(lower_left, n, 0.0).astype(BF16), x).astype(BF16) for n, x in zip(n_list, tb)]
        t = [x - _dot(xb, y) for x, xb, y in zip(t, tb, et)]
        size *= 2
    return t


def _rwkv_kernel(zr_ref, zk_ref, zv_ref, zg_ref, zwa_ref, mu_ref, muwa_ref, w0_ref, wup_ref, a0_ref,
                 aup_ref, kk_ref, ka_ref, rk_ref, lng_ref, lnb_ref, o_ref,
                 state_ref, prev_ref, prevwa_ref, y_ref):
    nb, c = zr_ref.shape[0], zr_ref.shape[1]
    width = RWKV_WIDTH

    @pl.when(pl.program_id(1) == 0)
    def _():
        state_ref[...] = jnp.zeros_like(state_ref)
        prev_ref[...] = jnp.zeros_like(prev_ref)
        prevwa_ref[...] = jnp.zeros_like(prevwa_ref)

    first_row = lax.broadcasted_iota(I32, (c, width), 0) == 0
    first_row_wa = lax.broadcasted_iota(I32, (c, LANES), 0) == 0
    hrow = lax.broadcasted_iota(I32, (width, width), 0) // HEAD_DIM
    hcol = lax.broadcasted_iota(I32, (width, width), 1) // HEAD_DIM
    head_ones = jnp.where(hrow == hcol, 1.0, 0.0).astype(BF16)
    row = lax.broadcasted_iota(I32, (c, c), 0)
    col = lax.broadcasted_iota(I32, (c, c), 1)
    tri = jnp.where(col <= row, 1.0, 0.0).astype(BF16)
    strict = col < row
    incl = col <= row
    hs = [slice(h * HEAD_DIM, (h + 1) * HEAD_DIM) for h in range(RWKV_HEADS)]

    def prepare(bi):
        def shifted(z_ref, idx):
            z = z_ref[bi]
            prev = jnp.where(first_row, prev_ref[bi, idx, 0:1, :], pltpu.roll(z, 1, 0))
            prev_ref[bi, idx, 0:1, :] = z[c - 1:c, :]
            return z + (prev - z) * mu_ref[idx:idx + 1, :]

        r = shifted(zr_ref, 0)
        k = shifted(zk_ref, 1)
        v = shifted(zv_ref, 2)
        g = shifted(zg_ref, 3)
        zwa = zwa_ref[bi]
        prev_wa = jnp.where(first_row_wa, prevwa_ref[bi, 0:1, :], pltpu.roll(zwa, 1, 0))
        prevwa_ref[bi, 0:1, :] = zwa[c - 1:c, :]
        wa = zwa + (prev_wa - zwa) * muwa_ref[...]
        wd = wa[:, :LORA_RANK]
        ad = wa[:, LORA_RANK:]

        lw = -DECAY_SCALE * _sigmoid(w0_ref[...] + _dot_x3(jnp.tanh(wd), wup_ref[...]))
        a = _sigmoid(a0_ref[...] + _dot_x3(ad, aup_ref[...]))
        kk = k * kk_ref[...]
        norm = jnp.sqrt(_head_sum(kk * kk, head_ones))
        kk = kk / jnp.maximum(norm, 1e-12)
        kmod = k * (1.0 + (a - 1.0) * ka_ref[...])
        b = a * kk

        lw_hi, lw_lo = _split(lw)
        cl = _dot(tri, lw_hi) + _dot(tri, lw_lo)
        c_mid = cl[c // 2 - 1:c // 2, :]
        c_end = cl[c - 1:c, :]
        w = jnp.exp(lw)
        e_in1 = jnp.exp(cl - lw)
        e_in2 = e_in1 * w
        e_k = jnp.exp(c_mid - cl)
        e_mid_inv = jnp.exp(-c_mid)
        e_end = e_k * jnp.exp(c_end - c_mid)
        ops = dict(
            q1=kk * e_in1 * e_mid_inv, q2=r * e_in2 * e_mid_inv, k1=b * e_k, k2=kmod * e_k,
            kkd=kk * e_in1, rd=r * e_in2, bd=b * e_end, kd=kmod * e_end, vb=v)
        ops = {name: [x.astype(BF16)[:, s] for s in hs] for name, x in ops.items()}
        return dict(ops, r=r, kmod=kmod, v=v, g=g, d_end=jnp.exp(c_end))

    prep = [prepare(bi) for bi in range(nb)]

    probs = [(bi, h) for bi in range(nb) for h in range(RWKV_HEADS)]
    op = lambda name: [prep[bi][name][h] for bi, h in probs]
    q1, q2, k1, k2, kkd, rd, bd, kd, vh = map(op, ("q1", "q2", "k1", "k2", "kkd", "rd", "bd", "kd", "vb"))
    n = range(len(probs))
    qk = [_dot_nt(jnp.concatenate([q1[i], q2[i]], axis=0), jnp.concatenate([k1[i], k2[i]], axis=0)) for i in n]
    a_ab = [jnp.where(strict, x[:c, :c], 0.0) for x in qk]
    t_inv = _unit_lower_inverse(a_ab, row, col)
    a_rb = [jnp.where(incl, x[c:, :c], 0.0).astype(BF16) for x in qk]
    a_v = [jnp.concatenate([jnp.where(strict, x[:c, c:], 0.0), jnp.where(incl, x[c:, c:], 0.0)],
                           axis=0).astype(BF16) for x in qk]
    s_old = [state_ref[bi, h] for bi, h in probs]
    s_in = [s.astype(BF16) for s in s_old]
    from_v = [_dot(a_v[i], vh[i]) for i in n]
    from_s = [_dot_nt(jnp.concatenate([kkd[i], rd[i]], axis=0), s_in[i]) for i in n]
    rhs = [from_s[i][:c] + from_v[i][:c] for i in n]
    u = [-_dot(t_inv[i].astype(BF16), rhs[i].astype(BF16)) for i in n]
    uv_t = [jnp.concatenate([u[i], prep[bi]["v"][:, hs[h]]], axis=1).T.astype(BF16)
            for i, (bi, h) in enumerate(probs)]
    for i, (bi, h) in enumerate(probs):
        state_ref[bi, h] = (s_old[i] * prep[bi]["d_end"][:, hs[h]]
                            + _dot(uv_t[i][:HEAD_DIM, :], bd[i]) + _dot(uv_t[i][HEAD_DIM:, :], kd[i]))
    for i, (bi, h) in enumerate(probs):
        y_ref[bi, :, hs[h]] = from_s[i][c:] + _dot(a_rb[i], u[i].astype(BF16)) + from_v[i][c:]

    inv_n = 1.0 / HEAD_DIM
    for bi in range(nb):
        pb = prep[bi]
        y = y_ref[bi]
        mean = _head_sum(y, head_ones) * inv_n
        dlt = y - mean
        var = _head_sum(dlt * dlt, head_ones) * inv_n
        yn = dlt * lax.rsqrt(var + GN_EPS) * lng_ref[...] + lnb_ref[...]
        bonus = _head_sum(pb["r"] * pb["kmod"] * rk_ref[...], head_ones) * pb["v"]
        g = pb["g"]
        o_ref[bi] = ((yn + bonus) * (g * _sigmoid(g))).astype(o_ref.dtype)


def _rwkv(z3d, mu4, mu_wa, w0, w_up, a0, a_up, k_k, k_a, r_k, ln_g, ln_b):
    batch, seq, _ = z3d.shape
    c = RWKV_CHUNK
    nb = RWKV_BATCH_ROWS if batch % RWKV_BATCH_ROWS == 0 else 1
    zspec = lambda u: pl.BlockSpec((nb, c, RWKV_WIDTH), lambda b, i, u=u: (b, i, u))
    full = lambda a: pl.BlockSpec(a.shape, lambda b, i: (0,) * a.ndim)
    params = (mu4, mu_wa, w0, w_up, a0, a_up, k_k, k_a, r_k, ln_g, ln_b)
    return pl.pallas_call(
        _rwkv_kernel,
        grid=(batch // nb, seq // c),
        in_specs=[zspec(0), zspec(1), zspec(2), zspec(3),
                  pl.BlockSpec((nb, c, LANES), lambda b, i: (b, i, U_RW_WA))]
                 + [full(a) for a in params],
        out_specs=pl.BlockSpec((nb, c, RWKV_WIDTH), lambda b, i: (b, i, 0)),
        out_shape=jax.ShapeDtypeStruct((batch, seq, RWKV_WIDTH), BF16),
        scratch_shapes=[
            pltpu.VMEM((nb, RWKV_HEADS, HEAD_DIM, HEAD_DIM), F32),
            pltpu.VMEM((nb, 4, 8, RWKV_WIDTH), F32),
            pltpu.VMEM((nb, 8, LANES), F32),
            pltpu.VMEM((nb, c, RWKV_WIDTH), F32),
        ],
        compiler_params=pltpu.CompilerParams(
            dimension_semantics=("parallel", "arbitrary"), vmem_limit_bytes=VMEM_LIMIT),
        name="rwkv7",
    )(z3d, z3d, z3d, z3d, z3d, *params)


def _bit_transpose32(words):
    a = list(words)
    j, mask = 16, 0x0000FFFF
    while j:
        for k in range(32):
            if k & j == 0:
                t = (a[k] ^ lax.shift_right_logical(a[k + j], jnp.int32(j))) & jnp.int32(mask)
                a[k] = a[k] ^ t
                a[k + j] = a[k + j] ^ (t << j)
        j >>= 1
        mask ^= mask << j
    return a


def _dsa_kernel(*refs):
    seq = refs[3].shape[0]
    lax.fori_loop(0, seq // ATT_BLOCK, lambda qb, carry: (_dsa_query_block(qb, *refs), carry)[1], 0)


def _dsa_query_block(qb, zq_ref, zg_ref, ziq_ref, zkv_ref, zms_ref, c64_ref, s64_ref, c32_ref, s32_ref, o_ref,
                     k_ref, vt_ref, ik_ref, key_ref, plane_ref):
    seq = zkv_ref.shape[0]
    blk = ATT_BLOCK
    kb_rows = DSA_KEY_BLOCK
    n_kb = (qb * blk) // kb_rows + 1

    @pl.when(qb == 0)
    def _():
        lane = lax.broadcasted_iota(I32, (kb_rows, LANES), 1)

        def body(t, carry):
            rows = pl.ds(pl.multiple_of(t * kb_rows, kb_rows), kb_rows)
            kv = zkv_ref[rows, :]
            kvr = jnp.where(lane < HEAD_DIM, _rope(kv, c64_ref[rows, :], s64_ref[rows, :], 8, HEAD_DIM), kv)
            k_ref[rows, :] = kvr[:, :HEAD_DIM].astype(BF16)
            vt_ref[t] = kvr.T[HEAD_DIM:, :].astype(BF16)
            ms = zms_ref[rows, :]
            ikr = _rope(ms, c32_ref[rows, :], s32_ref[rows, :], 4, IDX_DIM)
            ik_ref[rows, :] = ikr[:, :IDX_DIM].astype(BF16)
            return carry

        lax.fori_loop(0, seq // kb_rows, body, 0)

    r0 = pl.multiple_of(qb * blk, blk)
    qrows = pl.ds(r0, blk)
    c64 = c64_ref[qrows, :]
    s64 = s64_ref[qrows, :]
    c32 = c32_ref[qrows, :]
    s32 = s32_ref[qrows, :]
    q2 = [_rope(zq_ref[qrows, i * LANES:(i + 1) * LANES], c64, s64, 8, HEAD_DIM) * QK_SCALE_LOG2
          for i in range(2)]
    q_stack = jnp.concatenate(
        [q2[h // 2][:, (h % 2) * HEAD_DIM:(h % 2 + 1) * HEAD_DIM] for h in range(DSA_HEADS)],
        axis=0).astype(BF16)
    iq2 = [_rope(ziq_ref[qrows, i * LANES:(i + 1) * LANES], c32, s32, 4, IDX_DIM) for i in range(2)]
    iq_stack = jnp.concatenate(
        [iq2[h // 4][:, (h % 4) * IDX_DIM:(h % 4 + 1) * IDX_DIM] for h in range(IDX_HEADS)],
        axis=0).astype(BF16)
    iw_t = zms_ref[qrows, :].T[IDX_DIM:IDX_DIM + IDX_HEADS, :] * ((IDX_HEADS * IDX_DIM) ** -0.5)
    iw_row = jnp.concatenate([iw_t[h:h + 1, :] for h in range(IDX_HEADS)], axis=1)

    kpos_l = lax.broadcasted_iota(I32, (kb_rows, blk), 0)
    qpos = r0 + lax.broadcasted_iota(I32, (kb_rows, blk), 1)

    tri = jnp.where(lax.broadcasted_iota(I32, (blk, blk), 1)
                    <= lax.broadcasted_iota(I32, (blk, blk), 0), 1.0, 0.0).astype(BF16)
    g = zg_ref[qrows, :]
    gate = g * _sigmoid(g)

    def step(n):
        rows = [slice(kb * kb_rows, (kb + 1) * kb_rows) for kb in range(n)]
        groups = kb_rows // BIT_GROUP_ROWS
        causal = (n - 1) * kb_rows + kpos_l <= qpos

        for kb in range(n):
            s = jnp.maximum(_dot_nt(ik_ref[rows[kb], :], iq_stack), 0.0) * iw_row
            sc = s[:, 0:blk]
            for h in range(1, IDX_HEADS):
                sc = sc + s[:, h * blk:(h + 1) * blk]
            if kb == n - 1:
                sc = jnp.where(causal, sc, -jnp.inf)
            bits = pltpu.bitcast(sc, I32)
            key = bits ^ ((bits >> 31) & 0x7FFFFFFF)
            key_ref[rows[kb], :] = key
            ku = key ^ INT_MIN
            for grp in range(groups):
                words = [ku[(grp * 32 + v) * 8:(grp * 32 + v + 1) * 8, :] for v in range(32)]
                for b, plane in enumerate(_bit_transpose32(words)):
                    plane_ref[kb * groups + grp, b] = plane

        n_groups = n * groups

        def bit_body(i, carry):
            active, k_rem, thr_u = carry
            ones = [active[g] & plane_ref[g, i] for g in range(n_groups)]
            acc = lax.population_count(ones[0])
            for g in range(1, n_groups):
                acc = acc + lax.population_count(ones[g])
            cnt = acc.sum(axis=0, keepdims=True)
            dec = cnt >= k_rem
            k_rem = jnp.where(dec, k_rem, k_rem - cnt)
            thr_u = thr_u | jnp.where(dec, jnp.int32(1) << (31 - i), 0)
            dec8 = jnp.broadcast_to(dec, (8, blk))
            active = tuple(jnp.where(dec8, ones[g], active[g] ^ ones[g]) for g in range(n_groups))
            return active, k_rem, thr_u

        def search():
            init = (tuple(jnp.full((8, blk), -1, I32) for _ in range(n_groups)),
                    jnp.full((1, blk), DSA_TOPK, I32), jnp.zeros((1, blk), I32))
            _, k_rem, thr_u = lax.fori_loop(0, 32, bit_body, init)
            return thr_u ^ INT_MIN, k_rem

        keep_all = lambda: (jnp.full((1, blk), INT_MIN, I32), jnp.zeros((1, blk), I32))
        if n * kb_rows - blk + 1 > DSA_TOPK:
            thr, need = search()
        else:
            thr, need = lax.cond(qb * blk >= DSA_TOPK, search, keep_all)
        need = need.astype(F32)

        wide = DSA_HEADS * blk
        m = jnp.full((1, wide), NEG_BIG, F32)
        l = jnp.zeros((1, wide), F32)
        acc = jnp.zeros((HEAD_DIM, wide), F32)
        run = jnp.zeros((1, blk), F32)
        for kb in range(n):
            keys = key_ref[rows[kb], :]
            eq = keys == thr
            eq_b = jnp.where(eq, 1.0, 0.0).astype(BF16)
            local = [_dot(tri, eq_b[sub * blk:(sub + 1) * blk, :]) for sub in range(kb_rows // blk)]
            ranks = []
            for x in local:
                ranks.append(x + run)
                run = run + x[blk - 1:blk, :]
            rank = jnp.concatenate(ranks, axis=0)
            take = (keys > thr) | (eq & (rank <= need))
            if kb == n - 1:
                take = take & causal
            bias = jnp.where(take, 0.0, NEG_BIG)
            s = _dot_nt(k_ref[rows[kb], :], q_stack)
            s = s + jnp.concatenate([bias] * DSA_HEADS, axis=1)
            m_new = jnp.maximum(m, jnp.max(s, axis=0, keepdims=True))
            p = jnp.exp2(s - m_new)
            alpha = jnp.exp2(m - m_new)
            l = alpha * l + jnp.sum(p, axis=0, keepdims=True)
            acc = alpha * acc + _dot(vt_ref[kb], p.astype(BF16))
            m = m_new

        o_t = acc / l
        o = jnp.concatenate(
            [jnp.concatenate([o_t[:, (2 * i) * blk:(2 * i + 1) * blk], o_t[:, (2 * i + 1) * blk:(2 * i + 2) * blk]],
                             axis=0).T for i in range(DSA_HEADS // 2)], axis=1)
        o_ref[qrows, :] = (o * gate).astype(o_ref.dtype)

    for n in range(1, seq // kb_rows + 1):
        pl.when(n_kb == n)(functools.partial(step, n))


def _dsa(z2d, batch, seq, c64, s64, c32, s32):
    blk = ATT_BLOCK
    qspec = lambda u2: pl.BlockSpec((seq, 2 * LANES), lambda b, u2=u2: (b, u2))
    seqspec = lambda u: pl.BlockSpec((seq, LANES), lambda b, u=u: (b, u))
    tab = pl.BlockSpec((seq, LANES), lambda b: (0, 0))
    return pl.pallas_call(
        _dsa_kernel,
        grid=(batch,),
        in_specs=[qspec(U_DS_Q // 2), qspec(U_DS_G // 2), qspec(U_DS_IQ // 2),
                  seqspec(U_DS_KV), seqspec(U_DS_MISC), tab, tab, tab, tab],
        out_specs=pl.BlockSpec((seq, DSA_WIDTH), lambda b: (b, 0)),
        out_shape=jax.ShapeDtypeStruct((batch * seq, DSA_WIDTH), BF16),
        scratch_shapes=[
            pltpu.VMEM((seq, HEAD_DIM), BF16),
            pltpu.VMEM((seq // DSA_KEY_BLOCK, HEAD_DIM, DSA_KEY_BLOCK), BF16),
            pltpu.VMEM((seq, IDX_DIM), BF16),
            pltpu.VMEM((seq, blk), I32),
            pltpu.VMEM((seq // BIT_GROUP_ROWS, 32, 8, blk), I32),
        ],
        compiler_params=pltpu.CompilerParams(
            dimension_semantics=("parallel",), vmem_limit_bytes=VMEM_LIMIT),
        name="dsa",
    )(z2d, z2d, z2d, z2d, z2d, c64, s64, c32, s32)


def _dil_group(gi, rate, q_ref, k_ref, v_ref, g_ref, c64_ref, s64_ref, qr_ref, kr_ref, o_s, lse_s):
    seq = q_ref.shape[0]
    blk = ATT_BLOCK
    rope_rows = 4 * blk
    nrow = seq // rope_rows

    def rope_body(t, carry):
        rows = pl.ds(pl.multiple_of(t * rope_rows, rope_rows), rope_rows)
        ct = c64_ref[rows, :]
        st = s64_ref[rows, :]
        q = q_ref[rows, :]
        k = k_ref[rows, :]
        qr_ref[rows, :] = (q * ct + pltpu.roll(q, HEAD_DIM, 1) * st) * QK_SCALE_LOG2
        kr_ref[rows, :] = k * ct + pltpu.roll(k, HEAD_DIM, 1) * st
        return carry

    lax.fori_loop(0, nrow, rope_body, 0)

    nblk = seq // (blk * rate)
    qi = lax.broadcasted_iota(I32, (blk, 2 * blk), 0)
    kj = lax.broadcasted_iota(I32, (blk, 2 * blk), 1)
    band = (kj >= qi) & (kj <= qi + DIL_STEPS)

    def strided(ref, start):
        if rate == 1:
            return ref[pl.ds(start, blk), :]
        return ref[pl.ds(start, blk, stride=rate), :]

    def store(ref, start, val):
        if rate == 1:
            ref[gi, pl.ds(start, blk), :] = val
        else:
            ref[gi, pl.ds(start, blk, stride=rate), :] = val

    first_q = lax.broadcasted_iota(I32, (blk, LANES), 1) < HEAD_DIM
    first_kv = lax.broadcasted_iota(I32, (2 * blk, LANES), 1) < HEAD_DIM
    lane64 = lax.broadcasted_iota(I32, (blk, LANES), 1) % HEAD_DIM
    first_qk = (lane64 < 8) | ((lane64 >= 16) & (lane64 < 40))
    unroll = DIL_UNROLL

    def body(it, carry):
        starts, qs, ks, vs, masks = [], [], [], [], []
        for u in range(unroll):
            cidx = it * unroll + u
            rho = cidx // nblk
            j = cidx % nblk
            start = j * (blk * rate) + rho
            pstart = jnp.maximum(j - 1, 0) * (blk * rate) + rho
            starts.append(start)
            qs.append(strided(qr_ref, start))
            ks.append(jnp.concatenate([strided(kr_ref, pstart), strided(kr_ref, start)], axis=0).astype(BF16))
            vs.append(jnp.concatenate([strided(v_ref, pstart), strided(v_ref, start)], axis=0))
            masks.append(band & ((kj >= blk) | (j > 0)))
        prob = [(u, h) for u in range(unroll) for h in range(2)]
        qh = [jnp.where(first_qk == (h == 0), qs[u], 0.0).astype(BF16) for u, h in prob]
        vh = [jnp.where(first_kv == (h == 0), vs[u], 1.0).astype(BF16) for u, h in prob]
        s = [jnp.where(masks[u], _dot_nt(qh[i], ks[u]), -jnp.inf) for i, (u, h) in enumerate(prob)]
        m = [jnp.max(x, axis=-1, keepdims=True) for x in s]
        r = [_dot(jnp.exp2(s[i] - m[i]).astype(BF16), vh[i]) for i in range(len(prob))]
        for u in range(unroll):
            r0, r1 = r[2 * u], r[2 * u + 1]
            o = jnp.where(first_q, r0, r1)
            l = pltpu.roll(jnp.where(first_q, r1, r0), HEAD_DIM, 1)
            lse = jnp.where(first_q, m[2 * u], m[2 * u + 1]) * math.log(2.0) + jnp.log(l)
            gate = strided(g_ref, starts[u])
            store(o_s, starts[u], o / l * (gate * _sigmoid(gate)))
            store(lse_s, starts[u], lse)
        return carry

    lax.fori_loop(0, rate * nblk // unroll, body, 0)


def _dil_kernel(q_ref, k_ref, v_ref, g_ref, c64_ref, s64_ref, o_ref, qr_ref, kr_ref, o_s, lse_s):
    grp = pl.program_id(1)
    for gi, rate in enumerate(DIL_RATES):
        pl.when(grp == gi)(functools.partial(
            _dil_group, gi, rate, q_ref, k_ref, v_ref, g_ref, c64_ref, s64_ref, qr_ref, kr_ref, o_s, lse_s))

    @pl.when(grp == len(DIL_RATES) - 1)
    def _():
        blk = ATT_BLOCK

        def body(t, carry):
            rows = pl.ds(pl.multiple_of(t * blk, blk), blk)
            ls = [lse_s[i, rows, :] for i in range(3)]
            top = jnp.maximum(jnp.maximum(ls[0], ls[1]), ls[2])
            es = [jnp.exp(x - top) for x in ls]
            inv = 1.0 / (es[0] + es[1] + es[2])
            o_ref[rows, :] = jnp.concatenate(
                [o_s[i, rows, :] * (es[i] * inv) for i in range(3)], axis=1).astype(o_ref.dtype)
            return carry

        lax.fori_loop(0, q_ref.shape[0] // blk, body, 0)


def _dilated(z2d, batch, seq, c64, s64):
    ng = len(DIL_RATES)
    zspec = lambda base: pl.BlockSpec((seq, LANES), lambda b, g, base=base: (b, base + g))
    tab = pl.BlockSpec((seq, LANES), lambda b, g: (0, 0))
    return pl.pallas_call(
        _dil_kernel,
        grid=(batch, ng),
        in_specs=[zspec(U_DL), zspec(U_DL + 3), zspec(U_DL + 6), zspec(U_DL + 9), tab, tab],
        out_specs=pl.BlockSpec((seq, DIL_WIDTH), lambda b, g: (b, 0)),
        out_shape=jax.ShapeDtypeStruct((batch * seq, DIL_WIDTH), BF16),
        scratch_shapes=[
            pltpu.VMEM((seq, LANES), F32),
            pltpu.VMEM((seq, LANES), F32),
            pltpu.VMEM((ng, seq, LANES), F32),
            pltpu.VMEM((ng, seq, LANES), F32),
        ],
        compiler_params=pltpu.CompilerParams(
            dimension_semantics=("parallel", "arbitrary"), vmem_limit_bytes=VMEM_LIMIT),
        name="dilated",
    )(z2d, z2d, z2d, z2d, c64, s64)


def _rope_tables(seq, head, half):
    inv = ROPE_THETA ** (-jnp.arange(half, dtype=F32) / half)
    ang = jnp.arange(seq, dtype=F32)[:, None] * inv[None, :]
    cos, sin = jnp.cos(ang), jnp.sin(ang)
    lane = jnp.arange(LANES) % head
    idx = lane % half
    cos_t = jnp.where(lane[None, :] < 2 * half, cos[:, idx], 1.0)
    sin_t = jnp.where(lane[None, :] < half, -sin[:, idx], jnp.where(lane[None, :] < 2 * half, sin[:, idx], 0.0))
    return cos_t, sin_t


def _dil_pair_lanes():
    half, rest = HEAD_DIM // 8, (HEAD_DIM - HEAD_DIM // 4) // 2
    src = []
    for part in range(2):
        for head in range(2):
            src += [head * HEAD_DIM + part * half + i for i in range(half)]
        for head in range(2):
            src += [head * HEAD_DIM + 2 * half + part * rest + i for i in range(rest)]
    return src


def _rope_tables_paired(seq):
    half = HEAD_DIM // 8
    inv = ROPE_THETA ** (-jnp.arange(half, dtype=F32) / half)
    ang = jnp.arange(seq, dtype=F32)[:, None] * inv[None, :]
    cos, sin = jnp.cos(ang), jnp.sin(ang)
    lane = jnp.arange(LANES)[None, :]
    idx = jnp.arange(LANES) % half
    rotated = lane % HEAD_DIM < 2 * half
    cos_t = jnp.where(rotated, cos[:, idx], 1.0)
    sin_t = jnp.where(rotated, jnp.where(lane < HEAD_DIM, -sin[:, idx], sin[:, idx]), 0.0)
    return cos_t, sin_t


def _in_col_sources():
    rw = RWKV_WIDTH
    cols = lambda o, wdt: list(range(o, o + wdt))
    o = 0
    r, k, v, g = (cols(o + i * rw, rw) for i in range(4))
    o = 4 * rw
    wd, ad = cols(o, LORA_RANK), cols(o + LORA_RANK, LORA_RANK)
    o += 2 * LORA_RANK
    parts = []
    for wdt in (DSA_WIDTH, HEAD_DIM, HEAD_DIM, IDX_HEADS * IDX_DIM, IDX_DIM, IDX_HEADS, DSA_WIDTH):
        parts.append(cols(o, wdt))
        o += wdt
    dq, dk, dv, diq, dik, diw, dg = parts
    cq, ck, cv, cg = (cols(o + i * DIL_WIDTH, DIL_WIDTH) for i in range(4))
    o += 4 * DIL_WIDTH
    pair = [t * LANES + s for t in range(DIL_WIDTH // LANES) for s in _dil_pair_lanes()]
    cq, ck = [cq[i] for i in pair], [ck[i] for i in pair]
    pad = [o] * (LANES - IDX_DIM - IDX_HEADS)
    src = r + k + v + g + cq + ck + cv + cg + dq + dg + diq + dk + dv + dik + diw + pad + wd + ad
    assert len(src) == Z_COLS
    return src


def _permute_in_cols(w):
    return jnp.take(w, jnp.asarray(_in_col_sources(), I32), axis=-1, mode="fill", fill_value=0)


def kernel(x, p, norm_g, w_in, tshift_mu, rwkv_w0, rwkv_w_up, rwkv_a0, rwkv_a_up, rwkv_k_k, rwkv_k_a, rwkv_r_k,
           rwkv_ln_g, rwkv_ln_b, w_out, ple_norm_g, ple_w_gate, ple_w_proj, final_norm_g):
    batch, seq, d_model = x.shape
    depth = w_in.shape[0]
    n = batch * seq
    assert seq % (ATT_BLOCK * max(DIL_RATES)) == 0 and seq >= 4 * DSA_TOPK

    c64, s64 = _rope_tables(seq, HEAD_DIM, HEAD_DIM // 8)
    c32, s32 = _rope_tables(seq, IDX_DIM, IDX_DIM // 8)
    c64p, s64p = _rope_tables_paired(seq)
    w_in_k = _permute_in_cols(w_in.astype(BF16))
    rw = RWKV_WIDTH
    row = lambda a: a.reshape(1, -1)

    x2d = x.reshape(n, d_model)
    p3d = p.reshape(depth, n, p.shape[-1])
    for i in range(depth):
        z = _in_proj(x2d, row(norm_g[i]), w_in_k, i)
        mu = tshift_mu[i]
        y_a = _rwkv(z.reshape(batch, seq, Z_COLS), mu[:4 * rw].reshape(4, rw), row(mu[4 * rw:]), row(rwkv_w0[i]), rwkv_w_up[i],
                    row(rwkv_a0[i]), rwkv_a_up[i], row(rwkv_k_k[i]), row(rwkv_k_a[i]), row(rwkv_r_k[i]),
                    row(rwkv_ln_g[i]), row(rwkv_ln_b[i])).reshape(n, rw)
        y_b = _dsa(z, batch, seq, c64, s64, c32, s32)
        y_c = _dilated(z, batch, seq, c64p, s64p)
        wo = w_out[i].astype(BF16)
        x2d = _out_proj(x2d, y_a, y_b, y_c, p3d, i, wo[:rw], wo[rw:rw + DSA_WIDTH],
                        wo[rw + DSA_WIDTH:], row(ple_norm_g[i]), ple_w_gate[i].astype(BF16),
                        ple_w_proj[i].astype(BF16), row(final_norm_g), final=(i == depth - 1))
    return x2d.reshape(batch, seq, d_model)
```

```python
import functools
import math

import jax
import jax.numpy as jnp
from jax import lax
from jax.experimental import pallas as pl
from jax.experimental.pallas import tpu as pltpu

F32 = jnp.float32
BF16 = jnp.bfloat16
I32 = jnp.int32

HEAD_DIM = 64
ROPE_THETA = 500000.0
NORM_EPS = 1e-6
RWKV_HEADS = 6
RWKV_WIDTH = RWKV_HEADS * HEAD_DIM
LORA_RANK = 64
GN_EPS = 64e-5
DECAY_SCALE = math.exp(-0.5)
DSA_HEADS = 4
DSA_WIDTH = DSA_HEADS * HEAD_DIM
IDX_HEADS = 8
IDX_DIM = 32
DSA_TOPK = 256
DIL_RATES = (1, 4, 16)
DIL_STEPS = 128
DIL_UNROLL = 4
DIL_WIDTH = 384
ATT_BLOCK = 128
DSA_QUERY_BLOCK = 256
DSA_KEY_BLOCK = 512

LANES = 128
MXU_WIDTH = 256
VMEM_LIMIT = 48 * 1024 * 1024

U_RW = 0
U_DL = 12
U_DS_Q = 24
U_DS_G = 26
U_DS_IQ = 28
U_DS_KV = 30
U_DS_MISC = 31
U_RW_WA = 32
Z_UNITS = 33
Z_COLS = Z_UNITS * LANES

RWKV_CHUNK = 128
RWKV_BATCH_ROWS = 4
INV_BASE = 16
QK_SCALE_LOG2 = HEAD_DIM ** -0.5 * math.log2(math.e)
NEG_BIG = -1e30
BIT_GROUP_ROWS = 256
INT_MIN = -2 ** 31


def _dot(a, b):
    return jnp.dot(a, b, preferred_element_type=F32)


def _dot_nt(a, b):
    return lax.dot_general(a, b, (((1,), (1,)), ((), ())), preferred_element_type=F32)


def _split(x):
    hi = x.astype(BF16)
    lo = (x - hi.astype(F32)).astype(BF16)
    return hi, lo


def _head_sum(x, head_ones):
    return _dot(x.astype(BF16), head_ones)


def _dot_x3(x, w):
    xh, xl = _split(x)
    wh, wl = _split(w)
    return _dot(xh, wh) + (_dot(xh, wl) + _dot(xl, wh))


def _sigmoid(x):
    return 1.0 / (1.0 + jnp.exp(-x))


def _rope(x, cos_t, sin_t, half, head):
    lane = lax.broadcasted_iota(I32, x.shape, 1) % head
    partner = jnp.where(lane < half, pltpu.roll(x, LANES - half, 1), pltpu.roll(x, half, 1))
    return x * cos_t + partner * sin_t


def _in_proj_kernel(x_ref, g_ref, w_ref, z_ref, *, tn):
    x = x_ref[...]
    ms = jnp.mean(x * x, axis=-1, keepdims=True)
    h = (x * lax.rsqrt(ms + NORM_EPS) * g_ref[...]).astype(BF16)
    for c in range(0, Z_COLS, tn):
        wdt = min(tn, Z_COLS - c)
        z_ref[:, c:c + wdt] = _dot(h, w_ref[:, c:c + wdt])


def _in_proj(x2d, g, w_all, layer, tm=512, tn=MXU_WIDTH):
    n, d = x2d.shape
    return pl.pallas_call(
        functools.partial(_in_proj_kernel, tn=tn),
        grid=(n // tm,),
        in_specs=[
            pl.BlockSpec((tm, d), lambda i: (i, 0)),
            pl.BlockSpec((1, d), lambda i: (0, 0)),
            pl.BlockSpec((None, d, Z_COLS), lambda i: (layer, 0, 0)),
        ],
        out_specs=pl.BlockSpec((tm, Z_COLS), lambda i: (i, 0)),
        out_shape=jax.ShapeDtypeStruct((n, Z_COLS), F32),
        compiler_params=pltpu.CompilerParams(
            dimension_semantics=("parallel",), vmem_limit_bytes=VMEM_LIMIT),
        name="in_proj",
    )(x2d, g, w_all)


def _out_proj_kernel(x_ref, ya_ref, yb_ref, yc_ref, p_ref, wa_ref, wb_ref, wc_ref, ng_ref, wg_ref,
                     wp_ref, fg_ref, o_ref, *, final):
    acc = x_ref[...] + _dot(ya_ref[...], wa_ref[...])
    acc = acc + _dot(yb_ref[...], wb_ref[...])
    acc = acc + _dot(yc_ref[...], wc_ref[...])
    ms = jnp.mean(acc * acc, axis=-1, keepdims=True)
    hn = (acc * lax.rsqrt(ms + NORM_EPS) * ng_ref[...]).astype(BF16)
    gate = _sigmoid(_dot(hn, wg_ref[...]))
    out = acc + gate * _dot(p_ref[...].astype(BF16), wp_ref[...])
    if final:
        ms = jnp.mean(out * out, axis=-1, keepdims=True)
        out = out * lax.rsqrt(ms + NORM_EPS) * fg_ref[...]
    o_ref[...] = out


def _out_proj(x2d, ya, yb, yc, p3d, layer, wa, wb, wc, ng, wg, wp, fg, final, tm=1024):
    n, d = x2d.shape
    row = lambda w: pl.BlockSpec((tm, w), lambda i: (i, 0))
    full = lambda a: pl.BlockSpec(a.shape, lambda i: (0, 0))
    return pl.pallas_call(
        functools.partial(_out_proj_kernel, final=final),
        grid=(n // tm,),
        in_specs=[row(d), row(ya.shape[1]), row(yb.shape[1]), row(yc.shape[1]),
                  pl.BlockSpec((None, tm, p3d.shape[2]), lambda i: (layer, i, 0)),
                  full(wa), full(wb), full(wc), full(ng), full(wg), full(wp), full(fg)],
        out_specs=row(d),
        out_shape=jax.ShapeDtypeStruct((n, d), F32),
        compiler_params=pltpu.CompilerParams(
            dimension_semantics=("parallel",), vmem_limit_bytes=VMEM_LIMIT),
        name="out_proj",
    )(x2d, ya, yb, yc, p3d, wa, wb, wc, ng, wg, wp, fg)


def _unit_lower_inverse(n_list, row, col):
    c = n_list[0].shape[0]
    eye = jnp.where(row == col, 1.0, 0.0).astype(F32)
    same = (row // INV_BASE) == (col // INV_BASE)
    pw = [jnp.where(same, n, 0.0) for n in n_list]
    t = [eye - x for x in pw]
    span = 2
    while span < INV_BASE:
        pwb = [x.astype(BF16) for x in pw]
        pw = [_dot(x, x) for x in pwb]
        t = [x + _dot(x.astype(BF16), y.astype(BF16)) for x, y in zip(t, pw)]
        span *= 2
    size = INV_BASE
    while size < c:
        lower_left = ((row // (2 * size)) == (col // (2 * size))) & ((row // size) != (col // size))
        tb = [x.astype(BF16) for x in t]
        et = [_dot(jnp.where(lower_left, n, 0.0).astype(BF16), x).astype(BF16) for n, x in zip(n_list, tb)]
        t = [x - _dot(xb, y) for x, xb, y in zip(t, tb, et)]
        size *= 2
    return t


def _rwkv_kernel(zr_ref, zk_ref, zv_ref, zg_ref, zwa_ref, mu_ref, muwa_ref, w0_ref, wup_ref, a0_ref,
                 aup_ref, kk_ref, ka_ref, rk_ref, lng_ref, lnb_ref, o_ref,
                 state_ref, prev_ref, prevwa_ref, y_ref):
    nb, c = zr_ref.shape[0], zr_ref.shape[1]
    width = RWKV_WIDTH

    @pl.when(pl.program_id(1) == 0)
    def _():
        state_ref[...] = jnp.zeros_like(state_ref)
        prev_ref[...] = jnp.zeros_like(prev_ref)
        prevwa_ref[...] = jnp.zeros_like(prevwa_ref)

    first_row = lax.broadcasted_iota(I32, (c, width), 0) == 0
    first_row_wa = lax.broadcasted_iota(I32, (c, LANES), 0) == 0
    hrow = lax.broadcasted_iota(I32, (width, width), 0) // HEAD_DIM
    hcol = lax.broadcasted_iota(I32, (width, width), 1) // HEAD_DIM
    head_ones = jnp.where(hrow == hcol, 1.0, 0.0).astype(BF16)
    row = lax.broadcasted_iota(I32, (c, c), 0)
    col = lax.broadcasted_iota(I32, (c, c), 1)
    tri = jnp.where(col <= row, 1.0, 0.0).astype(BF16)
    strict = col < row
    incl = col <= row
    hs = [slice(h * HEAD_DIM, (h + 1) * HEAD_DIM) for h in range(RWKV_HEADS)]

    def prepare(bi):
        def shifted(z_ref, idx):
            z = z_ref[bi]
            prev = jnp.where(first_row, prev_ref[bi, idx, 0:1, :], pltpu.roll(z, 1, 0))
            prev_ref[bi, idx, 0:1, :] = z[c - 1:c, :]
            return z + (prev - z) * mu_ref[idx:idx + 1, :]

        r = shifted(zr_ref, 0)
        k = shifted(zk_ref, 1)
        v = shifted(zv_ref, 2)
        g = shifted(zg_ref, 3)
        zwa = zwa_ref[bi]
        prev_wa = jnp.where(first_row_wa, prevwa_ref[bi, 0:1, :], pltpu.roll(zwa, 1, 0))
        prevwa_ref[bi, 0:1, :] = zwa[c - 1:c, :]
        wa = zwa + (prev_wa - zwa) * muwa_ref[...]
        wd = wa[:, :LORA_RANK]
        ad = wa[:, LORA_RANK:]

        lw = -DECAY_SCALE * _sigmoid(w0_ref[...] + _dot_x3(jnp.tanh(wd), wup_ref[...]))
        a = _sigmoid(a0_ref[...] + _dot_x3(ad, aup_ref[...]))
        kk = k * kk_ref[...]
        norm = jnp.sqrt(_head_sum(kk * kk, head_ones))
        kk = kk / jnp.maximum(norm, 1e-12)
        kmod = k * (1.0 + (a - 1.0) * ka_ref[...])
        b = a * kk

        lw_hi, lw_lo = _split(lw)
        cl = _dot(tri, lw_hi) + _dot(tri, lw_lo)
        c_mid = cl[c // 2 - 1:c // 2, :]
        c_end = cl[c - 1:c, :]
        w = jnp.exp(lw)
        e_in1 = jnp.exp(cl - lw)
        e_in2 = e_in1 * w
        e_k = jnp.exp(c_mid - cl)
        e_mid_inv = jnp.exp(-c_mid)
        e_end = e_k * jnp.exp(c_end - c_mid)
        ops = dict(
            q1=kk * e_in1 * e_mid_inv, q2=r * e_in2 * e_mid_inv, k1=b * e_k, k2=kmod * e_k,
            kkd=kk * e_in1, rd=r * e_in2, bd=b * e_end, kd=kmod * e_end, vb=v)
        ops = {name: [x.astype(BF16)[:, s] for s in hs] for name, x in ops.items()}
        return dict(ops, r=r, kmod=kmod, v=v, g=g, d_end=jnp.exp(c_end))

    prep = [prepare(bi) for bi in range(nb)]

    probs = [(bi, h) for bi in range(nb) for h in range(RWKV_HEADS)]
    op = lambda name: [prep[bi][name][h] for bi, h in probs]
    q1, q2, k1, k2, kkd, rd, bd, kd, vh = map(op, ("q1", "q2", "k1", "k2", "kkd", "rd", "bd", "kd", "vb"))
    n = range(len(probs))
    qk = [_dot_nt(jnp.concatenate([q1[i], q2[i]], axis=0), jnp.concatenate([k1[i], k2[i]], axis=0)) for i in n]
    a_ab = [jnp.where(strict, x[:c, :c], 0.0) for x in qk]
    t_inv = _unit_lower_inverse(a_ab, row, col)
    a_rb = [jnp.where(incl, x[c:, :c], 0.0).astype(BF16) for x in qk]
    a_v = [jnp.concatenate([jnp.where(strict, x[:c, c:], 0.0), jnp.where(incl, x[c:, c:], 0.0)],
                           axis=0).astype(BF16) for x in qk]
    s_old = [state_ref[bi, h] for bi, h in probs]
    s_in = [s.astype(BF16) for s in s_old]
    from_v = [_dot(a_v[i], vh[i]) for i in n]
    from_s = [_dot_nt(jnp.concatenate([kkd[i], rd[i]], axis=0), s_in[i]) for i in n]
    rhs = [from_s[i][:c] + from_v[i][:c] for i in n]
    u = [-_dot(t_inv[i].astype(BF16), rhs[i].astype(BF16)) for i in n]
    uv_t = [jnp.concatenate([u[i], prep[bi]["v"][:, hs[h]]], axis=1).T.astype(BF16)
            for i, (bi, h) in enumerate(probs)]
    for i, (bi, h) in enumerate(probs):
        state_ref[bi, h] = (s_old[i] * prep[bi]["d_end"][:, hs[h]]
                            + _dot(uv_t[i][:HEAD_DIM, :], bd[i]) + _dot(uv_t[i][HEAD_DIM:, :], kd[i]))
    for i, (bi, h) in enumerate(probs):
        y_ref[bi, :, hs[h]] = from_s[i][c:] + _dot(a_rb[i], u[i].astype(BF16)) + from_v[i][c:]

    inv_n = 1.0 / HEAD_DIM
    for bi in range(nb):
        pb = prep[bi]
        y = y_ref[bi]
        mean = _head_sum(y, head_ones) * inv_n
        dlt = y - mean
        var = _head_sum(dlt * dlt, head_ones) * inv_n
        yn = dlt * lax.rsqrt(var + GN_EPS) * lng_ref[...] + lnb_ref[...]
        bonus = _head_sum(pb["r"] * pb["kmod"] * rk_ref[...], head_ones) * pb["v"]
        g = pb["g"]
        o_ref[bi] = ((yn + bonus) * (g * _sigmoid(g))).astype(o_ref.dtype)


def _rwkv(z3d, mu4, mu_wa, w0, w_up, a0, a_up, k_k, k_a, r_k, ln_g, ln_b):
    batch, seq, _ = z3d.shape
    c = RWKV_CHUNK
    nb = RWKV_BATCH_ROWS if batch % RWKV_BATCH_ROWS == 0 else 1
    zspec = lambda u: pl.BlockSpec((nb, c, RWKV_WIDTH), lambda b, i, u=u: (b, i, u))
    full = lambda a: pl.BlockSpec(a.shape, lambda b, i: (0,) * a.ndim)
    params = (mu4, mu_wa, w0, w_up, a0, a_up, k_k, k_a, r_k, ln_g, ln_b)
    return pl.pallas_call(
        _rwkv_kernel,
        grid=(batch // nb, seq // c),
        in_specs=[zspec(0), zspec(1), zspec(2), zspec(3),
                  pl.BlockSpec((nb, c, LANES), lambda b, i: (b, i, U_RW_WA))]
                 + [full(a) for a in params],
        out_specs=pl.BlockSpec((nb, c, RWKV_WIDTH), lambda b, i: (b, i, 0)),
        out_shape=jax.ShapeDtypeStruct((batch, seq, RWKV_WIDTH), BF16),
        scratch_shapes=[
            pltpu.VMEM((nb, RWKV_HEADS, HEAD_DIM, HEAD_DIM), F32),
            pltpu.VMEM((nb, 4, 8, RWKV_WIDTH), F32),
            pltpu.VMEM((nb, 8, LANES), F32),
            pltpu.VMEM((nb, c, RWKV_WIDTH), F32),
        ],
        compiler_params=pltpu.CompilerParams(
            dimension_semantics=("parallel", "arbitrary"), vmem_limit_bytes=VMEM_LIMIT),
        name="rwkv7",
    )(z3d, z3d, z3d, z3d, z3d, *params)


def _bit_transpose32(words):
    a = list(words)
    j, mask = 16, 0x0000FFFF
    while j:
        for k in range(32):
            if k & j == 0:
                t = (a[k] ^ lax.shift_right_logical(a[k + j], jnp.int32(j))) & jnp.int32(mask)
                a[k] = a[k] ^ t
                a[k + j] = a[k + j] ^ (t << j)
        j >>= 1
        mask ^= mask << j
    return a


def _dsa_kernel(*refs):
    seq = refs[3].shape[0]
    lax.fori_loop(0, seq // DSA_QUERY_BLOCK, lambda qb, carry: (_dsa_query_block(qb, *refs), carry)[1], 0)


def _dsa_query_block(qb, zq_ref, zg_ref, ziq_ref, zkv_ref, zms_ref, c64_ref, s64_ref, c32_ref, s32_ref, o_ref,
                     k_ref, vt_ref, ik_ref, key_ref, plane_ref):
    seq = zkv_ref.shape[0]
    blk = DSA_QUERY_BLOCK
    sub_rows = ATT_BLOCK
    kb_rows = DSA_KEY_BLOCK
    n_kb = (qb * blk) // kb_rows + 1

    @pl.when(qb == 0)
    def _():
        lane = lax.broadcasted_iota(I32, (kb_rows, LANES), 1)

        def body(t, carry):
            rows = pl.ds(pl.multiple_of(t * kb_rows, kb_rows), kb_rows)
            kv = zkv_ref[rows, :]
            kvr = jnp.where(lane < HEAD_DIM, _rope(kv, c64_ref[rows, :], s64_ref[rows, :], 8, HEAD_DIM), kv)
            k_ref[rows, :] = kvr[:, :HEAD_DIM].astype(BF16)
            vt_ref[t] = kvr.T[HEAD_DIM:, :].astype(BF16)
            ms = zms_ref[rows, :]
            ikr = _rope(ms, c32_ref[rows, :], s32_ref[rows, :], 4, IDX_DIM)
            ik_ref[rows, :] = ikr[:, :IDX_DIM].astype(BF16)
            return carry

        lax.fori_loop(0, seq // kb_rows, body, 0)

    r0 = pl.multiple_of(qb * blk, blk)
    qrows = pl.ds(r0, blk)
    c64 = c64_ref[qrows, :]
    s64 = s64_ref[qrows, :]
    c32 = c32_ref[qrows, :]
    s32 = s32_ref[qrows, :]
    q2 = [_rope(zq_ref[qrows, i * LANES:(i + 1) * LANES], c64, s64, 8, HEAD_DIM) * QK_SCALE_LOG2
          for i in range(2)]
    q_stack = jnp.concatenate(
        [q2[h // 2][:, (h % 2) * HEAD_DIM:(h % 2 + 1) * HEAD_DIM] for h in range(DSA_HEADS)],
        axis=0).astype(BF16)
    iq2 = [_rope(ziq_ref[qrows, i * LANES:(i + 1) * LANES], c32, s32, 4, IDX_DIM) for i in range(2)]
    iq_stack = jnp.concatenate(
        [iq2[h // 4][:, (h % 4) * IDX_DIM:(h % 4 + 1) * IDX_DIM] for h in range(IDX_HEADS)],
        axis=0).astype(BF16)
    iw_t = zms_ref[qrows, :].T[IDX_DIM:IDX_DIM + IDX_HEADS, :] * ((IDX_HEADS * IDX_DIM) ** -0.5)
    iw_row = jnp.concatenate([iw_t[h:h + 1, :] for h in range(IDX_HEADS)], axis=1)

    kpos_l = lax.broadcasted_iota(I32, (kb_rows, blk), 0)
    qpos = r0 + lax.broadcasted_iota(I32, (kb_rows, blk), 1)

    tri = jnp.where(lax.broadcasted_iota(I32, (sub_rows, sub_rows), 1)
                    <= lax.broadcasted_iota(I32, (sub_rows, sub_rows), 0), 1.0, 0.0).astype(BF16)
    g = zg_ref[qrows, :]
    gate = g * _sigmoid(g)

    def step(n):
        rows = [slice(kb * kb_rows, (kb + 1) * kb_rows) for kb in range(n)]
        groups = kb_rows // BIT_GROUP_ROWS
        causal = (n - 1) * kb_rows + kpos_l <= qpos

        for kb in range(n):
            s = jnp.maximum(_dot_nt(ik_ref[rows[kb], :], iq_stack), 0.0) * iw_row
            sc = s[:, 0:blk]
            for h in range(1, IDX_HEADS):
                sc = sc + s[:, h * blk:(h + 1) * blk]
            if kb == n - 1:
                sc = jnp.where(causal, sc, -jnp.inf)
            bits = pltpu.bitcast(sc, I32)
            key = bits ^ ((bits >> 31) & 0x7FFFFFFF)
            key_ref[rows[kb], :] = key
            ku = key ^ INT_MIN
            for grp in range(groups):
                words = [ku[(grp * 32 + v) * 8:(grp * 32 + v + 1) * 8, :] for v in range(32)]
                for b, plane in enumerate(_bit_transpose32(words)):
                    plane_ref[kb * groups + grp, b] = plane

        n_groups = n * groups

        def bit_body(i, carry):
            active, k_rem, thr_u = carry
            ones = [active[g] & plane_ref[g, i] for g in range(n_groups)]
            acc = lax.population_count(ones[0])
            for g in range(1, n_groups):
                acc = acc + lax.population_count(ones[g])
            cnt = acc.sum(axis=0, keepdims=True)
            dec = cnt >= k_rem
            k_rem = jnp.where(dec, k_rem, k_rem - cnt)
            thr_u = thr_u | jnp.where(dec, jnp.int32(1) << (31 - i), 0)
            dec8 = jnp.broadcast_to(dec, (8, blk))
            active = tuple(jnp.where(dec8, ones[g], active[g] ^ ones[g]) for g in range(n_groups))
            return active, k_rem, thr_u

        def search():
            init = (tuple(jnp.full((8, blk), -1, I32) for _ in range(n_groups)),
                    jnp.full((1, blk), DSA_TOPK, I32), jnp.zeros((1, blk), I32))
            _, k_rem, thr_u = lax.fori_loop(0, 32, bit_body, init)
            return thr_u ^ INT_MIN, k_rem

        keep_all = lambda: (jnp.full((1, blk), INT_MIN, I32), jnp.zeros((1, blk), I32))
        if n * kb_rows - blk + 1 > DSA_TOPK:
            thr, need = search()
        else:
            thr, need = lax.cond(qb * blk >= DSA_TOPK, search, keep_all)
        need = need.astype(F32)

        wide = DSA_HEADS * blk
        m = jnp.full((1, wide), NEG_BIG, F32)
        l = jnp.zeros((1, wide), F32)
        acc = jnp.zeros((HEAD_DIM, wide), F32)
        run = jnp.zeros((1, blk), F32)
        for kb in range(n):
            keys = key_ref[rows[kb], :]
            eq = keys == thr
            eq_b = jnp.where(eq, 1.0, 0.0).astype(BF16)
            local = [_dot(tri, eq_b[sub * sub_rows:(sub + 1) * sub_rows, :]) for sub in range(kb_rows // sub_rows)]
            ranks = []
            for x in local:
                ranks.append(x + run)
                run = run + x[sub_rows - 1:sub_rows, :]
            rank = jnp.concatenate(ranks, axis=0)
            take = (keys > thr) | (eq & (rank <= need))
            if kb == n - 1:
                take = take & causal
            bias = jnp.where(take, 0.0, NEG_BIG)
            s = _dot_nt(k_ref[rows[kb], :], q_stack)
            s = s + jnp.concatenate([bias] * DSA_HEADS, axis=1)
            m_new = jnp.maximum(m, jnp.max(s, axis=0, keepdims=True))
            p = jnp.exp2(s - m_new)
            alpha = jnp.exp2(m - m_new)
            l = alpha * l + jnp.sum(p, axis=0, keepdims=True)
            acc = alpha * acc + _dot(vt_ref[kb], p.astype(BF16))
            m = m_new

        o_t = acc / l
        o = jnp.concatenate(
            [jnp.concatenate([o_t[:, (2 * i) * blk:(2 * i + 1) * blk], o_t[:, (2 * i + 1) * blk:(2 * i + 2) * blk]],
                             axis=0).T for i in range(DSA_HEADS // 2)], axis=1)
        o_ref[qrows, :] = (o * gate).astype(o_ref.dtype)

    for n in range(1, seq // kb_rows + 1):
        pl.when(n_kb == n)(functools.partial(step, n))


def _dsa(z2d, batch, seq, c64, s64, c32, s32):
    blk = DSA_QUERY_BLOCK
    qspec = lambda u2: pl.BlockSpec((seq, 2 * LANES), lambda b, u2=u2: (b, u2))
    seqspec = lambda u: pl.BlockSpec((seq, LANES), lambda b, u=u: (b, u))
    tab = pl.BlockSpec((seq, LANES), lambda b: (0, 0))
    return pl.pallas_call(
        _dsa_kernel,
        grid=(batch,),
        in_specs=[qspec(U_DS_Q // 2), qspec(U_DS_G // 2), qspec(U_DS_IQ // 2),
                  seqspec(U_DS_KV), seqspec(U_DS_MISC), tab, tab, tab, tab],
        out_specs=pl.BlockSpec((seq, DSA_WIDTH), lambda b: (b, 0)),
        out_shape=jax.ShapeDtypeStruct((batch * seq, DSA_WIDTH), BF16),
        scratch_shapes=[
            pltpu.VMEM((seq, HEAD_DIM), BF16),
            pltpu.VMEM((seq // DSA_KEY_BLOCK, HEAD_DIM, DSA_KEY_BLOCK), BF16),
            pltpu.VMEM((seq, IDX_DIM), BF16),
            pltpu.VMEM((seq, blk), I32),
            pltpu.VMEM((seq // BIT_GROUP_ROWS, 32, 8, blk), I32),
        ],
        compiler_params=pltpu.CompilerParams(
            dimension_semantics=("parallel",), vmem_limit_bytes=VMEM_LIMIT),
        name="dsa",
    )(z2d, z2d, z2d, z2d, z2d, c64, s64, c32, s32)


def _dil_group(gi, rate, q_ref, k_ref, v_ref, g_ref, c64_ref, s64_ref, qr_ref, kr_ref, o_s, lse_s):
    seq = q_ref.shape[0]
    blk = ATT_BLOCK
    rope_rows = 4 * blk
    nrow = seq // rope_rows

    def rope_body(t, carry):
        rows = pl.ds(pl.multiple_of(t * rope_rows, rope_rows), rope_rows)
        ct = c64_ref[rows, :]
        st = s64_ref[rows, :]
        q = q_ref[rows, :]
        k = k_ref[rows, :]
        qr_ref[rows, :] = (q * ct + pltpu.roll(q, HEAD_DIM, 1) * st) * QK_SCALE_LOG2
        kr_ref[rows, :] = k * ct + pltpu.roll(k, HEAD_DIM, 1) * st
        return carry

    lax.fori_loop(0, nrow, rope_body, 0)

    nblk = seq // (blk * rate)
    qi = lax.broadcasted_iota(I32, (blk, 2 * blk), 0)
    kj = lax.broadcasted_iota(I32, (blk, 2 * blk), 1)
    band = (kj >= qi) & (kj <= qi + DIL_STEPS)

    def strided(ref, start):
        if rate == 1:
            return ref[pl.ds(start, blk), :]
        return ref[pl.ds(start, blk, stride=rate), :]

    def store(ref, start, val):
        if rate == 1:
            ref[gi, pl.ds(start, blk), :] = val
        else:
            ref[gi, pl.ds(start, blk, stride=rate), :] = val

    first_q = lax.broadcasted_iota(I32, (blk, LANES), 1) < HEAD_DIM
    first_kv = lax.broadcasted_iota(I32, (2 * blk, LANES), 1) < HEAD_DIM
    lane64 = lax.broadcasted_iota(I32, (blk, LANES), 1) % HEAD_DIM
    first_qk = (lane64 < 8) | ((lane64 >= 16) & (lane64 < 40))
    unroll = DIL_UNROLL

    def body(it, carry):
        starts, qs, ks, vs, masks = [], [], [], [], []
        for u in range(unroll):
            cidx = it * unroll + u
            rho = cidx // nblk
            j = cidx % nblk
            start = j * (blk * rate) + rho
            pstart = jnp.maximum(j - 1, 0) * (blk * rate) + rho
            starts.append(start)
            qs.append(strided(qr_ref, start))
            ks.append(jnp.concatenate([strided(kr_ref, pstart), strided(kr_ref, start)], axis=0).astype(BF16))
            vs.append(jnp.concatenate([strided(v_ref, pstart), strided(v_ref, start)], axis=0))
            masks.append(band & ((kj >= blk) | (j > 0)))
        prob = [(u, h) for u in range(unroll) for h in range(2)]
        qh = [jnp.where(first_qk == (h == 0), qs[u], 0.0).astype(BF16) for u, h in prob]
        vh = [jnp.where(first_kv == (h == 0), vs[u], 1.0).astype(BF16) for u, h in prob]
        s = [jnp.where(masks[u], _dot_nt(qh[i], ks[u]), -jnp.inf) for i, (u, h) in enumerate(prob)]
        m = [jnp.max(x, axis=-1, keepdims=True) for x in s]
        r = [_dot(jnp.exp2(s[i] - m[i]).astype(BF16), vh[i]) for i in range(len(prob))]
        for u in range(unroll):
            r0, r1 = r[2 * u], r[2 * u + 1]
            o = jnp.where(first_q, r0, r1)
            l = pltpu.roll(jnp.where(first_q, r1, r0), HEAD_DIM, 1)
            lse = jnp.where(first_q, m[2 * u], m[2 * u + 1]) * math.log(2.0) + jnp.log(l)
            gate = strided(g_ref, starts[u])
            store(o_s, starts[u], o / l * (gate * _sigmoid(gate)))
            store(lse_s, starts[u], lse)
        return carry

    lax.fori_loop(0, rate * nblk // unroll, body, 0)


def _dil_kernel(q_ref, k_ref, v_ref, g_ref, c64_ref, s64_ref, o_ref, qr_ref, kr_ref, o_s, lse_s):
    grp = pl.program_id(1)
    for gi, rate in enumerate(DIL_RATES):
        pl.when(grp == gi)(functools.partial(
            _dil_group, gi, rate, q_ref, k_ref, v_ref, g_ref, c64_ref, s64_ref, qr_ref, kr_ref, o_s, lse_s))

    @pl.when(grp == len(DIL_RATES) - 1)
    def _():
        blk = ATT_BLOCK

        def body(t, carry):
            rows = pl.ds(pl.multiple_of(t * blk, blk), blk)
            ls = [lse_s[i, rows, :] for i in range(3)]
            top = jnp.maximum(jnp.maximum(ls[0], ls[1]), ls[2])
            es = [jnp.exp(x - top) for x in ls]
            inv = 1.0 / (es[0] + es[1] + es[2])
            o_ref[rows, :] = jnp.concatenate(
                [o_s[i, rows, :] * (es[i] * inv) for i in range(3)], axis=1).astype(o_ref.dtype)
            return carry

        lax.fori_loop(0, q_ref.shape[0] // blk, body, 0)


def _dilated(z2d, batch, seq, c64, s64):
    ng = len(DIL_RATES)
    zspec = lambda base: pl.BlockSpec((seq, LANES), lambda b, g, base=base: (b, base + g))
    tab = pl.BlockSpec((seq, LANES), lambda b, g: (0, 0))
    return pl.pallas_call(
        _dil_kernel,
        grid=(batch, ng),
        in_specs=[zspec(U_DL), zspec(U_DL + 3), zspec(U_DL + 6), zspec(U_DL + 9), tab, tab],
        out_specs=pl.BlockSpec((seq, DIL_WIDTH), lambda b, g: (b, 0)),
        out_shape=jax.ShapeDtypeStruct((batch * seq, DIL_WIDTH), BF16),
        scratch_shapes=[
            pltpu.VMEM((seq, LANES), F32),
            pltpu.VMEM((seq, LANES), F32),
            pltpu.VMEM((ng, seq, LANES), F32),
            pltpu.VMEM((ng, seq, LANES), F32),
        ],
        compiler_params=pltpu.CompilerParams(
            dimension_semantics=("parallel", "arbitrary"), vmem_limit_bytes=VMEM_LIMIT),
        name="dilated",
    )(z2d, z2d, z2d, z2d, c64, s64)


def _rope_tables(seq, head, half):
    inv = ROPE_THETA ** (-jnp.arange(half, dtype=F32) / half)
    ang = jnp.arange(seq, dtype=F32)[:, None] * inv[None, :]
    cos, sin = jnp.cos(ang), jnp.sin(ang)
    lane = jnp.arange(LANES) % head
    idx = lane % half
    cos_t = jnp.where(lane[None, :] < 2 * half, cos[:, idx], 1.0)
    sin_t = jnp.where(lane[None, :] < half, -sin[:, idx], jnp.where(lane[None, :] < 2 * half, sin[:, idx], 0.0))
    return cos_t, sin_t


def _dil_pair_lanes():
    half, rest = HEAD_DIM // 8, (HEAD_DIM - HEAD_DIM // 4) // 2
    src = []
    for part in range(2):
        for head in range(2):
            src += [head * HEAD_DIM + part * half + i for i in range(half)]
        for head in range(2):
            src += [head * HEAD_DIM + 2 * half + part * rest + i for i in range(rest)]
    return src


def _rope_tables_paired(seq):
    half = HEAD_DIM // 8
    inv = ROPE_THETA ** (-jnp.arange(half, dtype=F32) / half)
    ang = jnp.arange(seq, dtype=F32)[:, None] * inv[None, :]
    cos, sin = jnp.cos(ang), jnp.sin(ang)
    lane = jnp.arange(LANES)[None, :]
    idx = jnp.arange(LANES) % half
    rotated = lane % HEAD_DIM < 2 * half
    cos_t = jnp.where(rotated, cos[:, idx], 1.0)
    sin_t = jnp.where(rotated, jnp.where(lane < HEAD_DIM, -sin[:, idx], sin[:, idx]), 0.0)
    return cos_t, sin_t


def _in_col_sources():
    rw = RWKV_WIDTH
    cols = lambda o, wdt: list(range(o, o + wdt))
    o = 0
    r, k, v, g = (cols(o + i * rw, rw) for i in range(4))
    o = 4 * rw
    wd, ad = cols(o, LORA_RANK), cols(o + LORA_RANK, LORA_RANK)
    o += 2 * LORA_RANK
    parts = []
    for wdt in (DSA_WIDTH, HEAD_DIM, HEAD_DIM, IDX_HEADS * IDX_DIM, IDX_DIM, IDX_HEADS, DSA_WIDTH):
        parts.append(cols(o, wdt))
        o += wdt
    dq, dk, dv, diq, dik, diw, dg = parts
    cq, ck, cv, cg = (cols(o + i * DIL_WIDTH, DIL_WIDTH) for i in range(4))
    o += 4 * DIL_WIDTH
    pair = [t * LANES + s for t in range(DIL_WIDTH // LANES) for s in _dil_pair_lanes()]
    cq, ck = [cq[i] for i in pair], [ck[i] for i in pair]
    pad = [o] * (LANES - IDX_DIM - IDX_HEADS)
    src = r + k + v + g + cq + ck + cv + cg + dq + dg + diq + dk + dv + dik + diw + pad + wd + ad
    assert len(src) == Z_COLS
    return src


def _permute_in_cols(w):
    return jnp.take(w, jnp.asarray(_in_col_sources(), I32), axis=-1, mode="fill", fill_value=0)


def kernel(x, p, norm_g, w_in, tshift_mu, rwkv_w0, rwkv_w_up, rwkv_a0, rwkv_a_up, rwkv_k_k, rwkv_k_a, rwkv_r_k,
           rwkv_ln_g, rwkv_ln_b, w_out, ple_norm_g, ple_w_gate, ple_w_proj, final_norm_g):
    batch, seq, d_model = x.shape
    depth = w_in.shape[0]
    n = batch * seq
    assert seq % (ATT_BLOCK * max(DIL_RATES)) == 0 and seq >= 4 * DSA_TOPK

    c64, s64 = _rope_tables(seq, HEAD_DIM, HEAD_DIM // 8)
    c32, s32 = _rope_tables(seq, IDX_DIM, IDX_DIM // 8)
    c64p, s64p = _rope_tables_paired(seq)
    w_in_k = _permute_in_cols(w_in.astype(BF16))
    rw = RWKV_WIDTH
    row = lambda a: a.reshape(1, -1)

    x2d = x.reshape(n, d_model)
    p3d = p.reshape(depth, n, p.shape[-1])
    for i in range(depth):
        z = _in_proj(x2d, row(norm_g[i]), w_in_k, i)
        mu = tshift_mu[i]
        y_a = _rwkv(z.reshape(batch, seq, Z_COLS), mu[:4 * rw].reshape(4, rw), row(mu[4 * rw:]), row(rwkv_w0[i]), rwkv_w_up[i],
                    row(rwkv_a0[i]), rwkv_a_up[i], row(rwkv_k_k[i]), row(rwkv_k_a[i]), row(rwkv_r_k[i]),
                    row(rwkv_ln_g[i]), row(rwkv_ln_b[i])).reshape(n, rw)
        y_b = _dsa(z, batch, seq, c64, s64, c32, s32)
        y_c = _dilated(z, batch, seq, c64p, s64p)
        wo = w_out[i].astype(BF16)
        x2d = _out_proj(x2d, y_a, y_b, y_c, p3d, i, wo[:rw], wo[rw:rw + DSA_WIDTH],
                        wo[rw + DSA_WIDTH:], row(ple_norm_g[i]), ple_w_gate[i].astype(BF16),
                        ple_w_proj[i].astype(BF16), row(final_norm_g), final=(i == depth - 1))
    return x2d.reshape(batch, seq, d_model)
```

```python
import functools
import math

import jax
import jax.numpy as jnp
from jax import lax
from jax.experimental import pallas as pl
from jax.experimental.pallas import tpu as pltpu

F32 = jnp.float32
BF16 = jnp.bfloat16
I32 = jnp.int32

HEAD_DIM = 64
ROPE_THETA = 500000.0
NORM_EPS = 1e-6
RWKV_HEADS = 6
RWKV_WIDTH = RWKV_HEADS * HEAD_DIM
LORA_RANK = 64
GN_EPS = 64e-5
DECAY_SCALE = math.exp(-0.5)
DSA_HEADS = 4
DSA_WIDTH = DSA_HEADS * HEAD_DIM
IDX_HEADS = 8
IDX_DIM = 32
DSA_TOPK = 256
DIL_RATES = (1, 4, 16)
DIL_STEPS = 128
DIL_UNROLL = 4
DIL_WIDTH = 384
ATT_BLOCK = 128
DSA_QUERY_BLOCK = 256
DSA_KEY_BLOCK = 512

LANES = 128
MXU_WIDTH = 256
VMEM_LIMIT = 48 * 1024 * 1024

U_RW = 0
U_DL = 12
U_DS_Q = 24
U_DS_G = 26
U_DS_IQ = 28
U_DS_KV = 30
U_DS_MISC = 31
U_RW_WA = 32
Z_UNITS = 33
Z_COLS = Z_UNITS * LANES

RWKV_CHUNK = 128
RWKV_BATCH_ROWS = 4
INV_BASE = 16
QK_SCALE_LOG2 = HEAD_DIM ** -0.5 * math.log2(math.e)
NEG_BIG = -1e30
BIT_GROUP_ROWS = 256
INT_MIN = -2 ** 31


def _dot(a, b):
    return jnp.dot(a, b, preferred_element_type=F32)


def _dot_nt(a, b):
    return lax.dot_general(a, b, (((1,), (1,)), ((), ())), preferred_element_type=F32)


def _split(x):
    hi = x.astype(BF16)
    lo = (x - hi.astype(F32)).astype(BF16)
    return hi, lo


def _head_sum(x, head_ones):
    return _dot(x.astype(BF16), head_ones)


def _dot_x3(x, w):
    xh, xl = _split(x)
    wh, wl = _split(w)
    return _dot(xh, wh) + (_dot(xh, wl) + _dot(xl, wh))


def _sigmoid(x):
    return 1.0 / (1.0 + jnp.exp(-x))


def _rope(x, cos_t, sin_t, half, head):
    lane = lax.broadcasted_iota(I32, x.shape, 1) % head
    partner = jnp.where(lane < half, pltpu.roll(x, LANES - half, 1), pltpu.roll(x, half, 1))
    return x * cos_t + partner * sin_t


def _in_proj_kernel(x_ref, g_ref, w_ref, z_ref, *, tn):
    x = x_ref[...]
    ms = jnp.mean(x * x, axis=-1, keepdims=True)
    h = (x * lax.rsqrt(ms + NORM_EPS) * g_ref[...]).astype(BF16)
    for c in range(0, Z_COLS, tn):
        wdt = min(tn, Z_COLS - c)
        z_ref[:, c:c + wdt] = _dot(h, w_ref[:, c:c + wdt])


def _in_proj(x2d, g, w_all, layer, tm=512, tn=MXU_WIDTH):
    n, d = x2d.shape
    return pl.pallas_call(
        functools.partial(_in_proj_kernel, tn=tn),
        grid=(n // tm,),
        in_specs=[
            pl.BlockSpec((tm, d), lambda i: (i, 0)),
            pl.BlockSpec((1, d), lambda i: (0, 0)),
            pl.BlockSpec((None, d, Z_COLS), lambda i: (layer, 0, 0)),
        ],
        out_specs=pl.BlockSpec((tm, Z_COLS), lambda i: (i, 0)),
        out_shape=jax.ShapeDtypeStruct((n, Z_COLS), F32),
        compiler_params=pltpu.CompilerParams(
            dimension_semantics=("parallel",), vmem_limit_bytes=VMEM_LIMIT),
        name="in_proj",
    )(x2d, g, w_all)


def _out_proj_kernel(x_ref, ya_ref, yb_ref, yc_ref, p_ref, wa_ref, wb_ref, wc_ref, ng_ref, wg_ref,
                     wp_ref, fg_ref, o_ref, *, final):
    acc = x_ref[...] + _dot(ya_ref[...], wa_ref[...])
    acc = acc + _dot(yb_ref[...], wb_ref[...])
    acc = acc + _dot(yc_ref[...], wc_ref[...])
    ms = jnp.mean(acc * acc, axis=-1, keepdims=True)
    hn = (acc * lax.rsqrt(ms + NORM_EPS) * ng_ref[...]).astype(BF16)
    gate = _sigmoid(_dot(hn, wg_ref[...]))
    out = acc + gate * _dot(p_ref[...].astype(BF16), wp_ref[...])
    if final:
        ms = jnp.mean(out * out, axis=-1, keepdims=True)
        out = out * lax.rsqrt(ms + NORM_EPS) * fg_ref[...]
    o_ref[...] = out


def _out_proj(x2d, ya, yb, yc, p3d, layer, wa, wb, wc, ng, wg, wp, fg, final, tm=1024):
    n, d = x2d.shape
    row = lambda w: pl.BlockSpec((tm, w), lambda i: (i, 0))
    full = lambda a: pl.BlockSpec(a.shape, lambda i: (0, 0))
    return pl.pallas_call(
        functools.partial(_out_proj_kernel, final=final),
        grid=(n // tm,),
        in_specs=[row(d), row(ya.shape[1]), row(yb.shape[1]), row(yc.shape[1]),
                  pl.BlockSpec((None, tm, p3d.shape[2]), lambda i: (layer, i, 0)),
                  full(wa), full(wb), full(wc), full(ng), full(wg), full(wp), full(fg)],
        out_specs=row(d),
        out_shape=jax.ShapeDtypeStruct((n, d), F32),
        compiler_params=pltpu.CompilerParams(
            dimension_semantics=("parallel",), vmem_limit_bytes=VMEM_LIMIT),
        name="out_proj",
    )(x2d, ya, yb, yc, p3d, wa, wb, wc, ng, wg, wp, fg)


def _unit_lower_inverse(n_list, row, col):
    c = n_list[0].shape[0]
    eye = jnp.where(row == col, 1.0, 0.0).astype(F32)
    same = (row // INV_BASE) == (col // INV_BASE)
    pw = [jnp.where(same, n, 0.0) for n in n_list]
    t = [eye - x for x in pw]
    span = 2
    while span < INV_BASE:
        pwb = [x.astype(BF16) for x in pw]
        pw = [_dot(x, x) for x in pwb]
        t = [x + _dot(x.astype(BF16), y.astype(BF16)) for x, y in zip(t, pw)]
        span *= 2
    size = INV_BASE
    while size < c:
        lower_left = ((row // (2 * size)) == (col // (2 * size))) & ((row // size) != (col // size))
        tb = [x.astype(BF16) for x in t]
        et = [_dot(jnp.where(lower_left, n, 0.0).astype(BF16), x).astype(BF16) for n, x in zip(n_list, tb)]
        t = [x - _dot(xb, y) for x, xb, y in zip(t, tb, et)]
        size *= 2
    return t


def _rwkv_kernel(zr_ref, zk_ref, zv_ref, zg_ref, zwa_ref, mu_ref, muwa_ref, w0_ref, wup_ref, a0_ref,
                 aup_ref, kk_ref, ka_ref, rk_ref, lng_ref, lnb_ref, o_ref,
                 state_ref, prev_ref, prevwa_ref, y_ref):
    nb, c = zr_ref.shape[0], zr_ref.shape[1]
    width = RWKV_WIDTH

    @pl.when(pl.program_id(1) == 0)
    def _():
        state_ref[...] = jnp.zeros_like(state_ref)
        prev_ref[...] = jnp.zeros_like(prev_ref)
        prevwa_ref[...] = jnp.zeros_like(prevwa_ref)

    first_row = lax.broadcasted_iota(I32, (c, width), 0) == 0
    first_row_wa = lax.broadcasted_iota(I32, (c, LANES), 0) == 0
    hrow = lax.broadcasted_iota(I32, (width, width), 0) // HEAD_DIM
    hcol = lax.broadcasted_iota(I32, (width, width), 1) // HEAD_DIM
    head_ones = jnp.where(hrow == hcol, 1.0, 0.0).astype(BF16)
    row = lax.broadcasted_iota(I32, (c, c), 0)
    col = lax.broadcasted_iota(I32, (c, c), 1)
    tri = jnp.where(col <= row, 1.0, 0.0).astype(BF16)
    strict = col < row
    incl = col <= row
    hs = [slice(h * HEAD_DIM, (h + 1) * HEAD_DIM) for h in range(RWKV_HEADS)]

    def prepare(bi):
        def shifted(z_ref, idx):
            z = z_ref[bi]
            prev = jnp.where(first_row, prev_ref[bi, idx, 0:1, :], pltpu.roll(z, 1, 0))
            prev_ref[bi, idx, 0:1, :] = z[c - 1:c, :]
            return z + (prev - z) * mu_ref[idx:idx + 1, :]

        r = shifted(zr_ref, 0)
        k = shifted(zk_ref, 1)
        v = shifted(zv_ref, 2)
        g = shifted(zg_ref, 3)
        zwa = zwa_ref[bi]
        prev_wa = jnp.where(first_row_wa, prevwa_ref[bi, 0:1, :], pltpu.roll(zwa, 1, 0))
        prevwa_ref[bi, 0:1, :] = zwa[c - 1:c, :]
        wa = zwa + (prev_wa - zwa) * muwa_ref[...]
        wd = wa[:, :LORA_RANK]
        ad = wa[:, LORA_RANK:]

        lw = -DECAY_SCALE * _sigmoid(w0_ref[...] + _dot_x3(jnp.tanh(wd), wup_ref[...]))
        a = _sigmoid(a0_ref[...] + _dot_x3(ad, aup_ref[...]))
        kk = k * kk_ref[...]
        norm = jnp.sqrt(_head_sum(kk * kk, head_ones))
        kk = kk / jnp.maximum(norm, 1e-12)
        kmod = k * (1.0 + (a - 1.0) * ka_ref[...])
        b = a * kk

        lw_hi, lw_lo = _split(lw)
        cl = _dot(tri, lw_hi) + _dot(tri, lw_lo)
        c_mid = cl[c // 2 - 1:c // 2, :]
        c_end = cl[c - 1:c, :]
        w = jnp.exp(lw)
        e_in1 = jnp.exp(cl - lw)
        e_in2 = e_in1 * w
        e_k = jnp.exp(c_mid - cl)
        e_mid_inv = jnp.exp(-c_mid)
        e_end = e_k * jnp.exp(c_end - c_mid)
        ops = dict(
            q1=kk * e_in1 * e_mid_inv, q2=r * e_in2 * e_mid_inv, k1=b * e_k, k2=kmod * e_k,
            kkd=kk * e_in1, rd=r * e_in2, bd=b * e_end, kd=kmod * e_end, vb=v)
        ops = {name: [x.astype(BF16)[:, s] for s in hs] for name, x in ops.items()}
        return dict(ops, r=r, kmod=kmod, v=v, g=g, d_end=jnp.exp(c_end))

    prep = [prepare(bi) for bi in range(nb)]

    probs = [(bi, h) for bi in range(nb) for h in range(RWKV_HEADS)]
    op = lambda name: [prep[bi][name][h] for bi, h in probs]
    q1, q2, k1, k2, kkd, rd, bd, kd, vh = map(op, ("q1", "q2", "k1", "k2", "kkd", "rd", "bd", "kd", "vb"))
    n = range(len(probs))
    qk = [_dot_nt(jnp.concatenate([q1[i], q2[i]], axis=0), jnp.concatenate([k1[i], k2[i]], axis=0)) for i in n]
    a_ab = [jnp.where(strict, x[:c, :c], 0.0) for x in qk]
    t_inv = _unit_lower_inverse(a_ab, row, col)
    a_rb = [jnp.where(incl, x[c:, :c], 0.0).astype(BF16) for x in qk]
    a_v = [jnp.concatenate([jnp.where(strict, x[:c, c:], 0.0), jnp.where(incl, x[c:, c:], 0.0)],
                           axis=0).astype(BF16) for x in qk]
    s_old = [state_ref[bi, h] for bi, h in probs]
    s_in = [s.astype(BF16) for s in s_old]
    from_v = [_dot(a_v[i], vh[i]) for i in n]
    from_s = [_dot_nt(jnp.concatenate([kkd[i], rd[i]], axis=0), s_in[i]) for i in n]
    rhs = [from_s[i][:c] + from_v[i][:c] for i in n]
    u = [-_dot(t_inv[i].astype(BF16), rhs[i].astype(BF16)) for i in n]
    uv_t = [jnp.concatenate([u[i], prep[bi]["v"][:, hs[h]]], axis=1).T.astype(BF16)
            for i, (bi, h) in enumerate(probs)]
    for i, (bi, h) in enumerate(probs):
        state_ref[bi, h] = (s_old[i] * prep[bi]["d_end"][:, hs[h]]
                            + _dot(uv_t[i][:HEAD_DIM, :], bd[i]) + _dot(uv_t[i][HEAD_DIM:, :], kd[i]))
    for i, (bi, h) in enumerate(probs):
        y_ref[bi, :, hs[h]] = from_s[i][c:] + _dot(a_rb[i], u[i].astype(BF16)) + from_v[i][c:]

    inv_n = 1.0 / HEAD_DIM
    for bi in range(nb):
        pb = prep[bi]
        y = y_ref[bi]
        mean = _head_sum(y, head_ones) * inv_n
        dlt = y - mean
        var = _head_sum(dlt * dlt, head_ones) * inv_n
        yn = dlt * lax.rsqrt(var + GN_EPS) * lng_ref[...] + lnb_ref[...]
        bonus = _head_sum(pb["r"] * pb["kmod"] * rk_ref[...], head_ones) * pb["v"]
        g = pb["g"]
        o_ref[bi] = ((yn + bonus) * (g * _sigmoid(g))).astype(o_ref.dtype)


def _rwkv(z3d, mu4, mu_wa, w0, w_up, a0, a_up, k_k, k_a, r_k, ln_g, ln_b):
    batch, seq, _ = z3d.shape
    c = RWKV_CHUNK
    nb = RWKV_BATCH_ROWS if batch % RWKV_BATCH_ROWS == 0 else 1
    zspec = lambda u: pl.BlockSpec((nb, c, RWKV_WIDTH), lambda b, i, u=u: (b, i, u))
    full = lambda a: pl.BlockSpec(a.shape, lambda b, i: (0,) * a.ndim)
    params = (mu4, mu_wa, w0, w_up, a0, a_up, k_k, k_a, r_k, ln_g, ln_b)
    return pl.pallas_call(
        _rwkv_kernel,
        grid=(batch // nb, seq // c),
        in_specs=[zspec(0), zspec(1), zspec(2), zspec(3),
                  pl.BlockSpec((nb, c, LANES), lambda b, i: (b, i, U_RW_WA))]
                 + [full(a) for a in params],
        out_specs=pl.BlockSpec((nb, c, RWKV_WIDTH), lambda b, i: (b, i, 0)),
        out_shape=jax.ShapeDtypeStruct((batch, seq, RWKV_WIDTH), BF16),
        scratch_shapes=[
            pltpu.VMEM((nb, RWKV_HEADS, HEAD_DIM, HEAD_DIM), F32),
            pltpu.VMEM((nb, 4, 8, RWKV_WIDTH), F32),
            pltpu.VMEM((nb, 8, LANES), F32),
            pltpu.VMEM((nb, c, RWKV_WIDTH), F32),
        ],
        compiler_params=pltpu.CompilerParams(
            dimension_semantics=("parallel", "arbitrary"), vmem_limit_bytes=VMEM_LIMIT),
        name="rwkv7",
    )(z3d, z3d, z3d, z3d, z3d, *params)


def _bit_transpose32(words):
    a = list(words)
    j, mask = 16, 0x0000FFFF
    while j:
        for k in range(32):
            if k & j == 0:
                t = (a[k] ^ lax.shift_right_logical(a[k + j], jnp.int32(j))) & jnp.int32(mask)
                a[k] = a[k] ^ t
                a[k + j] = a[k + j] ^ (t << j)
        j >>= 1
        mask ^= mask << j
    return a


def _dsa_kernel(*refs):
    seq = refs[3].shape[0]
    lax.fori_loop(0, seq // DSA_QUERY_BLOCK, lambda qb, carry: (_dsa_query_block(qb, *refs), carry)[1], 0)


def _dsa_query_block(qb, zq_ref, zg_ref, ziq_ref, zkv_ref, zms_ref, c64_ref, s64_ref, c32_ref, s32_ref, o_ref,
                     k_ref, vt_ref, ik_ref, key_ref, plane_ref):
    seq = zkv_ref.shape[0]
    blk = DSA_QUERY_BLOCK
    sub_rows = ATT_BLOCK
    kb_rows = DSA_KEY_BLOCK
    n_kb = (qb * blk) // kb_rows + 1

    @pl.when(qb == 0)
    def _():
        lane = lax.broadcasted_iota(I32, (kb_rows, LANES), 1)

        def body(t, carry):
            rows = pl.ds(pl.multiple_of(t * kb_rows, kb_rows), kb_rows)
            kv = zkv_ref[rows, :]
            kvr = jnp.where(lane < HEAD_DIM, _rope(kv, c64_ref[rows, :], s64_ref[rows, :], 8, HEAD_DIM), kv)
            k_ref[rows, :] = kvr[:, :HEAD_DIM].astype(BF16)
            vt_ref[t] = kvr.T[HEAD_DIM:, :].astype(BF16)
            ms = zms_ref[rows, :]
            ikr = _rope(ms, c32_ref[rows, :], s32_ref[rows, :], 4, IDX_DIM)
            ik_ref[rows, :] = ikr[:, :IDX_DIM].astype(BF16)
            return carry

        lax.fori_loop(0, seq // kb_rows, body, 0)

    r0 = pl.multiple_of(qb * blk, blk)
    qrows = pl.ds(r0, blk)
    c64 = c64_ref[qrows, :]
    s64 = s64_ref[qrows, :]
    c32 = c32_ref[qrows, :]
    s32 = s32_ref[qrows, :]
    q2 = [_rope(zq_ref[qrows, i * LANES:(i + 1) * LANES], c64, s64, 8, HEAD_DIM) * QK_SCALE_LOG2
          for i in range(2)]
    q_stack = jnp.concatenate(
        [q2[h // 2][:, (h % 2) * HEAD_DIM:(h % 2 + 1) * HEAD_DIM] for h in range(DSA_HEADS)],
        axis=0).astype(BF16)
    iq2 = [_rope(ziq_ref[qrows, i * LANES:(i + 1) * LANES], c32, s32, 4, IDX_DIM) for i in range(2)]
    iq_stack = jnp.concatenate(
        [iq2[h // 4][:, (h % 4) * IDX_DIM:(h % 4 + 1) * IDX_DIM] for h in range(IDX_HEADS)],
        axis=0).astype(BF16)
    iw_t = zms_ref[qrows, :].T[IDX_DIM:IDX_DIM + IDX_HEADS, :] * ((IDX_HEADS * IDX_DIM) ** -0.5)
    iw_row = jnp.concatenate([iw_t[h:h + 1, :] for h in range(IDX_HEADS)], axis=1)

    kpos_l = lax.broadcasted_iota(I32, (kb_rows, blk), 0)
    qpos = r0 + lax.broadcasted_iota(I32, (kb_rows, blk), 1)

    tri = jnp.where(lax.broadcasted_iota(I32, (sub_rows, sub_rows), 1)
                    <= lax.broadcasted_iota(I32, (sub_rows, sub_rows), 0), 1.0, 0.0).astype(BF16)
    g = zg_ref[qrows, :]
    gate = g * _sigmoid(g)

    def step(n):
        rows = [slice(kb * kb_rows, (kb + 1) * kb_rows) for kb in range(n)]
        groups = kb_rows // BIT_GROUP_ROWS
        causal = (n - 1) * kb_rows + kpos_l <= qpos

        for kb in range(n):
            s = jnp.maximum(_dot_nt(ik_ref[rows[kb], :], iq_stack), 0.0) * iw_row
            sc = s[:, 0:blk]
            for h in range(1, IDX_HEADS):
                sc = sc + s[:, h * blk:(h + 1) * blk]
            if kb == n - 1:
                sc = jnp.where(causal, sc, -jnp.inf)
            bits = pltpu.bitcast(sc, I32)
            key = bits ^ ((bits >> 31) & 0x7FFFFFFF)
            key_ref[rows[kb], :] = key
            ku = key ^ INT_MIN
            for grp in range(groups):
                words = [ku[(grp * 32 + v) * 8:(grp * 32 + v + 1) * 8, :] for v in range(32)]
                for b, plane in enumerate(_bit_transpose32(words)):
                    plane_ref[kb * groups + grp, b] = plane

        n_groups = n * groups

        def bit_body(i, carry):
            active, k_rem, thr_u = carry
            ones = [active[g] & plane_ref[g, i] for g in range(n_groups)]
            acc = lax.population_count(ones[0])
            for g in range(1, n_groups):
                acc = acc + lax.population_count(ones[g])
            cnt = acc.sum(axis=0, keepdims=True)
            dec = cnt >= k_rem
            k_rem = jnp.where(dec, k_rem, k_rem - cnt)
            thr_u = thr_u | jnp.where(dec, jnp.int32(1) << (31 - i), 0)
            dec8 = jnp.broadcast_to(dec, (8, blk))
            active = tuple(jnp.where(dec8, ones[g], active[g] ^ ones[g]) for g in range(n_groups))
            return active, k_rem, thr_u

        def search():
            init = (tuple(jnp.full((8, blk), -1, I32) for _ in range(n_groups)),
                    jnp.full((1, blk), DSA_TOPK, I32), jnp.zeros((1, blk), I32))
            _, k_rem, thr_u = lax.fori_loop(0, 32, bit_body, init)
            return thr_u ^ INT_MIN, k_rem

        keep_all = lambda: (jnp.full((1, blk), INT_MIN, I32), jnp.zeros((1, blk), I32))
        if (n - 1) * kb_rows >= DSA_TOPK:
            thr, need = search()
        else:
            thr, need = lax.cond((qb + 1) * blk > DSA_TOPK, search, keep_all)
        need = need.astype(F32)

        wide = DSA_HEADS * blk
        m = jnp.full((1, wide), NEG_BIG, F32)
        l = jnp.zeros((1, wide), F32)
        acc = jnp.zeros((HEAD_DIM, wide), F32)
        run = jnp.zeros((1, blk), F32)
        for kb in range(n):
            keys = key_ref[rows[kb], :]
            eq = keys == thr
            eq_b = jnp.where(eq, 1.0, 0.0).astype(BF16)
            local = [_dot(tri, eq_b[sub * sub_rows:(sub + 1) * sub_rows, :]) for sub in range(kb_rows // sub_rows)]
            ranks = []
            for x in local:
                ranks.append(x + run)
                run = run + x[sub_rows - 1:sub_rows, :]
            rank = jnp.concatenate(ranks, axis=0)
            take = (keys > thr) | (eq & (rank <= need))
            if kb == n - 1:
                take = take & causal
            bias = jnp.where(take, 0.0, NEG_BIG)
            s = _dot_nt(k_ref[rows[kb], :], q_stack)
            s = s + jnp.concatenate([bias] * DSA_HEADS, axis=1)
            m_new = jnp.maximum(m, jnp.max(s, axis=0, keepdims=True))
            p = jnp.exp2(s - m_new)
            alpha = jnp.exp2(m - m_new)
            l = alpha * l + jnp.sum(p, axis=0, keepdims=True)
            acc = alpha * acc + _dot(vt_ref[kb], p.astype(BF16))
            m = m_new

        o_t = acc / l
        o = jnp.concatenate(
            [jnp.concatenate([o_t[:, (2 * i) * blk:(2 * i + 1) * blk], o_t[:, (2 * i + 1) * blk:(2 * i + 2) * blk]],
                             axis=0).T for i in range(DSA_HEADS // 2)], axis=1)
        o_ref[qrows, :] = (o * gate).astype(o_ref.dtype)

    for n in range(1, seq // kb_rows + 1):
        pl.when(n_kb == n)(functools.partial(step, n))


def _dsa(z2d, batch, seq, c64, s64, c32, s32):
    blk = DSA_QUERY_BLOCK
    qspec = lambda u2: pl.BlockSpec((seq, 2 * LANES), lambda b, u2=u2: (b, u2))
    seqspec = lambda u: pl.BlockSpec((seq, LANES), lambda b, u=u: (b, u))
    tab = pl.BlockSpec((seq, LANES), lambda b: (0, 0))
    return pl.pallas_call(
        _dsa_kernel,
        grid=(batch,),
        in_specs=[qspec(U_DS_Q // 2), qspec(U_DS_G // 2), qspec(U_DS_IQ // 2),
                  seqspec(U_DS_KV), seqspec(U_DS_MISC), tab, tab, tab, tab],
        out_specs=pl.BlockSpec((seq, DSA_WIDTH), lambda b: (b, 0)),
        out_shape=jax.ShapeDtypeStruct((batch * seq, DSA_WIDTH), BF16),
        scratch_shapes=[
            pltpu.VMEM((seq, HEAD_DIM), BF16),
            pltpu.VMEM((seq // DSA_KEY_BLOCK, HEAD_DIM, DSA_KEY_BLOCK), BF16),
            pltpu.VMEM((seq, IDX_DIM), BF16),
            pltpu.VMEM((seq, blk), I32),
            pltpu.VMEM((seq // BIT_GROUP_ROWS, 32, 8, blk), I32),
        ],
        compiler_params=pltpu.CompilerParams(
            dimension_semantics=("parallel",), vmem_limit_bytes=VMEM_LIMIT),
        name="dsa",
    )(z2d, z2d, z2d, z2d, z2d, c64, s64, c32, s32)


def _dil_group(gi, rate, q_ref, k_ref, v_ref, g_ref, c64_ref, s64_ref, qr_ref, kr_ref, o_s, lse_s):
    seq = q_ref.shape[0]
    blk = ATT_BLOCK
    rope_rows = 4 * blk
    nrow = seq // rope_rows

    def rope_body(t, carry):
        rows = pl.ds(pl.multiple_of(t * rope_rows, rope_rows), rope_rows)
        ct = c64_ref[rows, :]
        st = s64_ref[rows, :]
        q = q_ref[rows, :]
        k = k_ref[rows, :]
        qr_ref[rows, :] = (q * ct + pltpu.roll(q, HEAD_DIM, 1) * st) * QK_SCALE_LOG2
        kr_ref[rows, :] = k * ct + pltpu.roll(k, HEAD_DIM, 1) * st
        return carry

    lax.fori_loop(0, nrow, rope_body, 0)

    nblk = seq // (blk * rate)
    qi = lax.broadcasted_iota(I32, (blk, 2 * blk), 0)
    kj = lax.broadcasted_iota(I32, (blk, 2 * blk), 1)
    band = (kj >= qi) & (kj <= qi + DIL_STEPS)

    def strided(ref, start):
        if rate == 1:
            return ref[pl.ds(start, blk), :]
        return ref[pl.ds(start, blk, stride=rate), :]

    def store(ref, start, val):
        if rate == 1:
            ref[gi, pl.ds(start, blk), :] = val
        else:
            ref[gi, pl.ds(start, blk, stride=rate), :] = val

    first_q = lax.broadcasted_iota(I32, (blk, LANES), 1) < HEAD_DIM
    first_kv = lax.broadcasted_iota(I32, (2 * blk, LANES), 1) < HEAD_DIM
    lane64 = lax.broadcasted_iota(I32, (blk, LANES), 1) % HEAD_DIM
    first_qk = (lane64 < 8) | ((lane64 >= 16) & (lane64 < 40))
    unroll = DIL_UNROLL

    def body(it, carry):
        starts, qs, ks, vs, masks = [], [], [], [], []
        for u in range(unroll):
            cidx = it * unroll + u
            rho = cidx // nblk
            j = cidx % nblk
            start = j * (blk * rate) + rho
            pstart = jnp.maximum(j - 1, 0) * (blk * rate) + rho
            starts.append(start)
            qs.append(strided(qr_ref, start))
            ks.append(jnp.concatenate([strided(kr_ref, pstart), strided(kr_ref, start)], axis=0).astype(BF16))
            vs.append(jnp.concatenate([strided(v_ref, pstart), strided(v_ref, start)], axis=0))
            masks.append(band & ((kj >= blk) | (j > 0)))
        prob = [(u, h) for u in range(unroll) for h in range(2)]
        qh = [jnp.where(first_qk == (h == 0), qs[u], 0.0).astype(BF16) for u, h in prob]
        vh = [jnp.where(first_kv == (h == 0), vs[u], 1.0).astype(BF16) for u, h in prob]
        s = [jnp.where(masks[u], _dot_nt(qh[i], ks[u]), -jnp.inf) for i, (u, h) in enumerate(prob)]
        m = [jnp.max(x, axis=-1, keepdims=True) for x in s]
        r = [_dot(jnp.exp2(s[i] - m[i]).astype(BF16), vh[i]) for i in range(len(prob))]
        for u in range(unroll):
            r0, r1 = r[2 * u], r[2 * u + 1]
            o = jnp.where(first_q, r0, r1)
            l = pltpu.roll(jnp.where(first_q, r1, r0), HEAD_DIM, 1)
            lse = jnp.where(first_q, m[2 * u], m[2 * u + 1]) * math.log(2.0) + jnp.log(l)
            gate = strided(g_ref, starts[u])
            store(o_s, starts[u], o / l * (gate * _sigmoid(gate)))
            store(lse_s, starts[u], lse)
        return carry

    lax.fori_loop(0, rate * nblk // unroll, body, 0)


def _dil_kernel(q_ref, k_ref, v_ref, g_ref, c64_ref, s64_ref, o_ref, qr_ref, kr_ref, o_s, lse_s):
    grp = pl.program_id(1)
    for gi, rate in enumerate(DIL_RATES):
        pl.when(grp == gi)(functools.partial(
            _dil_group, gi, rate, q_ref, k_ref, v_ref, g_ref, c64_ref, s64_ref, qr_ref, kr_ref, o_s, lse_s))

    @pl.when(grp == len(DIL_RATES) - 1)
    def _():
        blk = ATT_BLOCK

        def body(t, carry):
            rows = pl.ds(pl.multiple_of(t * blk, blk), blk)
            ls = [lse_s[i, rows, :] for i in range(3)]
            top = jnp.maximum(jnp.maximum(ls[0], ls[1]), ls[2])
            es = [jnp.exp(x - top) for x in ls]
            inv = 1.0 / (es[0] + es[1] + es[2])
            o_ref[rows, :] = jnp.concatenate(
                [o_s[i, rows, :] * (es[i] * inv) for i in range(3)], axis=1).astype(o_ref.dtype)
            return carry

        lax.fori_loop(0, q_ref.shape[0] // blk, body, 0)


def _dilated(z2d, batch, seq, c64, s64):
    ng = len(DIL_RATES)
    zspec = lambda base: pl.BlockSpec((seq, LANES), lambda b, g, base=base: (b, base + g))
    tab = pl.BlockSpec((seq, LANES), lambda b, g: (0, 0))
    return pl.pallas_call(
        _dil_kernel,
        grid=(batch, ng),
        in_specs=[zspec(U_DL), zspec(U_DL + 3), zspec(U_DL + 6), zspec(U_DL + 9), tab, tab],
        out_specs=pl.BlockSpec((seq, DIL_WIDTH), lambda b, g: (b, 0)),
        out_shape=jax.ShapeDtypeStruct((batch * seq, DIL_WIDTH), BF16),
        scratch_shapes=[
            pltpu.VMEM((seq, LANES), F32),
            pltpu.VMEM((seq, LANES), F32),
            pltpu.VMEM((ng, seq, LANES), F32),
            pltpu.VMEM((ng, seq, LANES), F32),
        ],
        compiler_params=pltpu.CompilerParams(
            dimension_semantics=("parallel", "arbitrary"), vmem_limit_bytes=VMEM_LIMIT),
        name="dilated",
    )(z2d, z2d, z2d, z2d, c64, s64)


def _rope_tables(seq, head, half):
    inv = ROPE_THETA ** (-jnp.arange(half, dtype=F32) / half)
    ang = jnp.arange(seq, dtype=F32)[:, None] * inv[None, :]
    cos, sin = jnp.cos(ang), jnp.sin(ang)
    lane = jnp.arange(LANES) % head
    idx = lane % half
    cos_t = jnp.where(lane[None, :] < 2 * half, cos[:, idx], 1.0)
    sin_t = jnp.where(lane[None, :] < half, -sin[:, idx], jnp.where(lane[None, :] < 2 * half, sin[:, idx], 0.0))
    return cos_t, sin_t


def _dil_pair_lanes():
    half, rest = HEAD_DIM // 8, (HEAD_DIM - HEAD_DIM // 4) // 2
    src = []
    for part in range(2):
        for head in range(2):
            src += [head * HEAD_DIM + part * half + i for i in range(half)]
        for head in range(2):
            src += [head * HEAD_DIM + 2 * half + part * rest + i for i in range(rest)]
    return src


def _rope_tables_paired(seq):
    half = HEAD_DIM // 8
    inv = ROPE_THETA ** (-jnp.arange(half, dtype=F32) / half)
    ang = jnp.arange(seq, dtype=F32)[:, None] * inv[None, :]
    cos, sin = jnp.cos(ang), jnp.sin(ang)
    lane = jnp.arange(LANES)[None, :]
    idx = jnp.arange(LANES) % half
    rotated = lane % HEAD_DIM < 2 * half
    cos_t = jnp.where(rotated, cos[:, idx], 1.0)
    sin_t = jnp.where(rotated, jnp.where(lane < HEAD_DIM, -sin[:, idx], sin[:, idx]), 0.0)
    return cos_t, sin_t


def _in_col_sources():
    rw = RWKV_WIDTH
    cols = lambda o, wdt: list(range(o, o + wdt))
    o = 0
    r, k, v, g = (cols(o + i * rw, rw) for i in range(4))
    o = 4 * rw
    wd, ad = cols(o, LORA_RANK), cols(o + LORA_RANK, LORA_RANK)
    o += 2 * LORA_RANK
    parts = []
    for wdt in (DSA_WIDTH, HEAD_DIM, HEAD_DIM, IDX_HEADS * IDX_DIM, IDX_DIM, IDX_HEADS, DSA_WIDTH):
        parts.append(cols(o, wdt))
        o += wdt
    dq, dk, dv, diq, dik, diw, dg = parts
    cq, ck, cv, cg = (cols(o + i * DIL_WIDTH, DIL_WIDTH) for i in range(4))
    o += 4 * DIL_WIDTH
    pair = [t * LANES + s for t in range(DIL_WIDTH // LANES) for s in _dil_pair_lanes()]
    cq, ck = [cq[i] for i in pair], [ck[i] for i in pair]
    pad = [o] * (LANES - IDX_DIM - IDX_HEADS)
    src = r + k + v + g + cq + ck + cv + cg + dq + dg + diq + dk + dv + dik + diw + pad + wd + ad
    assert len(src) == Z_COLS
    return src


def _permute_in_cols(w):
    return jnp.take(w, jnp.asarray(_in_col_sources(), I32), axis=-1, mode="fill", fill_value=0)


def kernel(x, p, norm_g, w_in, tshift_mu, rwkv_w0, rwkv_w_up, rwkv_a0, rwkv_a_up, rwkv_k_k, rwkv_k_a, rwkv_r_k,
           rwkv_ln_g, rwkv_ln_b, w_out, ple_norm_g, ple_w_gate, ple_w_proj, final_norm_g):
    batch, seq, d_model = x.shape
    depth = w_in.shape[0]
    n = batch * seq
    assert seq % (ATT_BLOCK * max(DIL_RATES)) == 0 and seq >= 4 * DSA_TOPK

    c64, s64 = _rope_tables(seq, HEAD_DIM, HEAD_DIM // 8)
    c32, s32 = _rope_tables(seq, IDX_DIM, IDX_DIM // 8)
    c64p, s64p = _rope_tables_paired(seq)
    w_in_k = _permute_in_cols(w_in.astype(BF16))
    rw = RWKV_WIDTH
    row = lambda a: a.reshape(1, -1)

    x2d = x.reshape(n, d_model)
    p3d = p.reshape(depth, n, p.shape[-1])
    for i in range(depth):
        z = _in_proj(x2d, row(norm_g[i]), w_in_k, i)
        mu = tshift_mu[i]
        y_a = _rwkv(z.reshape(batch, seq, Z_COLS), mu[:4 * rw].reshape(4, rw), row(mu[4 * rw:]), row(rwkv_w0[i]), rwkv_w_up[i],
                    row(rwkv_a0[i]), rwkv_a_up[i], row(rwkv_k_k[i]), row(rwkv_k_a[i]), row(rwkv_r_k[i]),
                    row(rwkv_ln_g[i]), row(rwkv_ln_b[i])).reshape(n, rw)
        y_b = _dsa(z, batch, seq, c64, s64, c32, s32)
        y_c = _dilated(z, batch, seq, c64p, s64p)
        wo = w_out[i].astype(BF16)
        x2d = _out_proj(x2d, y_a, y_b, y_c, p3d, i, wo[:rw], wo[rw:rw + DSA_WIDTH],
                        wo[rw + DSA_WIDTH:], row(ple_norm_g[i]), ple_w_gate[i].astype(BF16),
                        ple_w_proj[i].astype(BF16), row(final_norm_g), final=(i == depth - 1))
    return x2d.reshape(batch, seq, d_model)
```

```python
import functools
import math

import jax
import jax.numpy as jnp
from jax import lax
from jax.experimental import pallas as pl
from jax.experimental.pallas import tpu as pltpu

F32 = jnp.float32
BF16 = jnp.bfloat16
I32 = jnp.int32

HEAD_DIM = 64
ROPE_THETA = 500000.0
ROT_HALF = HEAD_DIM // 8
NORM_EPS = 1e-6
RWKV_HEADS = 6
RWKV_WIDTH = RWKV_HEADS * HEAD_DIM
LORA_RANK = 64
GN_EPS = 64e-5
DECAY_SCALE = math.exp(-0.5)
DSA_HEADS = 4
DSA_WIDTH = DSA_HEADS * HEAD_DIM
IDX_HEADS = 8
IDX_DIM = 32
IDX_ROT_HALF = IDX_DIM // 8
DSA_TOPK = 256
DIL_RATES = (1, 4, 16)
DIL_STEPS = 128
DIL_UNROLL = 4
DIL_WIDTH = 384
ATT_BLOCK = 128
DSA_QUERY_BLOCK = 256
DSA_KEY_BLOCK = 512

LANES = 128
MXU_WIDTH = 256
VMEM_LIMIT = 48 * 1024 * 1024

U_RW = 0
U_DL = 12
U_DS_Q = 24
U_DS_G = 26
U_DS_IQ = 28
U_DS_KV = 30
U_DS_MISC = 31
U_RW_WA = 32
Z_UNITS = 33
Z_COLS = Z_UNITS * LANES

RWKV_CHUNK = 128
RWKV_BATCH_ROWS = 4
INV_BASE = 16
QK_SCALE_LOG2 = HEAD_DIM ** -0.5 * math.log2(math.e)
NEG_BIG = -1e30
BIT_GROUP_ROWS = 256
INT_MIN = -2 ** 31


def _dot(a, b):
    return jnp.dot(a, b, preferred_element_type=F32)


def _dot_nt(a, b):
    return lax.dot_general(a, b, (((1,), (1,)), ((), ())), preferred_element_type=F32)


def _split(x):
    hi = x.astype(BF16)
    lo = (x - hi.astype(F32)).astype(BF16)
    return hi, lo


def _head_sum(x, head_ones):
    return _dot(x.astype(BF16), head_ones)


def _dot_x3(x, w):
    xh, xl = _split(x)
    wh, wl = _split(w)
    return _dot(xh, wh) + (_dot(xh, wl) + _dot(xl, wh))


def _sigmoid(x):
    return 1.0 / (1.0 + jnp.exp(-x))


def _rope(x, cos_t, sin_t, half, head):
    lane = lax.broadcasted_iota(I32, x.shape, 1) % head
    partner = jnp.where(lane < half, pltpu.roll(x, LANES - half, 1), pltpu.roll(x, half, 1))
    return x * cos_t + partner * sin_t


def _in_proj_kernel(x_ref, g_ref, w_ref, z_ref, *, tn):
    x = x_ref[...]
    ms = jnp.mean(x * x, axis=-1, keepdims=True)
    h = (x * lax.rsqrt(ms + NORM_EPS) * g_ref[...]).astype(BF16)
    for c in range(0, Z_COLS, tn):
        wdt = min(tn, Z_COLS - c)
        z_ref[:, c:c + wdt] = _dot(h, w_ref[:, c:c + wdt])


def _in_proj(x2d, g, w_all, layer, tm=512, tn=MXU_WIDTH):
    n, d = x2d.shape
    return pl.pallas_call(
        functools.partial(_in_proj_kernel, tn=tn),
        grid=(n // tm,),
        in_specs=[
            pl.BlockSpec((tm, d), lambda i: (i, 0)),
            pl.BlockSpec((1, d), lambda i: (0, 0)),
            pl.BlockSpec((None, d, Z_COLS), lambda i: (layer, 0, 0)),
        ],
        out_specs=pl.BlockSpec((tm, Z_COLS), lambda i: (i, 0)),
        out_shape=jax.ShapeDtypeStruct((n, Z_COLS), F32),
        compiler_params=pltpu.CompilerParams(
            dimension_semantics=("parallel",), vmem_limit_bytes=VMEM_LIMIT),
        name="in_proj",
    )(x2d, g, w_all)


def _out_proj_kernel(x_ref, ya_ref, yb_ref, yc_ref, p_ref, wa_ref, wb_ref, wc_ref, ng_ref, wg_ref,
                     wp_ref, fg_ref, o_ref, *, final):
    acc = x_ref[...] + _dot(ya_ref[...], wa_ref[...])
    acc = acc + _dot(yb_ref[...], wb_ref[...])
    acc = acc + _dot(yc_ref[...], wc_ref[...])
    ms = jnp.mean(acc * acc, axis=-1, keepdims=True)
    hn = (acc * lax.rsqrt(ms + NORM_EPS) * ng_ref[...]).astype(BF16)
    gate = _sigmoid(_dot(hn, wg_ref[...]))
    out = acc + gate * _dot(p_ref[...].astype(BF16), wp_ref[...])
    if final:
        ms = jnp.mean(out * out, axis=-1, keepdims=True)
        out = out * lax.rsqrt(ms + NORM_EPS) * fg_ref[...]
    o_ref[...] = out


def _out_proj(x2d, ya, yb, yc, p3d, layer, wa, wb, wc, ng, wg, wp, fg, final, tm=1024):
    n, d = x2d.shape
    row = lambda w: pl.BlockSpec((tm, w), lambda i: (i, 0))
    full = lambda a: pl.BlockSpec(a.shape, lambda i: (0, 0))
    return pl.pallas_call(
        functools.partial(_out_proj_kernel, final=final),
        grid=(n // tm,),
        in_specs=[row(d), row(ya.shape[1]), row(yb.shape[1]), row(yc.shape[1]),
                  pl.BlockSpec((None, tm, p3d.shape[2]), lambda i: (layer, i, 0)),
                  full(wa), full(wb), full(wc), full(ng), full(wg), full(wp), full(fg)],
        out_specs=row(d),
        out_shape=jax.ShapeDtypeStruct((n, d), F32),
        compiler_params=pltpu.CompilerParams(
            dimension_semantics=("parallel",), vmem_limit_bytes=VMEM_LIMIT),
        name="out_proj",
    )(x2d, ya, yb, yc, p3d, wa, wb, wc, ng, wg, wp, fg)


def _unit_lower_inverse(n_list, row, col):
    c = n_list[0].shape[0]
    eye = jnp.where(row == col, 1.0, 0.0).astype(F32)
    same = (row // INV_BASE) == (col // INV_BASE)
    pw = [jnp.where(same, n, 0.0) for n in n_list]
    t = [eye - x for x in pw]
    span = 2
    while span < INV_BASE:
        pwb = [x.astype(BF16) for x in pw]
        pw = [_dot(x, x) for x in pwb]
        t = [x + _dot(x.astype(BF16), y.astype(BF16)) for x, y in zip(t, pw)]
        span *= 2
    size = INV_BASE
    while size < c:
        lower_left = ((row // (2 * size)) == (col // (2 * size))) & ((row // size) != (col // size))
        tb = [x.astype(BF16) for x in t]
        et = [_dot(jnp.where(lower_left, n, 0.0).astype(BF16), x).astype(BF16) for n, x in zip(n_list, tb)]
        t = [x - _dot(xb, y) for x, xb, y in zip(t, tb, et)]
        size *= 2
    return t


def _rwkv_kernel(zr_ref, zk_ref, zv_ref, zg_ref, zwa_ref, mu_ref, muwa_ref, w0_ref, wup_ref, a0_ref,
                 aup_ref, kk_ref, ka_ref, rk_ref, lng_ref, lnb_ref, o_ref,
                 state_ref, prev_ref, prevwa_ref, y_ref):
    nb, c = zr_ref.shape[0], zr_ref.shape[1]
    width = RWKV_WIDTH

    @pl.when(pl.program_id(1) == 0)
    def _():
        state_ref[...] = jnp.zeros_like(state_ref)
        prev_ref[...] = jnp.zeros_like(prev_ref)
        prevwa_ref[...] = jnp.zeros_like(prevwa_ref)

    first_row = lax.broadcasted_iota(I32, (c, width), 0) == 0
    first_row_wa = lax.broadcasted_iota(I32, (c, LANES), 0) == 0
    hrow = lax.broadcasted_iota(I32, (width, width), 0) // HEAD_DIM
    hcol = lax.broadcasted_iota(I32, (width, width), 1) // HEAD_DIM
    head_ones = jnp.where(hrow == hcol, 1.0, 0.0).astype(BF16)
    row = lax.broadcasted_iota(I32, (c, c), 0)
    col = lax.broadcasted_iota(I32, (c, c), 1)
    tri = jnp.where(col <= row, 1.0, 0.0).astype(BF16)
    strict = col < row
    incl = col <= row
    hs = [slice(h * HEAD_DIM, (h + 1) * HEAD_DIM) for h in range(RWKV_HEADS)]

    def prepare(bi):
        def shifted(z_ref, idx):
            z = z_ref[bi]
            prev = jnp.where(first_row, prev_ref[bi, idx, 0:1, :], pltpu.roll(z, 1, 0))
            prev_ref[bi, idx, 0:1, :] = z[c - 1:c, :]
            return z + (prev - z) * mu_ref[idx:idx + 1, :]

        r = shifted(zr_ref, 0)
        k = shifted(zk_ref, 1)
        v = shifted(zv_ref, 2)
        g = shifted(zg_ref, 3)
        zwa = zwa_ref[bi]
        prev_wa = jnp.where(first_row_wa, prevwa_ref[bi, 0:1, :], pltpu.roll(zwa, 1, 0))
        prevwa_ref[bi, 0:1, :] = zwa[c - 1:c, :]
        wa = zwa + (prev_wa - zwa) * muwa_ref[...]
        wd = wa[:, :LORA_RANK]
        ad = wa[:, LORA_RANK:]

        lw = -DECAY_SCALE * _sigmoid(w0_ref[...] + _dot_x3(jnp.tanh(wd), wup_ref[...]))
        a = _sigmoid(a0_ref[...] + _dot_x3(ad, aup_ref[...]))
        kk = k * kk_ref[...]
        norm = jnp.sqrt(_head_sum(kk * kk, head_ones))
        kk = kk / jnp.maximum(norm, 1e-12)
        kmod = k * (1.0 + (a - 1.0) * ka_ref[...])
        b = a * kk

        lw_hi, lw_lo = _split(lw)
        cl = _dot(tri, lw_hi) + _dot(tri, lw_lo)
        c_mid = cl[c // 2 - 1:c // 2, :]
        c_end = cl[c - 1:c, :]
        w = jnp.exp(lw)
        e_in1 = jnp.exp(cl - lw)
        e_in2 = e_in1 * w
        e_k = jnp.exp(c_mid - cl)
        e_mid_inv = jnp.exp(-c_mid)
        e_end = e_k * jnp.exp(c_end - c_mid)
        ops = dict(
            q1=kk * e_in1 * e_mid_inv, q2=r * e_in2 * e_mid_inv, k1=b * e_k, k2=kmod * e_k,
            kkd=kk * e_in1, rd=r * e_in2, bd=b * e_end, kd=kmod * e_end, vb=v)
        ops = {name: [x.astype(BF16)[:, s] for s in hs] for name, x in ops.items()}
        return dict(ops, r=r, kmod=kmod, v=v, g=g, d_end=jnp.exp(c_end))

    prep = [prepare(bi) for bi in range(nb)]

    probs = [(bi, h) for bi in range(nb) for h in range(RWKV_HEADS)]
    op = lambda name: [prep[bi][name][h] for bi, h in probs]
    q1, q2, k1, k2, kkd, rd, bd, kd, vh = map(op, ("q1", "q2", "k1", "k2", "kkd", "rd", "bd", "kd", "vb"))
    n = range(len(probs))
    qk = [_dot_nt(jnp.concatenate([q1[i], q2[i]], axis=0), jnp.concatenate([k1[i], k2[i]], axis=0)) for i in n]
    a_ab = [jnp.where(strict, x[:c, :c], 0.0) for x in qk]
    t_inv = _unit_lower_inverse(a_ab, row, col)
    a_rb = [jnp.where(incl, x[c:, :c], 0.0).astype(BF16) for x in qk]
    a_v = [jnp.concatenate([jnp.where(strict, x[:c, c:], 0.0), jnp.where(incl, x[c:, c:], 0.0)],
                           axis=0).astype(BF16) for x in qk]
    s_old = [state_ref[bi, h] for bi, h in probs]
    s_in = [s.astype(BF16) for s in s_old]
    from_v = [_dot(a_v[i], vh[i]) for i in n]
    from_s = [_dot_nt(jnp.concatenate([kkd[i], rd[i]], axis=0), s_in[i]) for i in n]
    rhs = [from_s[i][:c] + from_v[i][:c] for i in n]
    u = [-_dot(t_inv[i].astype(BF16), rhs[i].astype(BF16)) for i in n]
    uv_t = [jnp.concatenate([u[i], prep[bi]["v"][:, hs[h]]], axis=1).T.astype(BF16)
            for i, (bi, h) in enumerate(probs)]
    for i, (bi, h) in enumerate(probs):
        state_ref[bi, h] = (s_old[i] * prep[bi]["d_end"][:, hs[h]]
                            + _dot(uv_t[i][:HEAD_DIM, :], bd[i]) + _dot(uv_t[i][HEAD_DIM:, :], kd[i]))
    for i, (bi, h) in enumerate(probs):
        y_ref[bi, :, hs[h]] = from_s[i][c:] + _dot(a_rb[i], u[i].astype(BF16)) + from_v[i][c:]

    inv_n = 1.0 / HEAD_DIM
    for bi in range(nb):
        pb = prep[bi]
        y = y_ref[bi]
        mean = _head_sum(y, head_ones) * inv_n
        dlt = y - mean
        var = _head_sum(dlt * dlt, head_ones) * inv_n
        yn = dlt * lax.rsqrt(var + GN_EPS) * lng_ref[...] + lnb_ref[...]
        bonus = _head_sum(pb["r"] * pb["kmod"] * rk_ref[...], head_ones) * pb["v"]
        g = pb["g"]
        o_ref[bi] = ((yn + bonus) * (g * _sigmoid(g))).astype(o_ref.dtype)


def _rwkv(z3d, mu4, mu_wa, w0, w_up, a0, a_up, k_k, k_a, r_k, ln_g, ln_b):
    batch, seq, _ = z3d.shape
    c = RWKV_CHUNK
    nb = RWKV_BATCH_ROWS if batch % RWKV_BATCH_ROWS == 0 else 1
    zspec = lambda u: pl.BlockSpec((nb, c, RWKV_WIDTH), lambda b, i, u=u: (b, i, u))
    full = lambda a: pl.BlockSpec(a.shape, lambda b, i: (0,) * a.ndim)
    params = (mu4, mu_wa, w0, w_up, a0, a_up, k_k, k_a, r_k, ln_g, ln_b)
    return pl.pallas_call(
        _rwkv_kernel,
        grid=(batch // nb, seq // c),
        in_specs=[zspec(0), zspec(1), zspec(2), zspec(3),
                  pl.BlockSpec((nb, c, LANES), lambda b, i: (b, i, U_RW_WA))]
                 + [full(a) for a in params],
        out_specs=pl.BlockSpec((nb, c, RWKV_WIDTH), lambda b, i: (b, i, 0)),
        out_shape=jax.ShapeDtypeStruct((batch, seq, RWKV_WIDTH), BF16),
        scratch_shapes=[
            pltpu.VMEM((nb, RWKV_HEADS, HEAD_DIM, HEAD_DIM), F32),
            pltpu.VMEM((nb, 4, 8, RWKV_WIDTH), F32),
            pltpu.VMEM((nb, 8, LANES), F32),
            pltpu.VMEM((nb, c, RWKV_WIDTH), F32),
        ],
        compiler_params=pltpu.CompilerParams(
            dimension_semantics=("parallel", "arbitrary"), vmem_limit_bytes=VMEM_LIMIT),
        name="rwkv7",
    )(z3d, z3d, z3d, z3d, z3d, *params)


def _bit_transpose32(words):
    a = list(words)
    j, mask = 16, 0x0000FFFF
    while j:
        for k in range(32):
            if k & j == 0:
                t = (a[k] ^ lax.shift_right_logical(a[k + j], jnp.int32(j))) & jnp.int32(mask)
                a[k] = a[k] ^ t
                a[k + j] = a[k + j] ^ (t << j)
        j >>= 1
        mask ^= mask << j
    return a


def _dsa_kernel(*refs):
    seq = refs[3].shape[0]
    lax.fori_loop(0, seq // DSA_QUERY_BLOCK, lambda qb, carry: (_dsa_query_block(qb, *refs), carry)[1], 0)


def _dsa_query_block(qb, zq_ref, zg_ref, ziq_ref, zkv_ref, zms_ref, c64_ref, s64_ref, c32_ref, s32_ref, o_ref,
                     k_ref, vt_ref, ik_ref, key_ref, plane_ref):
    seq = zkv_ref.shape[0]
    blk = DSA_QUERY_BLOCK
    sub_rows = ATT_BLOCK
    kb_rows = DSA_KEY_BLOCK
    n_kb = (qb * blk) // kb_rows + 1

    @pl.when(qb == 0)
    def _():
        lane = lax.broadcasted_iota(I32, (kb_rows, LANES), 1)

        def body(t, carry):
            rows = pl.ds(pl.multiple_of(t * kb_rows, kb_rows), kb_rows)
            kv = zkv_ref[rows, :]
            kvr = jnp.where(lane < HEAD_DIM, _rope(kv, c64_ref[rows, :], s64_ref[rows, :], ROT_HALF, HEAD_DIM), kv)
            k_ref[rows, :] = kvr[:, :HEAD_DIM].astype(BF16)
            vt_ref[t] = kvr.T[HEAD_DIM:, :].astype(BF16)
            ms = zms_ref[rows, :]
            ikr = _rope(ms, c32_ref[rows, :], s32_ref[rows, :], IDX_ROT_HALF, IDX_DIM)
            ik_ref[rows, :] = ikr[:, :IDX_DIM].astype(BF16)
            return carry

        lax.fori_loop(0, seq // kb_rows, body, 0)

    r0 = pl.multiple_of(qb * blk, blk)
    qrows = pl.ds(r0, blk)
    c64 = c64_ref[qrows, :]
    s64 = s64_ref[qrows, :]
    c32 = c32_ref[qrows, :]
    s32 = s32_ref[qrows, :]
    q2 = [_rope(zq_ref[qrows, i * LANES:(i + 1) * LANES], c64, s64, ROT_HALF, HEAD_DIM) * QK_SCALE_LOG2
          for i in range(2)]
    q_stack = jnp.concatenate(
        [q2[h // 2][:, (h % 2) * HEAD_DIM:(h % 2 + 1) * HEAD_DIM] for h in range(DSA_HEADS)],
        axis=0).astype(BF16)
    iq2 = [_rope(ziq_ref[qrows, i * LANES:(i + 1) * LANES], c32, s32, IDX_ROT_HALF, IDX_DIM) for i in range(2)]
    iq_stack = jnp.concatenate(
        [iq2[h // 4][:, (h % 4) * IDX_DIM:(h % 4 + 1) * IDX_DIM] for h in range(IDX_HEADS)],
        axis=0).astype(BF16)
    iw_t = zms_ref[qrows, :].T[IDX_DIM:IDX_DIM + IDX_HEADS, :] * ((IDX_HEADS * IDX_DIM) ** -0.5)
    iw_row = jnp.concatenate([iw_t[h:h + 1, :] for h in range(IDX_HEADS)], axis=1)

    kpos_l = lax.broadcasted_iota(I32, (kb_rows, blk), 0)
    qpos = r0 + lax.broadcasted_iota(I32, (kb_rows, blk), 1)

    tri = jnp.where(lax.broadcasted_iota(I32, (sub_rows, sub_rows), 1)
                    <= lax.broadcasted_iota(I32, (sub_rows, sub_rows), 0), 1.0, 0.0).astype(BF16)
    g = zg_ref[qrows, :]
    gate = g * _sigmoid(g)

    def step(n):
        rows = [slice(kb * kb_rows, (kb + 1) * kb_rows) for kb in range(n)]
        groups = kb_rows // BIT_GROUP_ROWS
        causal = (n - 1) * kb_rows + kpos_l <= qpos

        for kb in range(n):
            s = jnp.maximum(_dot_nt(ik_ref[rows[kb], :], iq_stack), 0.0) * iw_row
            sc = s[:, 0:blk]
            for h in range(1, IDX_HEADS):
                sc = sc + s[:, h * blk:(h + 1) * blk]
            if kb == n - 1:
                sc = jnp.where(causal, sc, -jnp.inf)
            bits = pltpu.bitcast(sc, I32)
            key = bits ^ ((bits >> 31) & 0x7FFFFFFF)
            key_ref[rows[kb], :] = key
            ku = key ^ INT_MIN
            for grp in range(groups):
                words = [ku[(grp * 32 + v) * 8:(grp * 32 + v + 1) * 8, :] for v in range(32)]
                for b, plane in enumerate(_bit_transpose32(words)):
                    plane_ref[kb * groups + grp, b] = plane

        n_groups = n * groups

        def bit_body(i, carry):
            active, k_rem, thr_u = carry
            ones = [active[g] & plane_ref[g, i] for g in range(n_groups)]
            acc = lax.population_count(ones[0])
            for g in range(1, n_groups):
                acc = acc + lax.population_count(ones[g])
            cnt = acc.sum(axis=0, keepdims=True)
            dec = cnt >= k_rem
            k_rem = jnp.where(dec, k_rem, k_rem - cnt)
            thr_u = thr_u | jnp.where(dec, jnp.int32(1) << (31 - i), 0)
            dec8 = jnp.broadcast_to(dec, (8, blk))
            active = tuple(jnp.where(dec8, ones[g], active[g] ^ ones[g]) for g in range(n_groups))
            return active, k_rem, thr_u

        def search():
            init = (tuple(jnp.full((8, blk), -1, I32) for _ in range(n_groups)),
                    jnp.full((1, blk), DSA_TOPK, I32), jnp.zeros((1, blk), I32))
            _, k_rem, thr_u = lax.fori_loop(0, 32, bit_body, init)
            return thr_u ^ INT_MIN, k_rem

        keep_all = lambda: (jnp.full((1, blk), INT_MIN, I32), jnp.zeros((1, blk), I32))
        if (n - 1) * kb_rows >= DSA_TOPK:
            thr, need = search()
        else:
            thr, need = lax.cond((qb + 1) * blk > DSA_TOPK, search, keep_all)
        need = need.astype(F32)

        wide = DSA_HEADS * blk
        m = jnp.full((1, wide), NEG_BIG, F32)
        l = jnp.zeros((1, wide), F32)
        acc = jnp.zeros((HEAD_DIM, wide), F32)
        run = jnp.zeros((1, blk), F32)
        for kb in range(n):
            keys = key_ref[rows[kb], :]
            eq = keys == thr
            eq_b = jnp.where(eq, 1.0, 0.0).astype(BF16)
            local = [_dot(tri, eq_b[sub * sub_rows:(sub + 1) * sub_rows, :]) for sub in range(kb_rows // sub_rows)]
            ranks = []
            for x in local:
                ranks.append(x + run)
                run = run + x[sub_rows - 1:sub_rows, :]
            rank = jnp.concatenate(ranks, axis=0)
            take = (keys > thr) | (eq & (rank <= need))
            if kb == n - 1:
                take = take & causal
            bias = jnp.where(take, 0.0, NEG_BIG)
            s = _dot_nt(k_ref[rows[kb], :], q_stack)
            s = s + jnp.concatenate([bias] * DSA_HEADS, axis=1)
            m_new = jnp.maximum(m, jnp.max(s, axis=0, keepdims=True))
            p = jnp.exp2(s - m_new)
            alpha = jnp.exp2(m - m_new)
            l = alpha * l + jnp.sum(p, axis=0, keepdims=True)
            acc = alpha * acc + _dot(vt_ref[kb], p.astype(BF16))
            m = m_new

        o_t = acc / l
        o = jnp.concatenate(
            [jnp.concatenate([o_t[:, (2 * i) * blk:(2 * i + 1) * blk], o_t[:, (2 * i + 1) * blk:(2 * i + 2) * blk]],
                             axis=0).T for i in range(DSA_HEADS // 2)], axis=1)
        o_ref[qrows, :] = (o * gate).astype(o_ref.dtype)

    for n in range(1, seq // kb_rows + 1):
        pl.when(n_kb == n)(functools.partial(step, n))


def _dsa(z2d, batch, seq, c64, s64, c32, s32):
    blk = DSA_QUERY_BLOCK
    qspec = lambda u2: pl.BlockSpec((seq, 2 * LANES), lambda b, u2=u2: (b, u2))
    seqspec = lambda u: pl.BlockSpec((seq, LANES), lambda b, u=u: (b, u))
    tab = pl.BlockSpec((seq, LANES), lambda b: (0, 0))
    return pl.pallas_call(
        _dsa_kernel,
        grid=(batch,),
        in_specs=[qspec(U_DS_Q // 2), qspec(U_DS_G // 2), qspec(U_DS_IQ // 2),
                  seqspec(U_DS_KV), seqspec(U_DS_MISC), tab, tab, tab, tab],
        out_specs=pl.BlockSpec((seq, DSA_WIDTH), lambda b: (b, 0)),
        out_shape=jax.ShapeDtypeStruct((batch * seq, DSA_WIDTH), BF16),
        scratch_shapes=[
            pltpu.VMEM((seq, HEAD_DIM), BF16),
            pltpu.VMEM((seq // DSA_KEY_BLOCK, HEAD_DIM, DSA_KEY_BLOCK), BF16),
            pltpu.VMEM((seq, IDX_DIM), BF16),
            pltpu.VMEM((seq, blk), I32),
            pltpu.VMEM((seq // BIT_GROUP_ROWS, 32, 8, blk), I32),
        ],
        compiler_params=pltpu.CompilerParams(
            dimension_semantics=("parallel",), vmem_limit_bytes=VMEM_LIMIT),
        name="dsa",
    )(z2d, z2d, z2d, z2d, z2d, c64, s64, c32, s32)


def _dil_group(gi, rate, q_ref, k_ref, v_ref, g_ref, c64_ref, s64_ref, qr_ref, kr_ref, o_s, lse_s):
    seq = q_ref.shape[0]
    blk = ATT_BLOCK
    rope_rows = 4 * blk
    nrow = seq // rope_rows

    def rope_body(t, carry):
        rows = pl.ds(pl.multiple_of(t * rope_rows, rope_rows), rope_rows)
        ct = c64_ref[rows, :]
        st = s64_ref[rows, :]
        q = q_ref[rows, :]
        k = k_ref[rows, :]
        qr_ref[rows, :] = (q * ct + pltpu.roll(q, HEAD_DIM, 1) * st) * QK_SCALE_LOG2
        kr_ref[rows, :] = k * ct + pltpu.roll(k, HEAD_DIM, 1) * st
        return carry

    lax.fori_loop(0, nrow, rope_body, 0)

    nblk = seq // (blk * rate)
    qi = lax.broadcasted_iota(I32, (blk, 2 * blk), 0)
    kj = lax.broadcasted_iota(I32, (blk, 2 * blk), 1)
    band = (kj >= qi) & (kj <= qi + DIL_STEPS)

    def strided(ref, start):
        if rate == 1:
            return ref[pl.ds(start, blk), :]
        return ref[pl.ds(start, blk, stride=rate), :]

    def store(ref, start, val):
        if rate == 1:
            ref[gi, pl.ds(start, blk), :] = val
        else:
            ref[gi, pl.ds(start, blk, stride=rate), :] = val

    first_q = lax.broadcasted_iota(I32, (blk, LANES), 1) < HEAD_DIM
    first_kv = lax.broadcasted_iota(I32, (2 * blk, LANES), 1) < HEAD_DIM
    lane64 = lax.broadcasted_iota(I32, (blk, LANES), 1) % HEAD_DIM
    rest = (HEAD_DIM - 2 * ROT_HALF) // 2
    first_qk = (lane64 < ROT_HALF) | ((lane64 >= 2 * ROT_HALF) & (lane64 < 2 * ROT_HALF + rest))
    unroll = DIL_UNROLL

    def body(it, carry):
        starts, qs, ks, vs, masks = [], [], [], [], []
        for u in range(unroll):
            cidx = it * unroll + u
            rho = cidx // nblk
            j = cidx % nblk
            start = j * (blk * rate) + rho
            pstart = jnp.maximum(j - 1, 0) * (blk * rate) + rho
            starts.append(start)
            qs.append(strided(qr_ref, start))
            ks.append(jnp.concatenate([strided(kr_ref, pstart), strided(kr_ref, start)], axis=0).astype(BF16))
            vs.append(jnp.concatenate([strided(v_ref, pstart), strided(v_ref, start)], axis=0))
            masks.append(band & ((kj >= blk) | (j > 0)))
        prob = [(u, h) for u in range(unroll) for h in range(2)]
        qh = [jnp.where(first_qk == (h == 0), qs[u], 0.0).astype(BF16) for u, h in prob]
        vh = [jnp.where(first_kv == (h == 0), vs[u], 1.0).astype(BF16) for u, h in prob]
        s = [jnp.where(masks[u], _dot_nt(qh[i], ks[u]), -jnp.inf) for i, (u, h) in enumerate(prob)]
        m = [jnp.max(x, axis=-1, keepdims=True) for x in s]
        r = [_dot(jnp.exp2(s[i] - m[i]).astype(BF16), vh[i]) for i in range(len(prob))]
        for u in range(unroll):
            r0, r1 = r[2 * u], r[2 * u + 1]
            o = jnp.where(first_q, r0, r1)
            l = pltpu.roll(jnp.where(first_q, r1, r0), HEAD_DIM, 1)
            lse = jnp.where(first_q, m[2 * u], m[2 * u + 1]) * math.log(2.0) + jnp.log(l)
            gate = strided(g_ref, starts[u])
            store(o_s, starts[u], o / l * (gate * _sigmoid(gate)))
            store(lse_s, starts[u], lse)
        return carry

    lax.fori_loop(0, rate * nblk // unroll, body, 0)


def _dil_kernel(q_ref, k_ref, v_ref, g_ref, c64_ref, s64_ref, o_ref, qr_ref, kr_ref, o_s, lse_s):
    grp = pl.program_id(1)
    for gi, rate in enumerate(DIL_RATES):
        pl.when(grp == gi)(functools.partial(
            _dil_group, gi, rate, q_ref, k_ref, v_ref, g_ref, c64_ref, s64_ref, qr_ref, kr_ref, o_s, lse_s))

    @pl.when(grp == len(DIL_RATES) - 1)
    def _():
        blk = ATT_BLOCK

        def body(t, carry):
            rows = pl.ds(pl.multiple_of(t * blk, blk), blk)
            ls = [lse_s[i, rows, :] for i in range(3)]
            top = jnp.maximum(jnp.maximum(ls[0], ls[1]), ls[2])
            es = [jnp.exp(x - top) for x in ls]
            inv = 1.0 / (es[0] + es[1] + es[2])
            o_ref[rows, :] = jnp.concatenate(
                [o_s[i, rows, :] * (es[i] * inv) for i in range(3)], axis=1).astype(o_ref.dtype)
            return carry

        lax.fori_loop(0, q_ref.shape[0] // blk, body, 0)


def _dilated(z2d, batch, seq, c64, s64):
    ng = len(DIL_RATES)
    zspec = lambda base: pl.BlockSpec((seq, LANES), lambda b, g, base=base: (b, base + g))
    tab = pl.BlockSpec((seq, LANES), lambda b, g: (0, 0))
    return pl.pallas_call(
        _dil_kernel,
        grid=(batch, ng),
        in_specs=[zspec(U_DL), zspec(U_DL + 3), zspec(U_DL + 6), zspec(U_DL + 9), tab, tab],
        out_specs=pl.BlockSpec((seq, DIL_WIDTH), lambda b, g: (b, 0)),
        out_shape=jax.ShapeDtypeStruct((batch * seq, DIL_WIDTH), BF16),
        scratch_shapes=[
            pltpu.VMEM((seq, LANES), F32),
            pltpu.VMEM((seq, LANES), F32),
            pltpu.VMEM((ng, seq, LANES), F32),
            pltpu.VMEM((ng, seq, LANES), F32),
        ],
        compiler_params=pltpu.CompilerParams(
            dimension_semantics=("parallel", "arbitrary"), vmem_limit_bytes=VMEM_LIMIT),
        name="dilated",
    )(z2d, z2d, z2d, z2d, c64, s64)


def _rope_tables(seq, head, half):
    inv = ROPE_THETA ** (-jnp.arange(half, dtype=F32) / half)
    ang = jnp.arange(seq, dtype=F32)[:, None] * inv[None, :]
    cos, sin = jnp.cos(ang), jnp.sin(ang)
    lane = jnp.arange(LANES) % head
    idx = lane % half
    cos_t = jnp.where(lane[None, :] < 2 * half, cos[:, idx], 1.0)
    sin_t = jnp.where(lane[None, :] < half, -sin[:, idx], jnp.where(lane[None, :] < 2 * half, sin[:, idx], 0.0))
    return cos_t, sin_t


def _dil_pair_lanes():
    half, rest = ROT_HALF, (HEAD_DIM - 2 * ROT_HALF) // 2
    src = []
    for part in range(2):
        for head in range(2):
            src += [head * HEAD_DIM + part * half + i for i in range(half)]
        for head in range(2):
            src += [head * HEAD_DIM + 2 * half + part * rest + i for i in range(rest)]
    return src


def _rope_tables_paired(seq):
    half = ROT_HALF
    inv = ROPE_THETA ** (-jnp.arange(half, dtype=F32) / half)
    ang = jnp.arange(seq, dtype=F32)[:, None] * inv[None, :]
    cos, sin = jnp.cos(ang), jnp.sin(ang)
    lane = jnp.arange(LANES)[None, :]
    idx = jnp.arange(LANES) % half
    rotated = lane % HEAD_DIM < 2 * half
    cos_t = jnp.where(rotated, cos[:, idx], 1.0)
    sin_t = jnp.where(rotated, jnp.where(lane < HEAD_DIM, -sin[:, idx], sin[:, idx]), 0.0)
    return cos_t, sin_t


def _in_col_sources():
    rw = RWKV_WIDTH
    cols = lambda o, wdt: list(range(o, o + wdt))
    o = 0
    r, k, v, g = (cols(o + i * rw, rw) for i in range(4))
    o = 4 * rw
    wd, ad = cols(o, LORA_RANK), cols(o + LORA_RANK, LORA_RANK)
    o += 2 * LORA_RANK
    parts = []
    for wdt in (DSA_WIDTH, HEAD_DIM, HEAD_DIM, IDX_HEADS * IDX_DIM, IDX_DIM, IDX_HEADS, DSA_WIDTH):
        parts.append(cols(o, wdt))
        o += wdt
    dq, dk, dv, diq, dik, diw, dg = parts
    cq, ck, cv, cg = (cols(o + i * DIL_WIDTH, DIL_WIDTH) for i in range(4))
    o += 4 * DIL_WIDTH
    pair = [t * LANES + s for t in range(DIL_WIDTH // LANES) for s in _dil_pair_lanes()]
    cq, ck = [cq[i] for i in pair], [ck[i] for i in pair]
    pad = [o] * (LANES - IDX_DIM - IDX_HEADS)
    src = r + k + v + g + cq + ck + cv + cg + dq + dg + diq + dk + dv + dik + diw + pad + wd + ad
    assert len(src) == Z_COLS
    return src


def _permute_in_cols(w):
    return jnp.take(w, jnp.asarray(_in_col_sources(), I32), axis=-1, mode="fill", fill_value=0)


def kernel(x, p, norm_g, w_in, tshift_mu, rwkv_w0, rwkv_w_up, rwkv_a0, rwkv_a_up, rwkv_k_k, rwkv_k_a, rwkv_r_k,
           rwkv_ln_g, rwkv_ln_b, w_out, ple_norm_g, ple_w_gate, ple_w_proj, final_norm_g):
    batch, seq, d_model = x.shape
    depth = w_in.shape[0]
    n = batch * seq
    assert seq % (ATT_BLOCK * max(DIL_RATES)) == 0 and seq >= 4 * DSA_TOPK

    c64, s64 = _rope_tables(seq, HEAD_DIM, ROT_HALF)
    c32, s32 = _rope_tables(seq, IDX_DIM, IDX_ROT_HALF)
    c64p, s64p = _rope_tables_paired(seq)
    w_in_k = _permute_in_cols(w_in.astype(BF16))
    rw = RWKV_WIDTH
    row = lambda a: a.reshape(1, -1)

    x2d = x.reshape(n, d_model)
    p3d = p.reshape(depth, n, p.shape[-1])
    for i in range(depth):
        z = _in_proj(x2d, row(norm_g[i]), w_in_k, i)
        mu = tshift_mu[i]
        y_a = _rwkv(z.reshape(batch, seq, Z_COLS), mu[:4 * rw].reshape(4, rw), row(mu[4 * rw:]), row(rwkv_w0[i]), rwkv_w_up[i],
                    row(rwkv_a0[i]), rwkv_a_up[i], row(rwkv_k_k[i]), row(rwkv_k_a[i]), row(rwkv_r_k[i]),
                    row(rwkv_ln_g[i]), row(rwkv_ln_b[i])).reshape(n, rw)
        y_b = _dsa(z, batch, seq, c64, s64, c32, s32)
        y_c = _dilated(z, batch, seq, c64p, s64p)
        wo = w_out[i].astype(BF16)
        x2d = _out_proj(x2d, y_a, y_b, y_c, p3d, i, wo[:rw], wo[rw:rw + DSA_WIDTH],
                        wo[rw + DSA_WIDTH:], row(ple_norm_g[i]), ple_w_gate[i].astype(BF16),
                        ple_w_proj[i].astype(BF16), row(final_norm_g), final=(i == depth - 1))
    return x2d.reshape(batch, seq, d_model)
```

```python
import functools
import math

import jax
import jax.numpy as jnp
from jax import lax
from jax.experimental import pallas as pl
from jax.experimental.pallas import tpu as pltpu

F32 = jnp.float32
BF16 = jnp.bfloat16
I32 = jnp.int32

HEAD_DIM = 64
ROPE_THETA = 500000.0
ROT_HALF = HEAD_DIM // 8
NORM_EPS = 1e-6
RWKV_HEADS = 6
RWKV_WIDTH = RWKV_HEADS * HEAD_DIM
LORA_RANK = 64
GN_EPS = 64e-5
DECAY_SCALE = math.exp(-0.5)
DSA_HEADS = 4
DSA_WIDTH = DSA_HEADS * HEAD_DIM
IDX_HEADS = 8
IDX_DIM = 32
IDX_ROT_HALF = IDX_DIM // 8
DSA_TOPK = 256
DIL_RATES = (1, 4, 16)
DIL_STEPS = 128
DIL_UNROLL = 4
DIL_WIDTH = 384
ATT_BLOCK = 128
DSA_QUERY_BLOCK = 256
DSA_KEY_BLOCK = 512

LANES = 128
MXU_WIDTH = 256
OUT_SUB_TILES = 4
VMEM_LIMIT = 48 * 1024 * 1024

U_RW = 0
U_DL = 12
U_DS_Q = 24
U_DS_G = 26
U_DS_IQ = 28
U_DS_KV = 30
U_DS_MISC = 31
U_RW_WA = 32
Z_UNITS = 33
Z_COLS = Z_UNITS * LANES

RWKV_CHUNK = 128
RWKV_BATCH_ROWS = 4
INV_BASE = 16
QK_SCALE_LOG2 = HEAD_DIM ** -0.5 * math.log2(math.e)
NEG_BIG = -1e30
BIT_GROUP_ROWS = 256
INT_MIN = -2 ** 31


def _dot(a, b):
    return jnp.dot(a, b, preferred_element_type=F32)


def _dot_nt(a, b):
    return lax.dot_general(a, b, (((1,), (1,)), ((), ())), preferred_element_type=F32)


def _split(x):
    hi = x.astype(BF16)
    lo = (x - hi.astype(F32)).astype(BF16)
    return hi, lo


def _head_sum(x, head_ones):
    return _dot(x.astype(BF16), head_ones)


def _dot_x3(x, w):
    xh, xl = _split(x)
    wh, wl = _split(w)
    return _dot(xh, wh) + (_dot(xh, wl) + _dot(xl, wh))


def _sigmoid(x):
    return 1.0 / (1.0 + jnp.exp(-x))


def _rope(x, cos_t, sin_t, half, head):
    lane = lax.broadcasted_iota(I32, x.shape, 1) % head
    partner = jnp.where(lane < half, pltpu.roll(x, LANES - half, 1), pltpu.roll(x, half, 1))
    return x * cos_t + partner * sin_t


def _in_proj_kernel(x_ref, g_ref, w_ref, z_ref, *, tn):
    x = x_ref[...]
    ms = jnp.mean(x * x, axis=-1, keepdims=True)
    h = (x * lax.rsqrt(ms + NORM_EPS) * g_ref[...]).astype(BF16)
    for c in range(0, Z_COLS, tn):
        wdt = min(tn, Z_COLS - c)
        z_ref[:, c:c + wdt] = _dot(h, w_ref[:, c:c + wdt])


def _in_proj(x2d, g, w_all, layer, tm=512, tn=MXU_WIDTH):
    n, d = x2d.shape
    return pl.pallas_call(
        functools.partial(_in_proj_kernel, tn=tn),
        grid=(n // tm,),
        in_specs=[
            pl.BlockSpec((tm, d), lambda i: (i, 0)),
            pl.BlockSpec((1, d), lambda i: (0, 0)),
            pl.BlockSpec((None, d, Z_COLS), lambda i: (layer, 0, 0)),
        ],
        out_specs=pl.BlockSpec((tm, Z_COLS), lambda i: (i, 0)),
        out_shape=jax.ShapeDtypeStruct((n, Z_COLS), F32),
        compiler_params=pltpu.CompilerParams(
            dimension_semantics=("parallel",), vmem_limit_bytes=VMEM_LIMIT),
        name="in_proj",
    )(x2d, g, w_all)


def _out_proj_kernel(x_ref, ya_ref, yb_ref, yc_ref, p_ref, wa_ref, wb_ref, wc_ref, ng_ref, wg_ref,
                     wp_ref, fg_ref, o_ref, *, final):
    sub = x_ref.shape[0] // OUT_SUB_TILES
    rows = [slice(i * sub, (i + 1) * sub) for i in range(OUT_SUB_TILES)]
    acc = [x_ref[r, :] + _dot(ya_ref[r, :], wa_ref[...]) for r in rows]
    acc = [a + _dot(yb_ref[r, :], wb_ref[...]) for a, r in zip(acc, rows)]
    acc = [a + _dot(yc_ref[r, :], wc_ref[...]) for a, r in zip(acc, rows)]
    hn = [(a * lax.rsqrt(jnp.mean(a * a, axis=-1, keepdims=True) + NORM_EPS) * ng_ref[...]).astype(BF16)
          for a in acc]
    gate = [_sigmoid(_dot(h, wg_ref[...])) for h in hn]
    out = [a + g * _dot(p_ref[r, :].astype(BF16), wp_ref[...]) for a, g, r in zip(acc, gate, rows)]
    if final:
        out = [o * lax.rsqrt(jnp.mean(o * o, axis=-1, keepdims=True) + NORM_EPS) * fg_ref[...] for o in out]
    for o, r in zip(out, rows):
        o_ref[r, :] = o


def _out_proj(x2d, ya, yb, yc, p3d, layer, wa, wb, wc, ng, wg, wp, fg, final, tm=1024):
    n, d = x2d.shape
    row = lambda w: pl.BlockSpec((tm, w), lambda i: (i, 0))
    full = lambda a: pl.BlockSpec(a.shape, lambda i: (0, 0))
    return pl.pallas_call(
        functools.partial(_out_proj_kernel, final=final),
        grid=(n // tm,),
        in_specs=[row(d), row(ya.shape[1]), row(yb.shape[1]), row(yc.shape[1]),
                  pl.BlockSpec((None, tm, p3d.shape[2]), lambda i: (layer, i, 0)),
                  full(wa), full(wb), full(wc), full(ng), full(wg), full(wp), full(fg)],
        out_specs=row(d),
        out_shape=jax.ShapeDtypeStruct((n, d), F32),
        compiler_params=pltpu.CompilerParams(
            dimension_semantics=("parallel",), vmem_limit_bytes=VMEM_LIMIT),
        name="out_proj",
    )(x2d, ya, yb, yc, p3d, wa, wb, wc, ng, wg, wp, fg)


def _unit_lower_inverse(n_list, row, col):
    c = n_list[0].shape[0]
    eye = jnp.where(row == col, 1.0, 0.0).astype(F32)
    same = (row // INV_BASE) == (col // INV_BASE)
    pw = [jnp.where(same, n, 0.0) for n in n_list]
    t = [eye - x for x in pw]
    span = 2
    while span < INV_BASE:
        pwb = [x.astype(BF16) for x in pw]
        pw = [_dot(x, x) for x in pwb]
        t = [x + _dot(x.astype(BF16), y.astype(BF16)) for x, y in zip(t, pw)]
        span *= 2
    size = INV_BASE
    while size < c:
        lower_left = ((row // (2 * size)) == (col // (2 * size))) & ((row // size) != (col // size))
        tb = [x.astype(BF16) for x in t]
        et = [_dot(jnp.where(lower_left, n, 0.0).astype(BF16), x).astype(BF16) for n, x in zip(n_list, tb)]
        t = [x - _dot(xb, y) for x, xb, y in zip(t, tb, et)]
        size *= 2
    return t


def _rwkv_kernel(zr_ref, zk_ref, zv_ref, zg_ref, zwa_ref, mu_ref, muwa_ref, w0_ref, wup_ref, a0_ref,
                 aup_ref, kk_ref, ka_ref, rk_ref, lng_ref, lnb_ref, o_ref,
                 state_ref, prev_ref, prevwa_ref, y_ref):
    nb, c = zr_ref.shape[0], zr_ref.shape[1]
    width = RWKV_WIDTH

    @pl.when(pl.program_id(1) == 0)
    def _():
        state_ref[...] = jnp.zeros_like(state_ref)
        prev_ref[...] = jnp.zeros_like(prev_ref)
        prevwa_ref[...] = jnp.zeros_like(prevwa_ref)

    first_row = lax.broadcasted_iota(I32, (c, width), 0) == 0
    first_row_wa = lax.broadcasted_iota(I32, (c, LANES), 0) == 0
    hrow = lax.broadcasted_iota(I32, (width, width), 0) // HEAD_DIM
    hcol = lax.broadcasted_iota(I32, (width, width), 1) // HEAD_DIM
    head_ones = jnp.where(hrow == hcol, 1.0, 0.0).astype(BF16)
    row = lax.broadcasted_iota(I32, (c, c), 0)
    col = lax.broadcasted_iota(I32, (c, c), 1)
    tri = jnp.where(col <= row, 1.0, 0.0).astype(BF16)
    strict = col < row
    incl = col <= row
    hs = [slice(h * HEAD_DIM, (h + 1) * HEAD_DIM) for h in range(RWKV_HEADS)]

    def prepare(bi):
        def shifted(z_ref, idx):
            z = z_ref[bi]
            prev = jnp.where(first_row, prev_ref[bi, idx, 0:1, :], pltpu.roll(z, 1, 0))
            prev_ref[bi, idx, 0:1, :] = z[c - 1:c, :]
            return z + (prev - z) * mu_ref[idx:idx + 1, :]

        r = shifted(zr_ref, 0)
        k = shifted(zk_ref, 1)
        v = shifted(zv_ref, 2)
        g = shifted(zg_ref, 3)
        zwa = zwa_ref[bi]
        prev_wa = jnp.where(first_row_wa, prevwa_ref[bi, 0:1, :], pltpu.roll(zwa, 1, 0))
        prevwa_ref[bi, 0:1, :] = zwa[c - 1:c, :]
        wa = zwa + (prev_wa - zwa) * muwa_ref[...]
        wd = wa[:, :LORA_RANK]
        ad = wa[:, LORA_RANK:]

        lw = -DECAY_SCALE * _sigmoid(w0_ref[...] + _dot_x3(jnp.tanh(wd), wup_ref[...]))
        a = _sigmoid(a0_ref[...] + _dot_x3(ad, aup_ref[...]))
        kk = k * kk_ref[...]
        norm = jnp.sqrt(_head_sum(kk * kk, head_ones))
        kk = kk / jnp.maximum(norm, 1e-12)
        kmod = k * (1.0 + (a - 1.0) * ka_ref[...])
        b = a * kk

        lw_hi, lw_lo = _split(lw)
        cl = _dot(tri, lw_hi) + _dot(tri, lw_lo)
        c_mid = cl[c // 2 - 1:c // 2, :]
        c_end = cl[c - 1:c, :]
        w = jnp.exp(lw)
        e_in1 = jnp.exp(cl - lw)
        e_in2 = e_in1 * w
        e_k = jnp.exp(c_mid - cl)
        e_mid_inv = jnp.exp(-c_mid)
        e_end = e_k * jnp.exp(c_end - c_mid)
        ops = dict(
            q1=kk * e_in1 * e_mid_inv, q2=r * e_in2 * e_mid_inv, k1=b * e_k, k2=kmod * e_k,
            kkd=kk * e_in1, rd=r * e_in2, bd=b * e_end, kd=kmod * e_end, vb=v)
        ops = {name: [x.astype(BF16)[:, s] for s in hs] for name, x in ops.items()}
        return dict(ops, r=r, kmod=kmod, v=v, g=g, d_end=jnp.exp(c_end))

    prep = [prepare(bi) for bi in range(nb)]

    probs = [(bi, h) for bi in range(nb) for h in range(RWKV_HEADS)]
    op = lambda name: [prep[bi][name][h] for bi, h in probs]
    q1, q2, k1, k2, kkd, rd, bd, kd, vh = map(op, ("q1", "q2", "k1", "k2", "kkd", "rd", "bd", "kd", "vb"))
    n = range(len(probs))
    qk = [_dot_nt(jnp.concatenate([q1[i], q2[i]], axis=0), jnp.concatenate([k1[i], k2[i]], axis=0)) for i in n]
    a_ab = [jnp.where(strict, x[:c, :c], 0.0) for x in qk]
    t_inv = _unit_lower_inverse(a_ab, row, col)
    a_rb = [jnp.where(incl, x[c:, :c], 0.0).astype(BF16) for x in qk]
    a_v = [jnp.concatenate([jnp.where(strict, x[:c, c:], 0.0), jnp.where(incl, x[c:, c:], 0.0)],
                           axis=0).astype(BF16) for x in qk]
    s_old = [state_ref[bi, h] for bi, h in probs]
    s_in = [s.astype(BF16) for s in s_old]
    from_v = [_dot(a_v[i], vh[i]) for i in n]
    from_s = [_dot_nt(jnp.concatenate([kkd[i], rd[i]], axis=0), s_in[i]) for i in n]
    rhs = [from_s[i][:c] + from_v[i][:c] for i in n]
    u = [-_dot(t_inv[i].astype(BF16), rhs[i].astype(BF16)) for i in n]
    uv_t = [jnp.concatenate([u[i], prep[bi]["v"][:, hs[h]]], axis=1).T.astype(BF16)
            for i, (bi, h) in enumerate(probs)]
    for i, (bi, h) in enumerate(probs):
        state_ref[bi, h] = (s_old[i] * prep[bi]["d_end"][:, hs[h]]
                            + _dot(uv_t[i][:HEAD_DIM, :], bd[i]) + _dot(uv_t[i][HEAD_DIM:, :], kd[i]))
    for i, (bi, h) in enumerate(probs):
        y_ref[bi, :, hs[h]] = from_s[i][c:] + _dot(a_rb[i], u[i].astype(BF16)) + from_v[i][c:]

    inv_n = 1.0 / HEAD_DIM
    for bi in range(nb):
        pb = prep[bi]
        y = y_ref[bi]
        mean = _head_sum(y, head_ones) * inv_n
        dlt = y - mean
        var = _head_sum(dlt * dlt, head_ones) * inv_n
        yn = dlt * lax.rsqrt(var + GN_EPS) * lng_ref[...] + lnb_ref[...]
        bonus = _head_sum(pb["r"] * pb["kmod"] * rk_ref[...], head_ones) * pb["v"]
        g = pb["g"]
        o_ref[bi] = ((yn + bonus) * (g * _sigmoid(g))).astype(o_ref.dtype)


def _rwkv(z3d, mu4, mu_wa, w0, w_up, a0, a_up, k_k, k_a, r_k, ln_g, ln_b):
    batch, seq, _ = z3d.shape
    c = RWKV_CHUNK
    nb = RWKV_BATCH_ROWS if batch % RWKV_BATCH_ROWS == 0 else 1
    zspec = lambda u: pl.BlockSpec((nb, c, RWKV_WIDTH), lambda b, i, u=u: (b, i, u))
    full = lambda a: pl.BlockSpec(a.shape, lambda b, i: (0,) * a.ndim)
    params = (mu4, mu_wa, w0, w_up, a0, a_up, k_k, k_a, r_k, ln_g, ln_b)
    return pl.pallas_call(
        _rwkv_kernel,
        grid=(batch // nb, seq // c),
        in_specs=[zspec(0), zspec(1), zspec(2), zspec(3),
                  pl.BlockSpec((nb, c, LANES), lambda b, i: (b, i, U_RW_WA))]
                 + [full(a) for a in params],
        out_specs=pl.BlockSpec((nb, c, RWKV_WIDTH), lambda b, i: (b, i, 0)),
        out_shape=jax.ShapeDtypeStruct((batch, seq, RWKV_WIDTH), BF16),
        scratch_shapes=[
            pltpu.VMEM((nb, RWKV_HEADS, HEAD_DIM, HEAD_DIM), F32),
            pltpu.VMEM((nb, 4, 8, RWKV_WIDTH), F32),
            pltpu.VMEM((nb, 8, LANES), F32),
            pltpu.VMEM((nb, c, RWKV_WIDTH), F32),
        ],
        compiler_params=pltpu.CompilerParams(
            dimension_semantics=("parallel", "arbitrary"), vmem_limit_bytes=VMEM_LIMIT),
        name="rwkv7",
    )(z3d, z3d, z3d, z3d, z3d, *params)


def _bit_transpose32(words):
    a = list(words)
    j, mask = 16, 0x0000FFFF
    while j:
        for k in range(32):
            if k & j == 0:
                t = (a[k] ^ lax.shift_right_logical(a[k + j], jnp.int32(j))) & jnp.int32(mask)
                a[k] = a[k] ^ t
                a[k + j] = a[k + j] ^ (t << j)
        j >>= 1
        mask ^= mask << j
    return a


def _dsa_kernel(*refs):
    seq = refs[3].shape[0]
    lax.fori_loop(0, seq // DSA_QUERY_BLOCK, lambda qb, carry: (_dsa_query_block(qb, *refs), carry)[1], 0)


def _dsa_query_block(qb, zq_ref, zg_ref, ziq_ref, zkv_ref, zms_ref, c64_ref, s64_ref, c32_ref, s32_ref, o_ref,
                     k_ref, vt_ref, ik_ref, key_ref, plane_ref):
    seq = zkv_ref.shape[0]
    blk = DSA_QUERY_BLOCK
    sub_rows = ATT_BLOCK
    kb_rows = DSA_KEY_BLOCK
    n_kb = (qb * blk) // kb_rows + 1

    @pl.when(qb == 0)
    def _():
        lane = lax.broadcasted_iota(I32, (kb_rows, LANES), 1)

        def body(t, carry):
            rows = pl.ds(pl.multiple_of(t * kb_rows, kb_rows), kb_rows)
            kv = zkv_ref[rows, :]
            kvr = jnp.where(lane < HEAD_DIM, _rope(kv, c64_ref[rows, :], s64_ref[rows, :], ROT_HALF, HEAD_DIM), kv)
            k_ref[rows, :] = kvr[:, :HEAD_DIM].astype(BF16)
            vt_ref[t] = kvr.T[HEAD_DIM:, :].astype(BF16)
            ms = zms_ref[rows, :]
            ikr = _rope(ms, c32_ref[rows, :], s32_ref[rows, :], IDX_ROT_HALF, IDX_DIM)
            ik_ref[rows, :] = ikr[:, :IDX_DIM].astype(BF16)
            return carry

        lax.fori_loop(0, seq // kb_rows, body, 0)

    r0 = pl.multiple_of(qb * blk, blk)
    qrows = pl.ds(r0, blk)
    c64 = c64_ref[qrows, :]
    s64 = s64_ref[qrows, :]
    c32 = c32_ref[qrows, :]
    s32 = s32_ref[qrows, :]
    q2 = [_rope(zq_ref[qrows, i * LANES:(i + 1) * LANES], c64, s64, ROT_HALF, HEAD_DIM) * QK_SCALE_LOG2
          for i in range(2)]
    q_stack = jnp.concatenate(
        [q2[h // 2][:, (h % 2) * HEAD_DIM:(h % 2 + 1) * HEAD_DIM] for h in range(DSA_HEADS)],
        axis=0).astype(BF16)
    iq2 = [_rope(ziq_ref[qrows, i * LANES:(i + 1) * LANES], c32, s32, IDX_ROT_HALF, IDX_DIM) for i in range(2)]
    iq_stack = jnp.concatenate(
        [iq2[h // 4][:, (h % 4) * IDX_DIM:(h % 4 + 1) * IDX_DIM] for h in range(IDX_HEADS)],
        axis=0).astype(BF16)
    iw_t = zms_ref[qrows, :].T[IDX_DIM:IDX_DIM + IDX_HEADS, :] * ((IDX_HEADS * IDX_DIM) ** -0.5)
    iw_row = jnp.concatenate([iw_t[h:h + 1, :] for h in range(IDX_HEADS)], axis=1)

    kpos_l = lax.broadcasted_iota(I32, (kb_rows, blk), 0)
    qpos = r0 + lax.broadcasted_iota(I32, (kb_rows, blk), 1)

    tri = jnp.where(lax.broadcasted_iota(I32, (sub_rows, sub_rows), 1)
                    <= lax.broadcasted_iota(I32, (sub_rows, sub_rows), 0), 1.0, 0.0).astype(BF16)
    g = zg_ref[qrows, :]
    gate = g * _sigmoid(g)

    def step(n):
        rows = [slice(kb * kb_rows, (kb + 1) * kb_rows) for kb in range(n)]
        groups = kb_rows // BIT_GROUP_ROWS
        causal = (n - 1) * kb_rows + kpos_l <= qpos

        for kb in range(n):
            s = jnp.maximum(_dot_nt(ik_ref[rows[kb], :], iq_stack), 0.0) * iw_row
            sc = s[:, 0:blk]
            for h in range(1, IDX_HEADS):
                sc = sc + s[:, h * blk:(h + 1) * blk]
            if kb == n - 1:
                sc = jnp.where(causal, sc, -jnp.inf)
            bits = pltpu.bitcast(sc, I32)
            key = bits ^ ((bits >> 31) & 0x7FFFFFFF)
            key_ref[rows[kb], :] = key
            ku = key ^ INT_MIN
            for grp in range(groups):
                words = [ku[(grp * 32 + v) * 8:(grp * 32 + v + 1) * 8, :] for v in range(32)]
                for b, plane in enumerate(_bit_transpose32(words)):
                    plane_ref[kb * groups + grp, b] = plane

        n_groups = n * groups

        def bit_body(i, carry):
            active, k_rem, thr_u = carry
            ones = [active[g] & plane_ref[g, i] for g in range(n_groups)]
            acc = lax.population_count(ones[0])
            for g in range(1, n_groups):
                acc = acc + lax.population_count(ones[g])
            cnt = acc.sum(axis=0, keepdims=True)
            dec = cnt >= k_rem
            k_rem = jnp.where(dec, k_rem, k_rem - cnt)
            thr_u = thr_u | jnp.where(dec, jnp.int32(1) << (31 - i), 0)
            dec8 = jnp.broadcast_to(dec, (8, blk))
            active = tuple(jnp.where(dec8, ones[g], active[g] ^ ones[g]) for g in range(n_groups))
            return active, k_rem, thr_u

        def search():
            init = (tuple(jnp.full((8, blk), -1, I32) for _ in range(n_groups)),
                    jnp.full((1, blk), DSA_TOPK, I32), jnp.zeros((1, blk), I32))
            _, k_rem, thr_u = lax.fori_loop(0, 32, bit_body, init)
            return thr_u ^ INT_MIN, k_rem

        keep_all = lambda: (jnp.full((1, blk), INT_MIN, I32), jnp.zeros((1, blk), I32))
        if (n - 1) * kb_rows >= DSA_TOPK:
            thr, need = search()
        else:
            thr, need = lax.cond((qb + 1) * blk > DSA_TOPK, search, keep_all)
        need = need.astype(F32)

        wide = DSA_HEADS * blk
        m = jnp.full((1, wide), NEG_BIG, F32)
        l = jnp.zeros((1, wide), F32)
        acc = jnp.zeros((HEAD_DIM, wide), F32)
        run = jnp.zeros((1, blk), F32)
        for kb in range(n):
            keys = key_ref[rows[kb], :]
            eq = keys == thr
            eq_b = jnp.where(eq, 1.0, 0.0).astype(BF16)
            local = [_dot(tri, eq_b[sub * sub_rows:(sub + 1) * sub_rows, :]) for sub in range(kb_rows // sub_rows)]
            ranks = []
            for x in local:
                ranks.append(x + run)
                run = run + x[sub_rows - 1:sub_rows, :]
            rank = jnp.concatenate(ranks, axis=0)
            tie_bias = jnp.where(eq, jnp.where(rank <= need, 0.0, NEG_BIG), NEG_BIG)
            bias = jnp.where(keys > thr, 0.0, tie_bias)
            if kb == n - 1:
                bias = jnp.where(causal, bias, NEG_BIG)
            s = _dot_nt(k_ref[rows[kb], :], q_stack)
            s = s + jnp.concatenate([bias] * DSA_HEADS, axis=1)
            m_new = jnp.maximum(m, jnp.max(s, axis=0, keepdims=True))
            p = jnp.exp2(s - m_new)
            alpha = jnp.exp2(m - m_new)
            l = alpha * l + jnp.sum(p, axis=0, keepdims=True)
            acc = alpha * acc + _dot(vt_ref[kb], p.astype(BF16))
            m = m_new

        o_t = acc / l
        o = jnp.concatenate(
            [jnp.concatenate([o_t[:, (2 * i) * blk:(2 * i + 1) * blk], o_t[:, (2 * i + 1) * blk:(2 * i + 2) * blk]],
                             axis=0).T for i in range(DSA_HEADS // 2)], axis=1)
        o_ref[qrows, :] = (o * gate).astype(o_ref.dtype)

    for n in range(1, seq // kb_rows + 1):
        pl.when(n_kb == n)(functools.partial(step, n))


def _dsa(z2d, batch, seq, c64, s64, c32, s32):
    blk = DSA_QUERY_BLOCK
    qspec = lambda u2: pl.BlockSpec((seq, 2 * LANES), lambda b, u2=u2: (b, u2))
    seqspec = lambda u: pl.BlockSpec((seq, LANES), lambda b, u=u: (b, u))
    tab = pl.BlockSpec((seq, LANES), lambda b: (0, 0))
    return pl.pallas_call(
        _dsa_kernel,
        grid=(batch,),
        in_specs=[qspec(U_DS_Q // 2), qspec(U_DS_G // 2), qspec(U_DS_IQ // 2),
                  seqspec(U_DS_KV), seqspec(U_DS_MISC), tab, tab, tab, tab],
        out_specs=pl.BlockSpec((seq, DSA_WIDTH), lambda b: (b, 0)),
        out_shape=jax.ShapeDtypeStruct((batch * seq, DSA_WIDTH), BF16),
        scratch_shapes=[
            pltpu.VMEM((seq, HEAD_DIM), BF16),
            pltpu.VMEM((seq // DSA_KEY_BLOCK, HEAD_DIM, DSA_KEY_BLOCK), BF16),
            pltpu.VMEM((seq, IDX_DIM), BF16),
            pltpu.VMEM((seq, blk), I32),
            pltpu.VMEM((seq // BIT_GROUP_ROWS, 32, 8, blk), I32),
        ],
        compiler_params=pltpu.CompilerParams(
            dimension_semantics=("parallel",), vmem_limit_bytes=VMEM_LIMIT),
        name="dsa",
    )(z2d, z2d, z2d, z2d, z2d, c64, s64, c32, s32)


def _dil_group(gi, rate, q_ref, k_ref, v_ref, g_ref, c64_ref, s64_ref, qr_ref, kr_ref, o_s, lse_s):
    seq = q_ref.shape[0]
    blk = ATT_BLOCK
    rope_rows = 4 * blk
    nrow = seq // rope_rows

    def rope_body(t, carry):
        rows = pl.ds(pl.multiple_of(t * rope_rows, rope_rows), rope_rows)
        ct = c64_ref[rows, :]
        st = s64_ref[rows, :]
        q = q_ref[rows, :]
        k = k_ref[rows, :]
        qr_ref[rows, :] = (q * ct + pltpu.roll(q, HEAD_DIM, 1) * st) * QK_SCALE_LOG2
        kr_ref[rows, :] = k * ct + pltpu.roll(k, HEAD_DIM, 1) * st
        return carry

    lax.fori_loop(0, nrow, rope_body, 0)

    nblk = seq // (blk * rate)
    qi = lax.broadcasted_iota(I32, (blk, 2 * blk), 0)
    kj = lax.broadcasted_iota(I32, (blk, 2 * blk), 1)
    band = (kj >= qi) & (kj <= qi + DIL_STEPS)

    def strided(ref, start):
        if rate == 1:
            return ref[pl.ds(start, blk), :]
        return ref[pl.ds(start, blk, stride=rate), :]

    def store(ref, start, val):
        if rate == 1:
            ref[gi, pl.ds(start, blk), :] = val
        else:
            ref[gi, pl.ds(start, blk, stride=rate), :] = val

    first_q = lax.broadcasted_iota(I32, (blk, LANES), 1) < HEAD_DIM
    first_kv = lax.broadcasted_iota(I32, (2 * blk, LANES), 1) < HEAD_DIM
    lane64 = lax.broadcasted_iota(I32, (blk, LANES), 1) % HEAD_DIM
    rest = (HEAD_DIM - 2 * ROT_HALF) // 2
    first_qk = (lane64 < ROT_HALF) | ((lane64 >= 2 * ROT_HALF) & (lane64 < 2 * ROT_HALF + rest))
    unroll = DIL_UNROLL if rate == max(DIL_RATES) else 2 * DIL_UNROLL

    def body(it, carry):
        starts, qs, ks, vs, masks = [], [], [], [], []
        for u in range(unroll):
            cidx = it * unroll + u
            rho = cidx // nblk
            j = cidx % nblk
            start = j * (blk * rate) + rho
            pstart = jnp.maximum(j - 1, 0) * (blk * rate) + rho
            starts.append(start)
            qs.append(strided(qr_ref, start))
            ks.append(jnp.concatenate([strided(kr_ref, pstart), strided(kr_ref, start)], axis=0).astype(BF16))
            vs.append(jnp.concatenate([strided(v_ref, pstart), strided(v_ref, start)], axis=0))
            masks.append(band & ((kj >= blk) | (j > 0)))
        prob = [(u, h) for u in range(unroll) for h in range(2)]
        qh = [jnp.where(first_qk == (h == 0), qs[u], 0.0).astype(BF16) for u, h in prob]
        vh = [jnp.where(first_kv == (h == 0), vs[u], 1.0).astype(BF16) for u, h in prob]
        s = [jnp.where(masks[u], _dot_nt(qh[i], ks[u]), -jnp.inf) for i, (u, h) in enumerate(prob)]
        m = [jnp.max(x, axis=-1, keepdims=True) for x in s]
        r = [_dot(jnp.exp2(s[i] - m[i]).astype(BF16), vh[i]) for i in range(len(prob))]
        for u in range(unroll):
            r0, r1 = r[2 * u], r[2 * u + 1]
            o = jnp.where(first_q, r0, r1)
            l = pltpu.roll(jnp.where(first_q, r1, r0), HEAD_DIM, 1)
            lse = jnp.where(first_q, m[2 * u], m[2 * u + 1]) * math.log(2.0) + jnp.log(l)
            gate = strided(g_ref, starts[u])
            store(o_s, starts[u], o / l * (gate * _sigmoid(gate)))
            store(lse_s, starts[u], lse)
        return carry

    lax.fori_loop(0, rate * nblk // unroll, body, 0)


def _dil_kernel(q_ref, k_ref, v_ref, g_ref, c64_ref, s64_ref, o_ref, qr_ref, kr_ref, o_s, lse_s):
    grp = pl.program_id(1)
    for gi, rate in enumerate(DIL_RATES):
        pl.when(grp == gi)(functools.partial(
            _dil_group, gi, rate, q_ref, k_ref, v_ref, g_ref, c64_ref, s64_ref, qr_ref, kr_ref, o_s, lse_s))

    @pl.when(grp == len(DIL_RATES) - 1)
    def _():
        blk = ATT_BLOCK

        def body(t, carry):
            rows = pl.ds(pl.multiple_of(t * blk, blk), blk)
            ls = [lse_s[i, rows, :] for i in range(3)]
            top = jnp.maximum(jnp.maximum(ls[0], ls[1]), ls[2])
            es = [jnp.exp(x - top) for x in ls]
            inv = 1.0 / (es[0] + es[1] + es[2])
            o_ref[rows, :] = jnp.concatenate(
                [o_s[i, rows, :] * (es[i] * inv) for i in range(3)], axis=1).astype(o_ref.dtype)
            return carry

        lax.fori_loop(0, q_ref.shape[0] // blk, body, 0)


def _dilated(z2d, batch, seq, c64, s64):
    ng = len(DIL_RATES)
    zspec = lambda base: pl.BlockSpec((seq, LANES), lambda b, g, base=base: (b, base + g))
    tab = pl.BlockSpec((seq, LANES), lambda b, g: (0, 0))
    return pl.pallas_call(
        _dil_kernel,
        grid=(batch, ng),
        in_specs=[zspec(U_DL), zspec(U_DL + 3), zspec(U_DL + 6), zspec(U_DL + 9), tab, tab],
        out_specs=pl.BlockSpec((seq, DIL_WIDTH), lambda b, g: (b, 0)),
        out_shape=jax.ShapeDtypeStruct((batch * seq, DIL_WIDTH), BF16),
        scratch_shapes=[
            pltpu.VMEM((seq, LANES), F32),
            pltpu.VMEM((seq, LANES), F32),
            pltpu.VMEM((ng, seq, LANES), F32),
            pltpu.VMEM((ng, seq, LANES), F32),
        ],
        compiler_params=pltpu.CompilerParams(
            dimension_semantics=("parallel", "arbitrary"), vmem_limit_bytes=VMEM_LIMIT),
        name="dilated",
    )(z2d, z2d, z2d, z2d, c64, s64)


def _rope_tables(seq, head, half):
    inv = ROPE_THETA ** (-jnp.arange(half, dtype=F32) / half)
    ang = jnp.arange(seq, dtype=F32)[:, None] * inv[None, :]
    cos, sin = jnp.cos(ang), jnp.sin(ang)
    lane = jnp.arange(LANES) % head
    idx = lane % half
    cos_t = jnp.where(lane[None, :] < 2 * half, cos[:, idx], 1.0)
    sin_t = jnp.where(lane[None, :] < half, -sin[:, idx], jnp.where(lane[None, :] < 2 * half, sin[:, idx], 0.0))
    return cos_t, sin_t


def _dil_pair_lanes():
    half, rest = ROT_HALF, (HEAD_DIM - 2 * ROT_HALF) // 2
    src = []
    for part in range(2):
        for head in range(2):
            src += [head * HEAD_DIM + part * half + i for i in range(half)]
        for head in range(2):
            src += [head * HEAD_DIM + 2 * half + part * rest + i for i in range(rest)]
    return src


def _rope_tables_paired(seq):
    half = ROT_HALF
    inv = ROPE_THETA ** (-jnp.arange(half, dtype=F32) / half)
    ang = jnp.arange(seq, dtype=F32)[:, None] * inv[None, :]
    cos, sin = jnp.cos(ang), jnp.sin(ang)
    lane = jnp.arange(LANES)[None, :]
    idx = jnp.arange(LANES) % half
    rotated = lane % HEAD_DIM < 2 * half
    cos_t = jnp.where(rotated, cos[:, idx], 1.0)
    sin_t = jnp.where(rotated, jnp.where(lane < HEAD_DIM, -sin[:, idx], sin[:, idx]), 0.0)
    return cos_t, sin_t


def _in_col_sources():
    rw = RWKV_WIDTH
    cols = lambda o, wdt: list(range(o, o + wdt))
    o = 0
    r, k, v, g = (cols(o + i * rw, rw) for i in range(4))
    o = 4 * rw
    wd, ad = cols(o, LORA_RANK), cols(o + LORA_RANK, LORA_RANK)
    o += 2 * LORA_RANK
    parts = []
    for wdt in (DSA_WIDTH, HEAD_DIM, HEAD_DIM, IDX_HEADS * IDX_DIM, IDX_DIM, IDX_HEADS, DSA_WIDTH):
        parts.append(cols(o, wdt))
        o += wdt
    dq, dk, dv, diq, dik, diw, dg = parts
    cq, ck, cv, cg = (cols(o + i * DIL_WIDTH, DIL_WIDTH) for i in range(4))
    o += 4 * DIL_WIDTH
    pair = [t * LANES + s for t in range(DIL_WIDTH // LANES) for s in _dil_pair_lanes()]
    cq, ck = [cq[i] for i in pair], [ck[i] for i in pair]
    pad = [o] * (LANES - IDX_DIM - IDX_HEADS)
    src = r + k + v + g + cq + ck + cv + cg + dq + dg + diq + dk + dv + dik + diw + pad + wd + ad
    assert len(src) == Z_COLS
    return src


def _permute_in_cols(w):
    return jnp.take(w, jnp.asarray(_in_col_sources(), I32), axis=-1, mode="fill", fill_value=0)


def kernel(x, p, norm_g, w_in, tshift_mu, rwkv_w0, rwkv_w_up, rwkv_a0, rwkv_a_up, rwkv_k_k, rwkv_k_a, rwkv_r_k,
           rwkv_ln_g, rwkv_ln_b, w_out, ple_norm_g, ple_w_gate, ple_w_proj, final_norm_g):
    batch, seq, d_model = x.shape
    depth = w_in.shape[0]
    n = batch * seq
    assert seq % (ATT_BLOCK * max(DIL_RATES)) == 0 and seq >= 4 * DSA_TOPK

    c64, s64 = _rope_tables(seq, HEAD_DIM, ROT_HALF)
    c32, s32 = _rope_tables(seq, IDX_DIM, IDX_ROT_HALF)
    c64p, s64p = _rope_tables_paired(seq)
    w_in_k = _permute_in_cols(w_in.astype(BF16))
    rw = RWKV_WIDTH
    row = lambda a: a.reshape(1, -1)

    x2d = x.reshape(n, d_model)
    p3d = p.reshape(depth, n, p.shape[-1])
    for i in range(depth):
        z = _in_proj(x2d, row(norm_g[i]), w_in_k, i)
        mu = tshift_mu[i]
        y_a = _rwkv(z.reshape(batch, seq, Z_COLS), mu[:4 * rw].reshape(4, rw), row(mu[4 * rw:]), row(rwkv_w0[i]), rwkv_w_up[i],
                    row(rwkv_a0[i]), rwkv_a_up[i], row(rwkv_k_k[i]), row(rwkv_k_a[i]), row(rwkv_r_k[i]),
                    row(rwkv_ln_g[i]), row(rwkv_ln_b[i])).reshape(n, rw)
        y_b = _dsa(z, batch, seq, c64, s64, c32, s32)
        y_c = _dilated(z, batch, seq, c64p, s64p)
        wo = w_out[i].astype(BF16)
        x2d = _out_proj(x2d, y_a, y_b, y_c, p3d, i, wo[:rw], wo[rw:rw + DSA_WIDTH],
                        wo[rw + DSA_WIDTH:], row(ple_norm_g[i]), ple_w_gate[i].astype(BF16),
                        ple_w_proj[i].astype(BF16), row(final_norm_g), final=(i == depth - 1))
    return x2d.reshape(batch, seq, d_model)
```

```python
import functools
import math

import jax
import jax.numpy as jnp
from jax import lax
from jax.experimental import pallas as pl
from jax.experimental.pallas import tpu as pltpu

F32 = jnp.float32
BF16 = jnp.bfloat16
I32 = jnp.int32

HEAD_DIM = 64
ROPE_THETA = 500000.0
ROT_HALF = HEAD_DIM // 8
NORM_EPS = 1e-6
RWKV_HEADS = 6
RWKV_WIDTH = RWKV_HEADS * HEAD_DIM
LORA_RANK = 64
GN_EPS = 64e-5
DECAY_SCALE = math.exp(-0.5)
DSA_HEADS = 4
DSA_WIDTH = DSA_HEADS * HEAD_DIM
IDX_HEADS = 8
IDX_DIM = 32
IDX_ROT_HALF = IDX_DIM // 8
DSA_TOPK = 256
DIL_RATES = (1, 4, 16)
DIL_STEPS = 128
DIL_UNROLL = 4
DIL_WIDTH = 384
ATT_BLOCK = 128
DSA_QUERY_BLOCK = 256
DSA_KEY_BLOCK = 256

LANES = 128
MXU_WIDTH = 256
OUT_SUB_TILES = 4
VMEM_LIMIT = 48 * 1024 * 1024

U_RW = 0
U_DL = 12
U_DS_Q = 24
U_DS_G = 26
U_DS_IQ = 28
U_DS_KV = 30
U_DS_MISC = 31
U_RW_WA = 32
Z_UNITS = 33
Z_COLS = Z_UNITS * LANES

RWKV_CHUNK = 128
RWKV_BATCH_ROWS = 4
INV_BASE = 16
QK_SCALE_LOG2 = HEAD_DIM ** -0.5 * math.log2(math.e)
NEG_BIG = -1e30
BIT_GROUP_ROWS = 256
INT_MIN = -2 ** 31


def _dot(a, b):
    return jnp.dot(a, b, preferred_element_type=F32)


def _dot_nt(a, b):
    return lax.dot_general(a, b, (((1,), (1,)), ((), ())), preferred_element_type=F32)


def _split(x):
    hi = x.astype(BF16)
    lo = (x - hi.astype(F32)).astype(BF16)
    return hi, lo


def _head_sum(x, head_ones):
    return _dot(x.astype(BF16), head_ones)


def _dot_x3(x, w):
    xh, xl = _split(x)
    wh, wl = _split(w)
    return _dot(xh, wh) + (_dot(xh, wl) + _dot(xl, wh))


def _sigmoid(x):
    return 1.0 / (1.0 + jnp.exp(-x))


def _rope(x, cos_t, sin_t, half, head):
    lane = lax.broadcasted_iota(I32, x.shape, 1) % head
    partner = jnp.where(lane < half, pltpu.roll(x, LANES - half, 1), pltpu.roll(x, half, 1))
    return x * cos_t + partner * sin_t


def _in_proj_kernel(x_ref, g_ref, w_ref, z_ref, *, tn):
    x = x_ref[...]
    ms = jnp.mean(x * x, axis=-1, keepdims=True)
    h = (x * lax.rsqrt(ms + NORM_EPS) * g_ref[...]).astype(BF16)
    for c in range(0, Z_COLS, tn):
        wdt = min(tn, Z_COLS - c)
        z_ref[:, c:c + wdt] = _dot(h, w_ref[:, c:c + wdt])


def _in_proj(x2d, g, w_all, layer, tm=512, tn=MXU_WIDTH):
    n, d = x2d.shape
    return pl.pallas_call(
        functools.partial(_in_proj_kernel, tn=tn),
        grid=(n // tm,),
        in_specs=[
            pl.BlockSpec((tm, d), lambda i: (i, 0)),
            pl.BlockSpec((1, d), lambda i: (0, 0)),
            pl.BlockSpec((None, d, Z_COLS), lambda i: (layer, 0, 0)),
        ],
        out_specs=pl.BlockSpec((tm, Z_COLS), lambda i: (i, 0)),
        out_shape=jax.ShapeDtypeStruct((n, Z_COLS), F32),
        compiler_params=pltpu.CompilerParams(
            dimension_semantics=("parallel",), vmem_limit_bytes=VMEM_LIMIT),
        name="in_proj",
    )(x2d, g, w_all)


def _out_proj_kernel(x_ref, ya_ref, yb_ref, yc_ref, p_ref, wa_ref, wb_ref, wc_ref, ng_ref, wg_ref,
                     wp_ref, fg_ref, o_ref, *, final):
    sub = x_ref.shape[0] // OUT_SUB_TILES
    rows = [slice(i * sub, (i + 1) * sub) for i in range(OUT_SUB_TILES)]
    acc = [x_ref[r, :] + _dot(ya_ref[r, :], wa_ref[...]) for r in rows]
    acc = [a + _dot(yb_ref[r, :], wb_ref[...]) for a, r in zip(acc, rows)]
    acc = [a + _dot(yc_ref[r, :], wc_ref[...]) for a, r in zip(acc, rows)]
    hn = [(a * lax.rsqrt(jnp.mean(a * a, axis=-1, keepdims=True) + NORM_EPS) * ng_ref[...]).astype(BF16)
          for a in acc]
    gate = [_sigmoid(_dot(h, wg_ref[...])) for h in hn]
    out = [a + g * _dot(p_ref[r, :].astype(BF16), wp_ref[...]) for a, g, r in zip(acc, gate, rows)]
    if final:
        out = [o * lax.rsqrt(jnp.mean(o * o, axis=-1, keepdims=True) + NORM_EPS) * fg_ref[...] for o in out]
    for o, r in zip(out, rows):
        o_ref[r, :] = o


def _out_proj(x2d, ya, yb, yc, p3d, layer, wa, wb, wc, ng, wg, wp, fg, final, tm=1024):
    n, d = x2d.shape
    row = lambda w: pl.BlockSpec((tm, w), lambda i: (i, 0))
    full = lambda a: pl.BlockSpec(a.shape, lambda i: (0, 0))
    return pl.pallas_call(
        functools.partial(_out_proj_kernel, final=final),
        grid=(n // tm,),
        in_specs=[row(d), row(ya.shape[1]), row(yb.shape[1]), row(yc.shape[1]),
                  pl.BlockSpec((None, tm, p3d.shape[2]), lambda i: (layer, i, 0)),
                  full(wa), full(wb), full(wc), full(ng), full(wg), full(wp), full(fg)],
        out_specs=row(d),
        out_shape=jax.ShapeDtypeStruct((n, d), F32),
        compiler_params=pltpu.CompilerParams(
            dimension_semantics=("parallel",), vmem_limit_bytes=VMEM_LIMIT),
        name="out_proj",
    )(x2d, ya, yb, yc, p3d, wa, wb, wc, ng, wg, wp, fg)


def _unit_lower_inverse(n_list, row, col):
    c = n_list[0].shape[0]
    eye = jnp.where(row == col, 1.0, 0.0).astype(F32)
    same = (row // INV_BASE) == (col // INV_BASE)
    pw = [jnp.where(same, n, 0.0) for n in n_list]
    t = [eye - x for x in pw]
    span = 2
    while span < INV_BASE:
        pwb = [x.astype(BF16) for x in pw]
        pw = [_dot(x, x) for x in pwb]
        t = [x + _dot(x.astype(BF16), y.astype(BF16)) for x, y in zip(t, pw)]
        span *= 2
    size = INV_BASE
    while size < c:
        lower_left = ((row // (2 * size)) == (col // (2 * size))) & ((row // size) != (col // size))
        tb = [x.astype(BF16) for x in t]
        et = [_dot(jnp.where(lower_left, n, 0.0).astype(BF16), x).astype(BF16) for n, x in zip(n_list, tb)]
        t = [x - _dot(xb, y) for x, xb, y in zip(t, tb, et)]
        size *= 2
    return t


def _rwkv_kernel(zr_ref, zk_ref, zv_ref, zg_ref, zwa_ref, mu_ref, muwa_ref, w0_ref, wup_ref, a0_ref,
                 aup_ref, kk_ref, ka_ref, rk_ref, lng_ref, lnb_ref, o_ref,
                 state_ref, prev_ref, prevwa_ref, y_ref):
    nb, c = zr_ref.shape[0], zr_ref.shape[1]
    width = RWKV_WIDTH

    @pl.when(pl.program_id(1) == 0)
    def _():
        state_ref[...] = jnp.zeros_like(state_ref)
        prev_ref[...] = jnp.zeros_like(prev_ref)
        prevwa_ref[...] = jnp.zeros_like(prevwa_ref)

    first_row = lax.broadcasted_iota(I32, (c, width), 0) == 0
    first_row_wa = lax.broadcasted_iota(I32, (c, LANES), 0) == 0
    hrow = lax.broadcasted_iota(I32, (width, width), 0) // HEAD_DIM
    hcol = lax.broadcasted_iota(I32, (width, width), 1) // HEAD_DIM
    head_ones = jnp.where(hrow == hcol, 1.0, 0.0).astype(BF16)
    row = lax.broadcasted_iota(I32, (c, c), 0)
    col = lax.broadcasted_iota(I32, (c, c), 1)
    tri = jnp.where(col <= row, 1.0, 0.0).astype(BF16)
    strict = col < row
    incl = col <= row
    hs = [slice(h * HEAD_DIM, (h + 1) * HEAD_DIM) for h in range(RWKV_HEADS)]

    def prepare(bi):
        def shifted(z_ref, idx):
            z = z_ref[bi]
            prev = jnp.where(first_row, prev_ref[bi, idx, 0:1, :], pltpu.roll(z, 1, 0))
            prev_ref[bi, idx, 0:1, :] = z[c - 1:c, :]
            return z + (prev - z) * mu_ref[idx:idx + 1, :]

        r = shifted(zr_ref, 0)
        k = shifted(zk_ref, 1)
        v = shifted(zv_ref, 2)
        g = shifted(zg_ref, 3)
        zwa = zwa_ref[bi]
        prev_wa = jnp.where(first_row_wa, prevwa_ref[bi, 0:1, :], pltpu.roll(zwa, 1, 0))
        prevwa_ref[bi, 0:1, :] = zwa[c - 1:c, :]
        wa = zwa + (prev_wa - zwa) * muwa_ref[...]
        wd = wa[:, :LORA_RANK]
        ad = wa[:, LORA_RANK:]

        lw = -DECAY_SCALE * _sigmoid(w0_ref[...] + _dot_x3(jnp.tanh(wd), wup_ref[...]))
        a = _sigmoid(a0_ref[...] + _dot_x3(ad, aup_ref[...]))
        kk = k * kk_ref[...]
        norm = jnp.sqrt(_head_sum(kk * kk, head_ones))
        kk = kk / jnp.maximum(norm, 1e-12)
        kmod = k * (1.0 + (a - 1.0) * ka_ref[...])
        b = a * kk

        lw_hi, lw_lo = _split(lw)
        cl = _dot(tri, lw_hi) + _dot(tri, lw_lo)
        c_mid = cl[c // 2 - 1:c // 2, :]
        c_end = cl[c - 1:c, :]
        w = jnp.exp(lw)
        e_in1 = jnp.exp(cl - lw)
        e_in2 = e_in1 * w
        e_k = jnp.exp(c_mid - cl)
        e_mid_inv = jnp.exp(-c_mid)
        e_end = e_k * jnp.exp(c_end - c_mid)
        ops = dict(
            q1=kk * e_in1 * e_mid_inv, q2=r * e_in2 * e_mid_inv, k1=b * e_k, k2=kmod * e_k,
            kkd=kk * e_in1, rd=r * e_in2, bd=b * e_end, kd=kmod * e_end, vb=v)
        ops = {name: [x.astype(BF16)[:, s] for s in hs] for name, x in ops.items()}
        return dict(ops, r=r, kmod=kmod, v=v, g=g, d_end=jnp.exp(c_end))

    prep = [prepare(bi) for bi in range(nb)]

    probs = [(bi, h) for bi in range(nb) for h in range(RWKV_HEADS)]
    op = lambda name: [prep[bi][name][h] for bi, h in probs]
    q1, q2, k1, k2, kkd, rd, bd, kd, vh = map(op, ("q1", "q2", "k1", "k2", "kkd", "rd", "bd", "kd", "vb"))
    n = range(len(probs))
    qk = [_dot_nt(jnp.concatenate([q1[i], q2[i]], axis=0), jnp.concatenate([k1[i], k2[i]], axis=0)) for i in n]
    a_ab = [jnp.where(strict, x[:c, :c], 0.0) for x in qk]
    t_inv = _unit_lower_inverse(a_ab, row, col)
    a_rb = [jnp.where(incl, x[c:, :c], 0.0).astype(BF16) for x in qk]
    a_v = [jnp.concatenate([jnp.where(strict, x[:c, c:], 0.0), jnp.where(incl, x[c:, c:], 0.0)],
                           axis=0).astype(BF16) for x in qk]
    s_old = [state_ref[bi, h] for bi, h in probs]
    s_in = [s.astype(BF16) for s in s_old]
    from_v = [_dot(a_v[i], vh[i]) for i in n]
    from_s = [_dot_nt(jnp.concatenate([kkd[i], rd[i]], axis=0), s_in[i]) for i in n]
    rhs = [from_s[i][:c] + from_v[i][:c] for i in n]
    u = [-_dot(t_inv[i].astype(BF16), rhs[i].astype(BF16)) for i in n]
    uv_t = [jnp.concatenate([u[i], prep[bi]["v"][:, hs[h]]], axis=1).T.astype(BF16)
            for i, (bi, h) in enumerate(probs)]
    for i, (bi, h) in enumerate(probs):
        state_ref[bi, h] = (s_old[i] * prep[bi]["d_end"][:, hs[h]]
                            + _dot(uv_t[i][:HEAD_DIM, :], bd[i]) + _dot(uv_t[i][HEAD_DIM:, :], kd[i]))
    for i, (bi, h) in enumerate(probs):
        y_ref[bi, :, hs[h]] = from_s[i][c:] + _dot(a_rb[i], u[i].astype(BF16)) + from_v[i][c:]

    inv_n = 1.0 / HEAD_DIM
    for bi in range(nb):
        pb = prep[bi]
        y = y_ref[bi]
        mean = _head_sum(y, head_ones) * inv_n
        dlt = y - mean
        var = _head_sum(dlt * dlt, head_ones) * inv_n
        yn = dlt * lax.rsqrt(var + GN_EPS) * lng_ref[...] + lnb_ref[...]
        bonus = _head_sum(pb["r"] * pb["kmod"] * rk_ref[...], head_ones) * pb["v"]
        g = pb["g"]
        o_ref[bi] = ((yn + bonus) * (g * _sigmoid(g))).astype(o_ref.dtype)


def _rwkv(z3d, mu4, mu_wa, w0, w_up, a0, a_up, k_k, k_a, r_k, ln_g, ln_b):
    batch, seq, _ = z3d.shape
    c = RWKV_CHUNK
    nb = RWKV_BATCH_ROWS if batch % RWKV_BATCH_ROWS == 0 else 1
    zspec = lambda u: pl.BlockSpec((nb, c, RWKV_WIDTH), lambda b, i, u=u: (b, i, u))
    full = lambda a: pl.BlockSpec(a.shape, lambda b, i: (0,) * a.ndim)
    params = (mu4, mu_wa, w0, w_up, a0, a_up, k_k, k_a, r_k, ln_g, ln_b)
    return pl.pallas_call(
        _rwkv_kernel,
        grid=(batch // nb, seq // c),
        in_specs=[zspec(0), zspec(1), zspec(2), zspec(3),
                  pl.BlockSpec((nb, c, LANES), lambda b, i: (b, i, U_RW_WA))]
                 + [full(a) for a in params],
        out_specs=pl.BlockSpec((nb, c, RWKV_WIDTH), lambda b, i: (b, i, 0)),
        out_shape=jax.ShapeDtypeStruct((batch, seq, RWKV_WIDTH), BF16),
        scratch_shapes=[
            pltpu.VMEM((nb, RWKV_HEADS, HEAD_DIM, HEAD_DIM), F32),
            pltpu.VMEM((nb, 4, 8, RWKV_WIDTH), F32),
            pltpu.VMEM((nb, 8, LANES), F32),
            pltpu.VMEM((nb, c, RWKV_WIDTH), F32),
        ],
        compiler_params=pltpu.CompilerParams(
            dimension_semantics=("parallel", "arbitrary"), vmem_limit_bytes=VMEM_LIMIT),
        name="rwkv7",
    )(z3d, z3d, z3d, z3d, z3d, *params)


def _bit_transpose32(words):
    a = list(words)
    j, mask = 16, 0x0000FFFF
    while j:
        for k in range(32):
            if k & j == 0:
                t = (a[k] ^ lax.shift_right_logical(a[k + j], jnp.int32(j))) & jnp.int32(mask)
                a[k] = a[k] ^ t
                a[k + j] = a[k + j] ^ (t << j)
        j >>= 1
        mask ^= mask << j
    return a


def _dsa_kernel(*refs):
    seq = refs[3].shape[0]
    lax.fori_loop(0, seq // DSA_QUERY_BLOCK, lambda qb, carry: (_dsa_query_block(qb, *refs), carry)[1], 0)


def _dsa_query_block(qb, zq_ref, zg_ref, ziq_ref, zkv_ref, zms_ref, c64_ref, s64_ref, c32_ref, s32_ref, o_ref,
                     k_ref, vt_ref, ik_ref, key_ref, plane_ref):
    seq = zkv_ref.shape[0]
    blk = DSA_QUERY_BLOCK
    sub_rows = ATT_BLOCK
    kb_rows = DSA_KEY_BLOCK
    n_kb = (qb * blk) // kb_rows + 1

    @pl.when(qb == 0)
    def _():
        lane = lax.broadcasted_iota(I32, (kb_rows, LANES), 1)

        def body(t, carry):
            rows = pl.ds(pl.multiple_of(t * kb_rows, kb_rows), kb_rows)
            kv = zkv_ref[rows, :]
            kvr = jnp.where(lane < HEAD_DIM, _rope(kv, c64_ref[rows, :], s64_ref[rows, :], ROT_HALF, HEAD_DIM), kv)
            k_ref[rows, :] = kvr[:, :HEAD_DIM].astype(BF16)
            vt_ref[t] = kvr.T[HEAD_DIM:, :].astype(BF16)
            ms = zms_ref[rows, :]
            ikr = _rope(ms, c32_ref[rows, :], s32_ref[rows, :], IDX_ROT_HALF, IDX_DIM)
            ik_ref[rows, :] = ikr[:, :IDX_DIM].astype(BF16)
            return carry

        lax.fori_loop(0, seq // kb_rows, body, 0)

    r0 = pl.multiple_of(qb * blk, blk)
    qrows = pl.ds(r0, blk)
    c64 = c64_ref[qrows, :]
    s64 = s64_ref[qrows, :]
    c32 = c32_ref[qrows, :]
    s32 = s32_ref[qrows, :]
    q2 = [_rope(zq_ref[qrows, i * LANES:(i + 1) * LANES], c64, s64, ROT_HALF, HEAD_DIM) * QK_SCALE_LOG2
          for i in range(2)]
    q_stack = jnp.concatenate(
        [q2[h // 2][:, (h % 2) * HEAD_DIM:(h % 2 + 1) * HEAD_DIM] for h in range(DSA_HEADS)],
        axis=0).astype(BF16)
    iq2 = [_rope(ziq_ref[qrows, i * LANES:(i + 1) * LANES], c32, s32, IDX_ROT_HALF, IDX_DIM) for i in range(2)]
    iq_stack = jnp.concatenate(
        [iq2[h // 4][:, (h % 4) * IDX_DIM:(h % 4 + 1) * IDX_DIM] for h in range(IDX_HEADS)],
        axis=0).astype(BF16)
    iw_t = zms_ref[qrows, :].T[IDX_DIM:IDX_DIM + IDX_HEADS, :] * ((IDX_HEADS * IDX_DIM) ** -0.5)
    iw_row = jnp.concatenate([iw_t[h:h + 1, :] for h in range(IDX_HEADS)], axis=1)

    kpos_l = lax.broadcasted_iota(I32, (kb_rows, blk), 0)
    qpos = r0 + lax.broadcasted_iota(I32, (kb_rows, blk), 1)

    tri = jnp.where(lax.broadcasted_iota(I32, (sub_rows, sub_rows), 1)
                    <= lax.broadcasted_iota(I32, (sub_rows, sub_rows), 0), 1.0, 0.0).astype(BF16)
    g = zg_ref[qrows, :]
    gate = g * _sigmoid(g)

    def step(n):
        rows = [slice(kb * kb_rows, (kb + 1) * kb_rows) for kb in range(n)]
        groups = kb_rows // BIT_GROUP_ROWS
        causal = (n - 1) * kb_rows + kpos_l <= qpos

        for kb in range(n):
            s = jnp.maximum(_dot_nt(ik_ref[rows[kb], :], iq_stack), 0.0) * iw_row
            sc = s[:, 0:blk]
            for h in range(1, IDX_HEADS):
                sc = sc + s[:, h * blk:(h + 1) * blk]
            if kb == n - 1:
                sc = jnp.where(causal, sc, -jnp.inf)
            bits = pltpu.bitcast(sc, I32)
            key = bits ^ ((bits >> 31) & 0x7FFFFFFF)
            key_ref[rows[kb], :] = key
            ku = key ^ INT_MIN
            for grp in range(groups):
                words = [ku[(grp * 32 + v) * 8:(grp * 32 + v + 1) * 8, :] for v in range(32)]
                for b, plane in enumerate(_bit_transpose32(words)):
                    plane_ref[kb * groups + grp, b] = plane

        n_groups = n * groups

        def bit_body(i, carry):
            active, k_rem, thr_u = carry
            ones = [active[g] & plane_ref[g, i] for g in range(n_groups)]
            acc = lax.population_count(ones[0])
            for g in range(1, n_groups):
                acc = acc + lax.population_count(ones[g])
            cnt = acc.sum(axis=0, keepdims=True)
            dec = cnt >= k_rem
            k_rem = jnp.where(dec, k_rem, k_rem - cnt)
            thr_u = thr_u | jnp.where(dec, jnp.int32(1) << (31 - i), 0)
            dec8 = jnp.broadcast_to(dec, (8, blk))
            active = tuple(jnp.where(dec8, ones[g], active[g] ^ ones[g]) for g in range(n_groups))
            return active, k_rem, thr_u

        def search():
            init = (tuple(jnp.full((8, blk), -1, I32) for _ in range(n_groups)),
                    jnp.full((1, blk), DSA_TOPK, I32), jnp.zeros((1, blk), I32))
            _, k_rem, thr_u = lax.fori_loop(0, 32, bit_body, init)
            return thr_u ^ INT_MIN, k_rem

        keep_all = lambda: (jnp.full((1, blk), INT_MIN, I32), jnp.zeros((1, blk), I32))
        if (n - 1) * kb_rows >= DSA_TOPK:
            thr, need = search()
        else:
            thr, need = lax.cond((qb + 1) * blk > DSA_TOPK, search, keep_all)
        need = need.astype(F32)

        wide = DSA_HEADS * blk
        m = jnp.full((1, wide), NEG_BIG, F32)
        l = jnp.zeros((1, wide), F32)
        acc = jnp.zeros((HEAD_DIM, wide), F32)
        run = jnp.zeros((1, blk), F32)
        for kb in range(n):
            keys = key_ref[rows[kb], :]
            eq = keys == thr
            eq_b = jnp.where(eq, 1.0, 0.0).astype(BF16)
            local = [_dot(tri, eq_b[sub * sub_rows:(sub + 1) * sub_rows, :]) for sub in range(kb_rows // sub_rows)]
            ranks = []
            for x in local:
                ranks.append(x + run)
                run = run + x[sub_rows - 1:sub_rows, :]
            rank = jnp.concatenate(ranks, axis=0)
            tie_bias = jnp.where(eq, jnp.where(rank <= need, 0.0, NEG_BIG), NEG_BIG)
            bias = jnp.where(keys > thr, 0.0, tie_bias)
            if kb == n - 1:
                bias = jnp.where(causal, bias, NEG_BIG)
            s = _dot_nt(k_ref[rows[kb], :], q_stack)
            s = s + jnp.concatenate([bias] * DSA_HEADS, axis=1)
            m_new = jnp.maximum(m, jnp.max(s, axis=0, keepdims=True))
            p = jnp.exp2(s - m_new)
            alpha = jnp.exp2(m - m_new)
            l = alpha * l + jnp.sum(p, axis=0, keepdims=True)
            acc = alpha * acc + _dot(vt_ref[kb], p.astype(BF16))
            m = m_new

        o_t = acc / l
        o = jnp.concatenate(
            [jnp.concatenate([o_t[:, (2 * i) * blk:(2 * i + 1) * blk], o_t[:, (2 * i + 1) * blk:(2 * i + 2) * blk]],
                             axis=0).T for i in range(DSA_HEADS // 2)], axis=1)
        o_ref[qrows, :] = (o * gate).astype(o_ref.dtype)

    for n in range(1, seq // kb_rows + 1):
        pl.when(n_kb == n)(functools.partial(step, n))


def _dsa(z2d, batch, seq, c64, s64, c32, s32):
    blk = DSA_QUERY_BLOCK
    qspec = lambda u2: pl.BlockSpec((seq, 2 * LANES), lambda b, u2=u2: (b, u2))
    seqspec = lambda u: pl.BlockSpec((seq, LANES), lambda b, u=u: (b, u))
    tab = pl.BlockSpec((seq, LANES), lambda b: (0, 0))
    return pl.pallas_call(
        _dsa_kernel,
        grid=(batch,),
        in_specs=[qspec(U_DS_Q // 2), qspec(U_DS_G // 2), qspec(U_DS_IQ // 2),
                  seqspec(U_DS_KV), seqspec(U_DS_MISC), tab, tab, tab, tab],
        out_specs=pl.BlockSpec((seq, DSA_WIDTH), lambda b: (b, 0)),
        out_shape=jax.ShapeDtypeStruct((batch * seq, DSA_WIDTH), BF16),
        scratch_shapes=[
            pltpu.VMEM((seq, HEAD_DIM), BF16),
            pltpu.VMEM((seq // DSA_KEY_BLOCK, HEAD_DIM, DSA_KEY_BLOCK), BF16),
            pltpu.VMEM((seq, IDX_DIM), BF16),
            pltpu.VMEM((seq, blk), I32),
            pltpu.VMEM((seq // BIT_GROUP_ROWS, 32, 8, blk), I32),
        ],
        compiler_params=pltpu.CompilerParams(
            dimension_semantics=("parallel",), vmem_limit_bytes=VMEM_LIMIT),
        name="dsa",
    )(z2d, z2d, z2d, z2d, z2d, c64, s64, c32, s32)


def _dil_group(gi, rate, q_ref, k_ref, v_ref, g_ref, c64_ref, s64_ref, qr_ref, kr_ref, o_s, lse_s):
    seq = q_ref.shape[0]
    blk = ATT_BLOCK
    rope_rows = 4 * blk
    nrow = seq // rope_rows

    def rope_body(t, carry):
        rows = pl.ds(pl.multiple_of(t * rope_rows, rope_rows), rope_rows)
        ct = c64_ref[rows, :]
        st = s64_ref[rows, :]
        q = q_ref[rows, :]
        k = k_ref[rows, :]
        qr_ref[rows, :] = (q * ct + pltpu.roll(q, HEAD_DIM, 1) * st) * QK_SCALE_LOG2
        kr_ref[rows, :] = k * ct + pltpu.roll(k, HEAD_DIM, 1) * st
        return carry

    lax.fori_loop(0, nrow, rope_body, 0)

    nblk = seq // (blk * rate)
    qi = lax.broadcasted_iota(I32, (blk, 2 * blk), 0)
    kj = lax.broadcasted_iota(I32, (blk, 2 * blk), 1)
    band = (kj >= qi) & (kj <= qi + DIL_STEPS)

    def strided(ref, start):
        if rate == 1:
            return ref[pl.ds(start, blk), :]
        return ref[pl.ds(start, blk, stride=rate), :]

    def store(ref, start, val):
        if rate == 1:
            ref[gi, pl.ds(start, blk), :] = val
        else:
            ref[gi, pl.ds(start, blk, stride=rate), :] = val

    first_q = lax.broadcasted_iota(I32, (blk, LANES), 1) < HEAD_DIM
    first_kv = lax.broadcasted_iota(I32, (2 * blk, LANES), 1) < HEAD_DIM
    lane64 = lax.broadcasted_iota(I32, (blk, LANES), 1) % HEAD_DIM
    rest = (HEAD_DIM - 2 * ROT_HALF) // 2
    first_qk = (lane64 < ROT_HALF) | ((lane64 >= 2 * ROT_HALF) & (lane64 < 2 * ROT_HALF + rest))
    unroll = DIL_UNROLL if rate == max(DIL_RATES) else 2 * DIL_UNROLL

    def body(it, carry):
        starts, qs, ks, vs, masks = [], [], [], [], []
        for u in range(unroll):
            cidx = it * unroll + u
            rho = cidx // nblk
            j = cidx % nblk
            start = j * (blk * rate) + rho
            pstart = jnp.maximum(j - 1, 0) * (blk * rate) + rho
            starts.append(start)
            qs.append(strided(qr_ref, start))
            ks.append(jnp.concatenate([strided(kr_ref, pstart), strided(kr_ref, start)], axis=0).astype(BF16))
            vs.append(jnp.concatenate([strided(v_ref, pstart), strided(v_ref, start)], axis=0))
            masks.append(band & ((kj >= blk) | (j > 0)))
        prob = [(u, h) for u in range(unroll) for h in range(2)]
        qh = [jnp.where(first_qk == (h == 0), qs[u], 0.0).astype(BF16) for u, h in prob]
        vh = [jnp.where(first_kv == (h == 0), vs[u], 1.0).astype(BF16) for u, h in prob]
        s = [jnp.where(masks[u], _dot_nt(qh[i], ks[u]), -jnp.inf) for i, (u, h) in enumerate(prob)]
        m = [jnp.max(x, axis=-1, keepdims=True) for x in s]
        r = [_dot(jnp.exp2(s[i] - m[i]).astype(BF16), vh[i]) for i in range(len(prob))]
        for u in range(unroll):
            r0, r1 = r[2 * u], r[2 * u + 1]
            o = jnp.where(first_q, r0, r1)
            l = pltpu.roll(jnp.where(first_q, r1, r0), HEAD_DIM, 1)
            lse = jnp.where(first_q, m[2 * u], m[2 * u + 1]) * math.log(2.0) + jnp.log(l)
            gate = strided(g_ref, starts[u])
            store(o_s, starts[u], o / l * (gate * _sigmoid(gate)))
            store(lse_s, starts[u], lse)
        return carry

    lax.fori_loop(0, rate * nblk // unroll, body, 0)


def _dil_kernel(q_ref, k_ref, v_ref, g_ref, c64_ref, s64_ref, o_ref, qr_ref, kr_ref, o_s, lse_s):
    grp = pl.program_id(1)
    for gi, rate in enumerate(DIL_RATES):
        pl.when(grp == gi)(functools.partial(
            _dil_group, gi, rate, q_ref, k_ref, v_ref, g_ref, c64_ref, s64_ref, qr_ref, kr_ref, o_s, lse_s))

    @pl.when(grp == len(DIL_RATES) - 1)
    def _():
        blk = ATT_BLOCK

        def body(t, carry):
            rows = pl.ds(pl.multiple_of(t * blk, blk), blk)
            ls = [lse_s[i, rows, :] for i in range(3)]
            top = jnp.maximum(jnp.maximum(ls[0], ls[1]), ls[2])
            es = [jnp.exp(x - top) for x in ls]
            inv = 1.0 / (es[0] + es[1] + es[2])
            o_ref[rows, :] = jnp.concatenate(
                [o_s[i, rows, :] * (es[i] * inv) for i in range(3)], axis=1).astype(o_ref.dtype)
            return carry

        lax.fori_loop(0, q_ref.shape[0] // blk, body, 0)


def _dilated(z2d, batch, seq, c64, s64):
    ng = len(DIL_RATES)
    zspec = lambda base: pl.BlockSpec((seq, LANES), lambda b, g, base=base: (b, base + g))
    tab = pl.BlockSpec((seq, LANES), lambda b, g: (0, 0))
    return pl.pallas_call(
        _dil_kernel,
        grid=(batch, ng),
        in_specs=[zspec(U_DL), zspec(U_DL + 3), zspec(U_DL + 6), zspec(U_DL + 9), tab, tab],
        out_specs=pl.BlockSpec((seq, DIL_WIDTH), lambda b, g: (b, 0)),
        out_shape=jax.ShapeDtypeStruct((batch * seq, DIL_WIDTH), BF16),
        scratch_shapes=[
            pltpu.VMEM((seq, LANES), F32),
            pltpu.VMEM((seq, LANES), F32),
            pltpu.VMEM((ng, seq, LANES), F32),
            pltpu.VMEM((ng, seq, LANES), F32),
        ],
        compiler_params=pltpu.CompilerParams(
            dimension_semantics=("parallel", "arbitrary"), vmem_limit_bytes=VMEM_LIMIT),
        name="dilated",
    )(z2d, z2d, z2d, z2d, c64, s64)


def _rope_tables(seq, head, half):
    inv = ROPE_THETA ** (-jnp.arange(half, dtype=F32) / half)
    ang = jnp.arange(seq, dtype=F32)[:, None] * inv[None, :]
    cos, sin = jnp.cos(ang), jnp.sin(ang)
    lane = jnp.arange(LANES) % head
    idx = lane % half
    cos_t = jnp.where(lane[None, :] < 2 * half, cos[:, idx], 1.0)
    sin_t = jnp.where(lane[None, :] < half, -sin[:, idx], jnp.where(lane[None, :] < 2 * half, sin[:, idx], 0.0))
    return cos_t, sin_t


def _dil_pair_lanes():
    half, rest = ROT_HALF, (HEAD_DIM - 2 * ROT_HALF) // 2
    src = []
    for part in range(2):
        for head in range(2):
            src += [head * HEAD_DIM + part * half + i for i in range(half)]
        for head in range(2):
            src += [head * HEAD_DIM + 2 * half + part * rest + i for i in range(rest)]
    return src


def _rope_tables_paired(seq):
    half = ROT_HALF
    inv = ROPE_THETA ** (-jnp.arange(half, dtype=F32) / half)
    ang = jnp.arange(seq, dtype=F32)[:, None] * inv[None, :]
    cos, sin = jnp.cos(ang), jnp.sin(ang)
    lane = jnp.arange(LANES)[None, :]
    idx = jnp.arange(LANES) % half
    rotated = lane % HEAD_DIM < 2 * half
    cos_t = jnp.where(rotated, cos[:, idx], 1.0)
    sin_t = jnp.where(rotated, jnp.where(lane < HEAD_DIM, -sin[:, idx], sin[:, idx]), 0.0)
    return cos_t, sin_t


def _in_col_sources():
    rw = RWKV_WIDTH
    cols = lambda o, wdt: list(range(o, o + wdt))
    o = 0
    r, k, v, g = (cols(o + i * rw, rw) for i in range(4))
    o = 4 * rw
    wd, ad = cols(o, LORA_RANK), cols(o + LORA_RANK, LORA_RANK)
    o += 2 * LORA_RANK
    parts = []
    for wdt in (DSA_WIDTH, HEAD_DIM, HEAD_DIM, IDX_HEADS * IDX_DIM, IDX_DIM, IDX_HEADS, DSA_WIDTH):
        parts.append(cols(o, wdt))
        o += wdt
    dq, dk, dv, diq, dik, diw, dg = parts
    cq, ck, cv, cg = (cols(o + i * DIL_WIDTH, DIL_WIDTH) for i in range(4))
    o += 4 * DIL_WIDTH
    pair = [t * LANES + s for t in range(DIL_WIDTH // LANES) for s in _dil_pair_lanes()]
    cq, ck = [cq[i] for i in pair], [ck[i] for i in pair]
    pad = [o] * (LANES - IDX_DIM - IDX_HEADS)
    src = r + k + v + g + cq + ck + cv + cg + dq + dg + diq + dk + dv + dik + diw + pad + wd + ad
    assert len(src) == Z_COLS
    return src


def _permute_in_cols(w):
    return jnp.take(w, jnp.asarray(_in_col_sources(), I32), axis=-1, mode="fill", fill_value=0)


def kernel(x, p, norm_g, w_in, tshift_mu, rwkv_w0, rwkv_w_up, rwkv_a0, rwkv_a_up, rwkv_k_k, rwkv_k_a, rwkv_r_k,
           rwkv_ln_g, rwkv_ln_b, w_out, ple_norm_g, ple_w_gate, ple_w_proj, final_norm_g):
    batch, seq, d_model = x.shape
    depth = w_in.shape[0]
    n = batch * seq
    assert seq % (ATT_BLOCK * max(DIL_RATES)) == 0 and seq >= 4 * DSA_TOPK

    c64, s64 = _rope_tables(seq, HEAD_DIM, ROT_HALF)
    c32, s32 = _rope_tables(seq, IDX_DIM, IDX_ROT_HALF)
    c64p, s64p = _rope_tables_paired(seq)
    w_in_k = _permute_in_cols(w_in.astype(BF16))
    rw = RWKV_WIDTH
    row = lambda a: a.reshape(1, -1)

    x2d = x.reshape(n, d_model)
    p3d = p.reshape(depth, n, p.shape[-1])
    for i in range(depth):
        z = _in_proj(x2d, row(norm_g[i]), w_in_k, i)
        mu = tshift_mu[i]
        y_a = _rwkv(z.reshape(batch, seq, Z_COLS), mu[:4 * rw].reshape(4, rw), row(mu[4 * rw:]), row(rwkv_w0[i]), rwkv_w_up[i],
                    row(rwkv_a0[i]), rwkv_a_up[i], row(rwkv_k_k[i]), row(rwkv_k_a[i]), row(rwkv_r_k[i]),
                    row(rwkv_ln_g[i]), row(rwkv_ln_b[i])).reshape(n, rw)
        y_b = _dsa(z, batch, seq, c64, s64, c32, s32)
        y_c = _dilated(z, batch, seq, c64p, s64p)
        wo = w_out[i].astype(BF16)
        x2d = _out_proj(x2d, y_a, y_b, y_c, p3d, i, wo[:rw], wo[rw:rw + DSA_WIDTH],
                        wo[rw + DSA_WIDTH:], row(ple_norm_g[i]), ple_w_gate[i].astype(BF16),
                        ple_w_proj[i].astype(BF16), row(final_norm_g), final=(i == depth - 1))
    return x2d.reshape(batch, seq, d_model)
```
